```python
import math
import jax, jax.numpy as jnp
from jax import lax
import numpy as np

D_MODEL = 2048
BATCH = 4
SEQ = 2048
DEPTH = 2

CTX_LEN = 256
GRID_W = 64
HEAD_DIM = 64
GROUP_WIDTH = D_MODEL // 4
A_HEADS = GROUP_WIDTH // HEAD_DIM
A_DK = HEAD_DIM
A_DV = HEAD_DIM
B_HEADS = GROUP_WIDTH // HEAD_DIM
B_DK = HEAD_DIM
B_DV = HEAD_DIM
C_DQK = HEAD_DIM
C_DV = 2 * HEAD_DIM
C_HEADS = GROUP_WIDTH // C_DV
D_WIDTH = GROUP_WIDTH
D_BLOCKS = 8
D_CONV = 4
D_CONV_LEFT = 1
RG_C = 8.0
MIX_WIDTH = A_HEADS * A_DV + B_HEADS * B_DV + C_HEADS * C_DV + D_WIDTH
IN_SIZES = (A_HEADS * A_DK, A_HEADS * A_DK, A_HEADS * A_DK, A_HEADS * A_DV, A_HEADS * A_DV,
            B_HEADS * B_DK, B_HEADS * B_DK, B_HEADS * B_DV, B_HEADS * B_DV,
            C_HEADS * 2 * C_DQK, C_HEADS * 2 * C_DQK, C_HEADS * C_DV,
            D_WIDTH, D_WIDTH)
IN_WIDTH = sum(IN_SIZES)
D_FF = ((8 * D_MODEL // 3 + 127) // 128) * 128
FFN_RESIDUAL = 0.5
N_MOD = 9
CHUNK = 16
Q_BLOCK = 128
ROPE_BASE = 10000.0
EPS = 1e-6

kernel_name = "hybrid_parallel_group_dit_trunk"


def rms_norm(t, g):
    tf = t.astype(jnp.float32)
    y = tf * lax.rsqrt(jnp.mean(tf * tf, axis=-1, keepdims=True) + EPS)
    return (y * g.astype(jnp.float32)).astype(t.dtype)


def modulate(t, shift, scale):
    return t * (1.0 + scale) + shift


def maybe_flip(t, axis, rev):
    return jnp.flip(t, axis=axis) if rev else t


def to_heads(t, h):
    b, n, _ = t.shape
    return t.reshape(b, n, h, -1).transpose(0, 2, 1, 3)


def from_heads(t):
    b, h, n, d = t.shape
    return t.transpose(0, 2, 1, 3).reshape(b, n, h * d)


def split_cols(z):
    return jnp.split(z, np.cumsum(IN_SIZES)[:-1].tolist(), axis=-1)


def axial_rope_tables(rows, head_dim):
    quarter = head_dim // 4
    inv_freq = ROPE_BASE ** (-jnp.arange(quarter, dtype=jnp.float32) / quarter)
    row = jnp.repeat(jnp.arange(rows, dtype=jnp.float32), GRID_W)
    col = jnp.tile(jnp.arange(GRID_W, dtype=jnp.float32), rows)
    ang = jnp.concatenate([row[:, None] * inv_freq, col[:, None] * inv_freq], axis=-1)
    return jnp.cos(ang), jnp.sin(ang)


def apply_rope(t, cos, sin):
    tf = t.astype(jnp.float32)
    t1, t2 = jnp.split(tf, 2, axis=-1)
    out = jnp.concatenate([t1 * cos - t2 * sin, t2 * cos + t1 * sin], axis=-1)
    return out.astype(t.dtype)


def lower_bounds_from_logits(lb_logits):
    cum = jnp.cumsum(jax.nn.softmax(lb_logits.astype(jnp.float32), axis=0), axis=0)
    return cum - cum[:1]


def gla_chunkwise(q, k, v, log_f, s0):
    out_dtype = v.dtype
    q, k, v, log_f = (t.astype(jnp.float32) for t in (q, k, v, log_f))
    b_, h, n_tok, dk = q.shape
    dv = v.shape[-1]
    n = n_tok // CHUNK
    rs = lambda t: t.reshape(b_, h, n, CHUNK, t.shape[-1])
    q, k, v, log_f = rs(q), rs(k), rs(v), rs(log_f)
    b = jnp.cumsum(log_f, axis=3)
    b_last = b[:, :, :, -1:]
    mask = jnp.tril(jnp.ones((CHUNK, CHUNK), bool))
    diff = b[:, :, :, :, None, :] - b[:, :, :, None, :, :]
    decay = jnp.exp(jnp.where(mask[:, :, None], diff, -jnp.inf))
    scores = jnp.einsum('bhnid,bhnjd,bhnijd->bhnij', q, k, decay)
    o_intra = jnp.einsum('bhnij,bhnjv->bhniv', scores, v)
    u = jnp.einsum('bhnjd,bhnjv->bhndv', k * jnp.exp(b_last - b), v)
    g = jnp.exp(b_last[:, :, :, 0])

    def step(s, xs):
        g_n, u_n = xs
        return g_n[..., None] * s + u_n, s

    s_final, s_prev = lax.scan(step, s0.astype(jnp.float32),
                               (jnp.moveaxis(g, 2, 0), jnp.moveaxis(u, 2, 0)))
    s_prev = jnp.moveaxis(s_prev, 0, 2)
    o_inter = jnp.einsum('bhnid,bhndv->bhniv', q * jnp.exp(b), s_prev)
    o = (o_intra + o_inter).reshape(b_, h, n_tok, dv)
    return o.astype(out_dtype), s_final


def bidirectional_gla(q_c, v_c, kf_c, q_x, v_x, kf_x):
    b_, h, _, dk = q_c.shape
    dv = v_c.shape[-1]
    o_c, o_x = 0.0, 0.0
    for d in range(2):
        rev = d == 1
        (k_c, lf_c), (k_x, lf_x) = kf_c[d], kf_x[d]
        s0 = jnp.zeros((b_, h, dk, dv), jnp.float32)
        oc, s_ctx = gla_chunkwise(maybe_flip(q_c, 2, rev), maybe_flip(k_c, 2, rev),
                                  maybe_flip(v_c, 2, rev), maybe_flip(lf_c, 2, rev), s0)
        ox, _ = gla_chunkwise(maybe_flip(q_x, 2, rev), maybe_flip(k_x, 2, rev),
                              maybe_flip(v_x, 2, rev), maybe_flip(lf_x, 2, rev), s_ctx)
        o_c = o_c + maybe_flip(oc, 2, rev)
        o_x = o_x + maybe_flip(ox, 2, rev)
    return o_c, o_x


def hgrn2_group(zc, zx, lower_bound, gain):
    def forget_key(zf, lb):
        lb_h = lb.astype(jnp.float32).reshape(A_HEADS, 1, A_DK)
        f = lb_h + (1.0 - lb_h) * jax.nn.sigmoid(to_heads(zf, A_HEADS).astype(jnp.float32))
        return (1.0 - f, jnp.log(f))

    def prep(z):
        zq, zff, zfb, zv, zg = z
        q = jax.nn.silu(to_heads(zq, A_HEADS))
        v = to_heads(zv, A_HEADS)
        kf = [forget_key(zff, lower_bound[0]), forget_key(zfb, lower_bound[1])]
        return q, v, kf, zg

    q_c, v_c, kf_c, g_c = prep(zc)
    q_x, v_x, kf_x, g_x = prep(zx)
    o_c, o_x = bidirectional_gla(q_c, v_c, kf_c, q_x, v_x, kf_x)
    out = lambda o, g: from_heads(rms_norm(o, gain)) * jax.nn.silu(g)
    return out(o_c, g_c), out(o_x, g_x)


def retention_group(zc, zx, gain, rope):
    log_decay = jnp.log1p(-jnp.exp2(-5.0 - jnp.arange(B_HEADS, dtype=jnp.float32)))

    def prep(z, use_rope):
        zq, zk, zv, zg = z
        q = to_heads(zq, B_HEADS)
        k = to_heads(zk, B_HEADS) * (B_DK ** -0.5)
        if use_rope:
            q, k = apply_rope(q, *rope), apply_rope(k, *rope)
        lf = jnp.broadcast_to(log_decay[:, None, None], q.shape)
        return q, to_heads(zv, B_HEADS), [(k, lf), (k, lf)], zg

    q_c, v_c, kf_c, g_c = prep(zc, False)
    q_x, v_x, kf_x, g_x = prep(zx, True)
    o_c, o_x = bidirectional_gla(q_c, v_c, kf_c, q_x, v_x, kf_x)
    out = lambda o, g: from_heads(rms_norm(o, gain)) * jax.nn.silu(g)
    return out(o_c, g_c), out(o_x, g_x)


def diff_attend(q1, q2, k1, k2, v, lam):
    scale = C_DQK ** -0.5
    s1 = jnp.einsum('bhqd,bhkd->bhqk', q1, k1).astype(jnp.float32) * scale
    s2 = jnp.einsum('bhqd,bhkd->bhqk', q2, k2).astype(jnp.float32) * scale
    a = jax.nn.softmax(s1, axis=-1) - lam * jax.nn.softmax(s2, axis=-1)
    return jnp.einsum('bhqk,bhkv->bhqv', a.astype(v.dtype), v)


def blocked_diff_attention(q1, q2, k1, k2, v, lam):
    b_, h, n_tok, d = q1.shape
    n = n_tok // Q_BLOCK
    qb = lambda t: jnp.moveaxis(t.reshape(b_, h, n, Q_BLOCK, d), 2, 0)
    out = lax.map(lambda xs: diff_attend(xs[0], xs[1], k1, k2, v, lam), (qb(q1), qb(q2)))
    return jnp.moveaxis(out, 0, 2).reshape(b_, h, n_tok, -1)


def diff_attn_group(zc, zx, lam_vecs, gain, rope, layer_idx, ctx_out):
    lam_init = 0.8 - 0.6 * math.exp(-0.3 * layer_idx)
    lv = lam_vecs.astype(jnp.float32)
    lam = jnp.exp(jnp.sum(lv[0] * lv[1])) - jnp.exp(jnp.sum(lv[2] * lv[3])) + lam_init

    def prep(z):
        zq, zk, zv = z
        q1, q2 = jnp.split(to_heads(zq, C_HEADS), 2, axis=-1)
        k1, k2 = jnp.split(to_heads(zk, C_HEADS), 2, axis=-1)
        return q1, q2, k1, k2, to_heads(zv, C_HEADS)

    q1c, q2c, k1c, k2c, vc = prep(zc)
    q1x, q2x, k1x, k2x, vx = prep(zx)
    q1x, q2x, k1x, k2x = (apply_rope(t, *rope) for t in (q1x, q2x, k1x, k2x))
    k1 = jnp.concatenate([k1c, k1x], axis=2)
    k2 = jnp.concatenate([k2c, k2x], axis=2)
    v = jnp.concatenate([vc, vx], axis=2)
    out = lambda o: from_heads(rms_norm(o, gain) * (1.0 - lam_init))
    o_x = out(blocked_diff_attention(q1x, q2x, k1, k2, v, lam))
    o_c = out(diff_attend(q1c, q2c, k1c, k2c, vc, lam)) if ctx_out else None
    return o_c, o_x


def depthwise_conv(t, w, b):
    out = lax.conv_general_dilated(t, w[:, None, :].astype(t.dtype), window_strides=(1,),
                                   padding=((D_CONV_LEFT, D_CONV - 1 - D_CONV_LEFT),),
                                   dimension_numbers=('NWC', 'WIO', 'NWC'),
                                   feature_group_count=t.shape[-1])
    return out + b


def block_diag_linear(t, w, b):
    b_, n, ch = t.shape
    y = jnp.einsum('btgi,gio->btgo', t.reshape(b_, n, w.shape[0], -1), w.astype(t.dtype))
    return y.reshape(b_, n, ch) + b


def rglru_coeffs(t, w_r, b_r, w_i, b_i, lam):
    t = t.astype(jnp.float32)
    r = jax.nn.sigmoid(block_diag_linear(t, w_r, b_r))
    i = jax.nn.sigmoid(block_diag_linear(t, w_i, b_i))
    log_a = -RG_C * r * jax.nn.softplus(-lam.astype(jnp.float32))
    return jnp.exp(log_a), jnp.sqrt(-jnp.expm1(2.0 * log_a)) * (i * t)


def linear_scan(a, u, h0):
    u = u.at[:, 0].add(a[:, 0] * h0)
    combine = lambda l, r: (r[0] * l[0], r[0] * l[1] + r[1])
    _, h = lax.associative_scan(combine, (a, u), axis=1)
    return h


def rglru_group(zc, zx, p):
    (xc, gc), (xx, gx) = zc, zx
    xc = depthwise_conv(xc, p['d_conv_w'], p['d_conv_b'])
    xx = depthwise_conv(xx, p['d_conv_w'], p['d_conv_b'])
    h_c_sum, h_x_sum = 0.0, 0.0
    for d in range(2):
        rev = d == 1
        coeffs = lambda t: rglru_coeffs(maybe_flip(t, 1, rev), p['d_w_r'][d], p['d_b_r'][d],
                                        p['d_w_i'][d], p['d_b_i'][d], p['d_lambda'][d])
        a_c, u_c = coeffs(xc)
        h_c = linear_scan(a_c, u_c, jnp.zeros((xc.shape[0], D_WIDTH), jnp.float32))
        a_x, u_x = coeffs(xx)
        h_x = linear_scan(a_x, u_x, h_c[:, -1])
        h_c_sum = h_c_sum + maybe_flip(h_c, 1, rev)
        h_x_sum = h_x_sum + maybe_flip(h_x, 1, rev)
    return (jax.nn.gelu(gc) * h_c_sum.astype(gc.dtype), jax.nn.gelu(gx) * h_x_sum.astype(gx.dtype))


def token_mixer(uc, ux, p, rope, layer_idx, ctx_out):
    zc = split_cols(uc @ p['w_in'])
    zx = split_cols(ux @ p['w_in'])
    a_c, a_x = hgrn2_group(zc[0:5], zx[0:5], p['lower_bound'], p['a_norm'])
    b_c, b_x = retention_group(zc[5:9], zx[5:9], p['b_norm'], rope)
    c_c, c_x = diff_attn_group(zc[9:12], zx[9:12], p['c_lambda'], p['c_norm'], rope, layer_idx, ctx_out)
    d_c, d_x = rglru_group(zc[12:14], zx[12:14], p)
    y_x = jnp.concatenate([a_x, b_x, c_x, d_x], axis=-1) @ p['w_out']
    y_c = jnp.concatenate([a_c, b_c, c_c, d_c], axis=-1) @ p['w_out'] if ctx_out else None
    return y_c, y_x


def ffn_sublayer(h, mod, p, ffn_idx, norm_idx):
    shift, scale, gate = mod
    u = modulate(rms_norm(h, p['norm_pre'][norm_idx]), shift, scale)
    g, up = jnp.split(u @ p['ffn_w_in'][ffn_idx], 2, axis=-1)
    y = (jax.nn.silu(g) * up) @ p['ffn_w_out'][ffn_idx]
    return h + FFN_RESIDUAL * gate * rms_norm(y, p['norm_post'][norm_idx])


def hybrid_layer(hc, hx, c_silu, cc_silu, p, rope, layer_idx, ctx_out):
    mod_x = jnp.split((c_silu @ p['w_ada'] + p['b_ada'])[:, None, :], N_MOD, axis=-1)
    mod_c = jnp.split((cc_silu @ p['w_ada'] + p['b_ada'])[None, None, :], N_MOD, axis=-1)
    hc = ffn_sublayer(hc, mod_c[0:3], p, 0, 0)
    hx = ffn_sublayer(hx, mod_x[0:3], p, 0, 0)
    uc = modulate(rms_norm(hc, p['norm_pre'][1]), mod_c[3], mod_c[4])
    ux = modulate(rms_norm(hx, p['norm_pre'][1]), mod_x[3], mod_x[4])
    y_c, y_x = token_mixer(uc, ux, p, rope, layer_idx, ctx_out)
    hx = hx + mod_x[5] * rms_norm(y_x, p['norm_post'][1])
    hx = ffn_sublayer(hx, mod_x[6:9], p, 1, 2)
    if ctx_out:
        hc = hc + mod_c[5] * rms_norm(y_c, p['norm_post'][1])
        hc = ffn_sublayer(hc, mod_c[6:9], p, 1, 2)
    else:
        hc = None
    return hc, hx


def setup_inputs(seed: int = 0) -> dict:
    key = jax.random.key(seed)
    ks = jax.random.split(key, 24)
    f32 = jnp.float32
    nrm = lambda k, shape, scale: jax.random.normal(k, shape, f32) * scale
    gain = lambda k, shape: 1.0 + 0.05 * jax.random.normal(k, shape, f32)
    bs = D_WIDTH // D_BLOCKS
    a0 = jax.random.uniform(ks[23], (DEPTH, 2, D_WIDTH), f32, 0.9, 0.999) ** (1.0 / RG_C)
    return {
        "x": nrm(ks[0], (BATCH, SEQ, D_MODEL), 1.0),
        "c": nrm(ks[1], (BATCH, D_MODEL), 1.0),
        "ctx": nrm(ks[2], (BATCH, CTX_LEN, D_MODEL), 1.0),
        "c_ctx": nrm(ks[3], (D_MODEL,), 1.0),
        "w_ada": nrm(ks[4], (DEPTH, D_MODEL, N_MOD * D_MODEL), 0.5 * D_MODEL ** -0.5),
        "b_ada": nrm(ks[5], (DEPTH, N_MOD * D_MODEL), 0.02),
        "norm_pre": gain(ks[6], (DEPTH, 3, D_MODEL)),
        "norm_post": gain(ks[7], (DEPTH, 3, D_MODEL)),
        "ffn_w_in": nrm(ks[8], (DEPTH, 2, D_MODEL, 2 * D_FF), D_MODEL ** -0.5),
        "ffn_w_out": nrm(ks[9], (DEPTH, 2, D_FF, D_MODEL), D_FF ** -0.5),
        "w_in": nrm(ks[10], (DEPTH, D_MODEL, IN_WIDTH), D_MODEL ** -0.5),
        "w_out": nrm(ks[11], (DEPTH, MIX_WIDTH, D_MODEL), MIX_WIDTH ** -0.5),
        "lb_logits": nrm(ks[12], (DEPTH, 2, A_HEADS * A_DK), 1.0),
        "a_norm": gain(ks[13], (DEPTH, A_DV)),
        "b_norm": gain(ks[14], (DEPTH, B_DV)),
        "c_lambda": nrm(ks[15], (DEPTH, 4, C_DQK), 0.1),
        "c_norm": gain(ks[16], (DEPTH, C_DV)),
        "d_conv_w": nrm(ks[17], (DEPTH, D_CONV, D_WIDTH), D_CONV ** -0.5),
        "d_conv_b": nrm(ks[18], (DEPTH, D_WIDTH), 0.02),
        "d_w_r": nrm(ks[19], (DEPTH, 2, D_BLOCKS, bs, bs), bs ** -0.5),
        "d_b_r": nrm(ks[20], (DEPTH, 2, D_WIDTH), 0.02),
        "d_w_i": nrm(ks[21], (DEPTH, 2, D_BLOCKS, bs, bs), bs ** -0.5),
        "d_b_i": nrm(ks[22], (DEPTH, 2, D_WIDTH), 0.02),
        "d_lambda": jnp.log(a0) - jnp.log1p(-a0),
    }


def reference(x, c, ctx, c_ctx, w_ada, b_ada, norm_pre, norm_post, ffn_w_in, ffn_w_out,
              w_in, w_out, lb_logits, a_norm, b_norm, c_lambda, c_norm, d_conv_w, d_conv_b,
              d_w_r, d_b_r, d_w_i, d_b_i, d_lambda):
    rows = x.shape[1] // GRID_W
    rope = axial_rope_tables(rows, HEAD_DIM)
    lower_bounds = lower_bounds_from_logits(lb_logits)
    c_silu = jax.nn.silu(c)
    cc_silu = jax.nn.silu(c_ctx)
    hc, hx = ctx, x
    for l in range(DEPTH):
        p = dict(w_ada=w_ada[l], b_ada=b_ada[l], norm_pre=norm_pre[l], norm_post=norm_post[l],
                 ffn_w_in=ffn_w_in[l], ffn_w_out=ffn_w_out[l], w_in=w_in[l], w_out=w_out[l],
                 lower_bound=lower_bounds[l], a_norm=a_norm[l], b_norm=b_norm[l],
                 c_lambda=c_lambda[l], c_norm=c_norm[l], d_conv_w=d_conv_w[l], d_conv_b=d_conv_b[l],
                 d_w_r=d_w_r[l], d_b_r=d_b_r[l], d_w_i=d_w_i[l], d_b_i=d_b_i[l], d_lambda=d_lambda[l])
        hc, hx = hybrid_layer(hc, hx, c_silu, cc_silu, p, rope, l, l < DEPTH - 1)
    return hx
```

```python
import functools
import math

import numpy as np
import jax
import jax.numpy as jnp
from jax import lax
from jax.experimental import pallas as pl
from jax.experimental.pallas import tpu as pltpu

F32 = jnp.float32
BF16 = jnp.bfloat16

D_MODEL = 2048
SEQ = 2048
CTX_LEN = 256
DEPTH = 2
GRID_W = 64
HEAD_DIM = 64
GROUP_WIDTH = 512
IN_WIDTH = 7168
D_FF = 5504
N_MOD = 9
ROPE_BASE = 10000.0
EPS = 1e-6
RG_C = 8.0
FFN_RESIDUAL = 0.5

LANES = 128
FF_TILE = 256
D_FF_PAD = ((D_FF + FF_TILE - 1) // FF_TILE) * FF_TILE
N_FF_TILES = D_FF_PAD // FF_TILE
MOD_ROWS = 8
GLA_CHUNK = 128
GLA_LEVELS = 7
SCAN_CHUNK = 256
VMEM_LIMIT = 56 * 1024 * 1024

G_AQ, G_AFF, G_AFB, G_AV, G_AG = 0, 1, 2, 3, 4
G_BQ, G_BK, G_BV, G_BG = 5, 6, 7, 8
G_CQ, G_CK, G_CV = 9, 10, 11
G_DX, G_DG = 12, 13
SLABS = GROUP_WIDTH // LANES


def _cparams(sem):
    return pltpu.CompilerParams(dimension_semantics=sem, vmem_limit_bytes=VMEM_LIMIT)


def _sigmoid(t):
    return jax.nn.sigmoid(t)


def _silu(t):
    return t * _sigmoid(t)


def _rms(t, gain):
    return t * lax.rsqrt(jnp.mean(t * t, axis=-1, keepdims=True) + EPS) * gain


def _dot(a, b):
    return jnp.dot(a, b, preferred_element_type=F32)


def _dot_nt(a, b):
    return lax.dot_general(a, b, (((1,), (1,)), ((), ())), preferred_element_type=F32)


def _split3(t):
    hi = t.astype(BF16)
    r1 = t - hi.astype(F32)
    mid = r1.astype(BF16)
    lo = (r1 - mid.astype(F32)).astype(BF16)
    return hi, mid, lo


def _ada_kernel(c_ref, w_ref, b_ref, o_ref):
    s = _silu(c_ref[...])
    s_hi, s_mid, s_lo = _split3(s)
    w = w_ref[...]
    w_hi = w.astype(BF16)
    r1 = w - w_hi.astype(F32)
    w_mid = r1.astype(BF16)
    w_lo = (r1 - w_mid.astype(F32)).astype(BF16)
    acc = _dot(s_hi, w_hi)
    acc += _dot(s_hi, w_mid) + _dot(s_mid, w_hi)
    acc += _dot(s_hi, w_lo) + _dot(s_mid, w_mid) + _dot(s_lo, w_hi)
    o_ref[...] = acc + b_ref[...]


def _ada_table(c_rows, w_ada, b_ada):
    tn = 1024
    n = N_MOD * D_MODEL
    return pl.pallas_call(
        _ada_kernel,
        grid=(DEPTH, n // tn),
        in_specs=[
            pl.BlockSpec((MOD_ROWS, D_MODEL), lambda l, j: (0, 0)),
            pl.BlockSpec((None, D_MODEL, tn), lambda l, j: (l, 0, j)),
            pl.BlockSpec((None, 1, tn), lambda l, j: (l, 0, j)),
        ],
        out_specs=pl.BlockSpec((None, MOD_ROWS, tn), lambda l, j: (l, 0, j)),
        out_shape=jax.ShapeDtypeStruct((DEPTH, MOD_ROWS, n), F32),
        compiler_params=_cparams(("arbitrary", "arbitrary")),
        name="ada_table",
    )(c_rows, w_ada, b_ada.reshape(DEPTH, 1, n))


ROW_CHUNK = 256


def _norm_modulate_to(u_scr, h_ref, gain_ref, mod_ref, k_shift, k_scale):
    tm = h_ref.shape[0]
    gain = gain_ref[...]
    shift = mod_ref[k_shift:k_shift + 1, :]
    scale1 = 1.0 + mod_ref[k_scale:k_scale + 1, :]

    def body(r, carry):
        r0 = pl.multiple_of(r * ROW_CHUNK, ROW_CHUNK)
        h = h_ref[pl.ds(r0, ROW_CHUNK), :]
        u_scr[pl.ds(r0, ROW_CHUNK), :] = (_rms(h, gain) * scale1 + shift).astype(BF16)
        return carry

    lax.fori_loop(0, tm // ROW_CHUNK, body, 0)


def _ffn_kernel(h_ref, mod_ref, gpre_ref, gpost_ref, win_ref, wout_ref, o_ref, u_scr, *, k0):
    j = pl.program_id(1)
    tm = h_ref.shape[0]

    @pl.when(j == 0)
    def _():
        _norm_modulate_to(u_scr, h_ref, gpre_ref, mod_ref, k0, k0 + 1)

    for r in range(tm // ROW_CHUNK):
        rows = slice(r * ROW_CHUNK, (r + 1) * ROW_CHUNK)
        gu = _dot(u_scr[rows, :], win_ref[...])
        act = (_silu(gu[:, :FF_TILE]) * gu[:, FF_TILE:]).astype(BF16)
        part = _dot(act, wout_ref[...])

        @pl.when(j == 0)
        def _():
            o_ref[rows, :] = part

        @pl.when(j > 0)
        def _():
            o_ref[rows, :] += part

    @pl.when(j == pl.num_programs(1) - 1)
    def _():
        gpost = gpost_ref[...]
        gate = FFN_RESIDUAL * mod_ref[k0 + 2:k0 + 3, :]

        def body(r, carry):
            r0 = pl.multiple_of(r * ROW_CHUNK, ROW_CHUNK)
            y = o_ref[pl.ds(r0, ROW_CHUNK), :]
            o_ref[pl.ds(r0, ROW_CHUNK), :] = h_ref[pl.ds(r0, ROW_CHUNK), :] + gate * _rms(y, gpost)
            return carry

        lax.fori_loop(0, tm // ROW_CHUNK, body, 0)


def _mod_spec(layer, row_of_tile):
    return pl.BlockSpec((None, N_MOD, D_MODEL), lambda i, *_: (layer * MOD_ROWS + row_of_tile(i), 0, 0))


def _ffn(h, mod, gpre, gpost, win, wout, *, layer, k0, row_of_tile, tm):
    n = h.shape[0]
    return pl.pallas_call(
        functools.partial(_ffn_kernel, k0=k0),
        grid=(n // tm, N_FF_TILES),
        in_specs=[
            pl.BlockSpec((tm, D_MODEL), lambda i, j: (i, 0)),
            _mod_spec(layer, row_of_tile),
            pl.BlockSpec((1, D_MODEL), lambda i, j: (0, 0)),
            pl.BlockSpec((1, D_MODEL), lambda i, j: (0, 0)),
            pl.BlockSpec((D_MODEL, 2 * FF_TILE), lambda i, j: (0, j)),
            pl.BlockSpec((FF_TILE, D_MODEL), lambda i, j: (j, 0)),
        ],
        out_specs=pl.BlockSpec((tm, D_MODEL), lambda i, j: (i, 0)),
        out_shape=jax.ShapeDtypeStruct((n, D_MODEL), F32),
        scratch_shapes=[pltpu.VMEM((tm, D_MODEL), BF16)],
        compiler_params=_cparams(("parallel", "arbitrary")),
        name="ffn",
    )(h, mod, gpre, gpost, win, wout)


def _inproj_kernel(h_ref, mod_ref, gpre_ref, w_ref, z_ref, u_scr):
    j = pl.program_id(1)
    tm = h_ref.shape[0]

    @pl.when(j == 0)
    def _():
        _norm_modulate_to(u_scr, h_ref, gpre_ref, mod_ref, 3, 4)

    for r in range(tm // ROW_CHUNK):
        rows = slice(r * ROW_CHUNK, (r + 1) * ROW_CHUNK)
        z_ref[rows, :] = _dot(u_scr[rows, :], w_ref[...])


def _inproj(h, mod, gpre, w, *, layer, row_of_tile, tm):
    n = h.shape[0]
    tn = 512
    return pl.pallas_call(
        _inproj_kernel,
        grid=(n // tm, IN_WIDTH // tn),
        in_specs=[
            pl.BlockSpec((tm, D_MODEL), lambda i, j: (i, 0)),
            _mod_spec(layer, row_of_tile),
            pl.BlockSpec((1, D_MODEL), lambda i, j: (0, 0)),
            pl.BlockSpec((D_MODEL, tn), lambda i, j: (0, j)),
        ],
        out_specs=pl.BlockSpec((tm, tn), lambda i, j: (i, j)),
        out_shape=jax.ShapeDtypeStruct((n, IN_WIDTH), F32),
        scratch_shapes=[pltpu.VMEM((tm, D_MODEL), BF16)],
        compiler_params=_cparams(("parallel", "arbitrary")),
        name="inproj",
    )(h, mod, gpre, w)


def _outproj_kernel(a_ref, b_ref, c_ref, d_ref, w_ref, h_ref, mod_ref, gpost_ref, o_ref):
    tm = h_ref.shape[0]
    gpost = gpost_ref[...]
    gate = mod_ref[5:6, :]
    for r in range(tm // ROW_CHUNK):
        rows = slice(r * ROW_CHUNK, (r + 1) * ROW_CHUNK)
        y = _dot(a_ref[rows, :], w_ref[0 * GROUP_WIDTH:1 * GROUP_WIDTH, :])
        y += _dot(b_ref[rows, :], w_ref[1 * GROUP_WIDTH:2 * GROUP_WIDTH, :])
        y += _dot(c_ref[rows, :], w_ref[2 * GROUP_WIDTH:3 * GROUP_WIDTH, :])
        y += _dot(d_ref[rows, :], w_ref[3 * GROUP_WIDTH:4 * GROUP_WIDTH, :])
        o_ref[rows, :] = h_ref[rows, :] + gate * _rms(y, gpost)


def _outproj(mix, w, h, mod, gpost, *, layer, row_of_tile, tm):
    n = h.shape[0]
    mix_spec = pl.BlockSpec((tm, GROUP_WIDTH), lambda i: (i, 0))
    return pl.pallas_call(
        _outproj_kernel,
        grid=(n // tm,),
        in_specs=[
            mix_spec, mix_spec, mix_spec, mix_spec,
            pl.BlockSpec((D_MODEL, D_MODEL), lambda i: (0, 0)),
            pl.BlockSpec((tm, D_MODEL), lambda i: (i, 0)),
            _mod_spec(layer, row_of_tile),
            pl.BlockSpec((1, D_MODEL), lambda i: (0, 0)),
        ],
        out_specs=pl.BlockSpec((tm, D_MODEL), lambda i: (i, 0)),
        out_shape=jax.ShapeDtypeStruct((n, D_MODEL), F32),
        compiler_params=_cparams(("parallel",)),
        name="outproj",
    )(*mix, w, h, mod, gpost)


def _gla_constants():
    c = GLA_CHUNK
    idx = np.arange(c)
    i, r = idx[:, None], idx[None, :]
    mats = np.zeros((2, GLA_LEVELS + 2, c, c), np.float32)
    masks = np.zeros((2, GLA_LEVELS + 1, c, c), np.float32)
    mats[0, 0] = r <= i
    mats[0, 1] = r > i
    mats[1, 0] = r >= i
    mats[1, 1] = r < i
    for m in range(GLA_LEVELS):
        hs = c >> (m + 1)
        start = (idx // (2 * hs)) * (2 * hs)
        mid = (start + hs - 1)[:, None]
        right = ((idx // hs) % 2 == 1)[:, None]
        mats[0, 2 + m] = np.where(right, (r > mid) & (r <= i), (r > i) & (r <= mid))
        mats[1, 2 + m] = np.where(right, (r > mid) & (r < i), (r >= i) & (r <= mid))
        same = (start[:, None] == start[None, :])
        q_right = right & ~right.T & same
        masks[0, m] = q_right
        masks[1, m] = q_right.T
    masks[0, GLA_LEVELS] = np.eye(c)
    masks[1, GLA_LEVELS] = np.eye(c)
    mats = mats.reshape(2, (GLA_LEVELS + 2) * c, c)
    return jnp.asarray(mats, BF16), jnp.asarray(masks, F32)


def _retention_constants():
    c = GLA_CHUNK
    heads = GROUP_WIDTH // HEAD_DIM
    log_decay = np.log1p(-np.exp2(-5.0 - np.arange(heads, dtype=np.float64)))
    ld_lane = np.repeat(log_decay, HEAD_DIM).reshape(SLABS, 1, LANES)
    idx = np.arange(c, dtype=np.float64)
    i, j = idx[:, None], idx[None, :]
    dq = np.zeros((SLABS, 2, c, LANES))
    dk = np.zeros((SLABS, 2, c, LANES))
    dq[:, 0] = np.exp((idx[None, :, None] + 1.0) * ld_lane)
    dk[:, 0] = np.exp((c - 1.0 - idx[None, :, None]) * ld_lane)
    dq[:, 1] = np.exp((c - idx[None, :, None]) * ld_lane)
    dk[:, 1] = np.exp(idx[None, :, None] * ld_lane)
    gt = np.exp(c * ld_lane)
    dm = np.zeros((SLABS, 2, 2, c, c))
    for p in range(SLABS):
        for hh in range(2):
            ld = log_decay[2 * p + hh]
            dm[p, 0, hh] = np.where(i >= j, np.exp((i - j) * ld), 0.0)
            dm[p, 1, hh] = np.where(j >= i, np.exp((j - i) * ld), 0.0)
    f = lambda t: jnp.asarray(t.astype(np.float32))
    return f(dq), f(dk), f(gt), f(dm)


def _rope_constants():
    quarter = HEAD_DIM // 4
    inv_freq = ROPE_BASE ** (-jnp.arange(quarter, dtype=F32) / quarter)
    rows = SEQ // GRID_W
    row = jnp.repeat(jnp.arange(rows, dtype=F32), GRID_W)
    col = jnp.tile(jnp.arange(GRID_W, dtype=F32), rows)
    ang = jnp.concatenate([row[:, None] * inv_freq, col[:, None] * inv_freq], axis=-1)
    cos, sin = jnp.cos(ang), jnp.sin(ang)
    reps = LANES // HEAD_DIM
    return (jnp.tile(jnp.concatenate([cos, cos], axis=-1), (1, reps)),
            jnp.tile(jnp.concatenate([-sin, sin], axis=-1), (1, reps)))


def _rope(t, cos, sin):
    lane = lax.broadcasted_iota(jnp.int32, t.shape, 1)
    first_half = (lane % HEAD_DIM) < (HEAD_DIM // 2)
    partner = jnp.where(first_half,
                        pltpu.roll(t, LANES - HEAD_DIM // 2, axis=1),
                        pltpu.roll(t, HEAD_DIM // 2, axis=1))
    return t * cos + partner * sin


def _head_lane_masks():
    lane = lax.broadcasted_iota(jnp.int32, (1, LANES), 1)
    return lane < HEAD_DIM, lane >= HEAD_DIM


def _gla_chunk_update(qe, ke, v, gtot, s_heads, st):
    hm0, hm1 = _head_lane_masks()
    o = _dot_nt(qe.astype(BF16), st.astype(BF16))
    s_cat = jnp.concatenate([s.astype(BF16) for s in s_heads], axis=1)
    v_cat = jnp.concatenate([jnp.where(hm0, v, 0.0).astype(BF16),
                             jnp.where(hm1, v, 0.0).astype(BF16)], axis=0)
    o += _dot(s_cat, v_cat)
    upd = _dot(v.T.astype(BF16), ke.astype(BF16))
    row_head = lax.broadcasted_iota(jnp.int32, (LANES, LANES), 0) // HEAD_DIM
    col_head = lax.broadcasted_iota(jnp.int32, (LANES, LANES), 1) // HEAD_DIM
    st = st * gtot + jnp.where(row_head == col_head, upd, 0.0)
    return o, st


def _gla_scan(chunk_fn, acc_c, acc_x):
    n_c = acc_c.shape[0] // GLA_CHUNK
    n_x = acc_x.shape[0] // GLA_CHUNK
    for d in (0, 1):
        def run(seg, acc_ref, n, st, d=d):
            def body(t, st):
                c = t if d == 0 else n - 1 - t
                r0 = pl.multiple_of(c * GLA_CHUNK, GLA_CHUNK)
                o, st = chunk_fn(d, seg, r0, st)
                if d == 0:
                    acc_ref[pl.ds(r0, GLA_CHUNK), :] = o
                else:
                    acc_ref[pl.ds(r0, GLA_CHUNK), :] += o
                return st
            return lax.fori_loop(0, n, body, st)

        st = jnp.zeros((LANES, LANES), F32)
        st = run(0, acc_c, n_c, st)
        run(1, acc_x, n_x, st)


def _gla_finalize(acc_ref, zg_ref, gain, o_ref):
    hm0, _ = _head_lane_masks()

    def body(c, carry):
        r0 = pl.multiple_of(c * GLA_CHUNK, GLA_CHUNK)
        o = acc_ref[pl.ds(r0, GLA_CHUNK), :]
        sq = o * o
        ms0 = jnp.sum(jnp.where(hm0, sq, 0.0), axis=1, keepdims=True) * (1.0 / HEAD_DIM)
        ms1 = jnp.sum(jnp.where(hm0, 0.0, sq), axis=1, keepdims=True) * (1.0 / HEAD_DIM)
        inv = jnp.where(hm0, lax.rsqrt(ms0 + EPS), lax.rsqrt(ms1 + EPS))
        g = zg_ref[pl.ds(r0, GLA_CHUNK), :]
        o_ref[pl.ds(r0, GLA_CHUNK), :] = (o * inv * gain * _silu(g)).astype(o_ref.dtype)
        return carry

    lax.fori_loop(0, acc_ref.shape[0] // GLA_CHUNK, body, 0)


def _hgrn2_kernel(zq_c, zff_c, zfb_c, zv_c, zg_c, zq_x, zff_x, zfb_x, zv_x, zg_x,
                  lbl_ref, gain_ref, mats_ref, masks_ref, oc_ref, ox_ref, acc_c, acc_x, *, layer):
    c = GLA_CHUNK
    hm0, hm1 = _head_lane_masks()
    lbl = lbl_ref[...]
    e = jnp.exp(lbl - jnp.max(lbl, axis=0, keepdims=True))
    sm = e / jnp.sum(e, axis=0, keepdims=True)
    lower = jnp.zeros_like(sm[0])
    for l in range(1, layer + 1):
        lower = lower + sm[l]
    zq = (zq_c, zq_x)
    zf = ((zff_c, zff_x), (zfb_c, zfb_x))
    zv = (zv_c, zv_x)
    row = lax.broadcasted_iota(jnp.int32, (c, 1), 0)

    def chunk_fn(d, seg, r0, st):
        lb = lower[d:d + 1, :]
        f = lb + (1.0 - lb) * _sigmoid(zf[d][seg][pl.ds(r0, c), :])
        lf = jnp.log(f)
        k = 1.0 - f
        q = _silu(zq[seg][pl.ds(r0, c), :])
        v = zv[seg][pl.ds(r0, c), :]
        lf_hi, lf_mid, lf_lo = _split3(lf)
        m = mats_ref[d]
        ex = jnp.exp(_dot(m, lf_hi) + _dot(m, lf_mid) + _dot(m, lf_lo))
        qe = q * ex[0:c]
        ke = k * ex[c:2 * c]
        gtot = jnp.exp(jnp.sum(lf, axis=0, keepdims=True))
        s_heads = [jnp.zeros((c, c), F32), jnp.zeros((c, c), F32)]
        for lvl in range(GLA_LEVELS + 1):
            if lvl < GLA_LEVELS:
                hs = c >> (lvl + 1)
                q_side = ((row // hs) % 2) == (1 if d == 0 else 0)
                both = jnp.where(q_side, q, k) * ex[(2 + lvl) * c:(3 + lvl) * c]
                lhs_src, rhs = both, both.astype(BF16)
            else:
                lhs_src, rhs = q, k.astype(BF16)
            for hh, hm in enumerate((hm0, hm1)):
                p = _dot_nt(jnp.where(hm, lhs_src, 0.0).astype(BF16), rhs)
                s_heads[hh] = s_heads[hh] + masks_ref[d, lvl] * p
        return _gla_chunk_update(qe, ke, v, gtot, s_heads, st)

    _gla_scan(chunk_fn, acc_c, acc_x)
    gain = gain_ref[...]
    _gla_finalize(acc_c, zg_c, gain, oc_ref)
    _gla_finalize(acc_x, zg_x, gain, ox_ref)


def _z_specs(groups, slab_of):
    c_specs = [pl.BlockSpec((CTX_LEN, LANES), lambda b, p, g=g: (b, g * SLABS + slab_of(b, p)))
               for g in groups]
    x_specs = [pl.BlockSpec((SEQ, LANES), lambda b, p, g=g: (b, g * SLABS + slab_of(b, p)))
               for g in groups]
    return c_specs, x_specs


def _mix_out(batch):
    specs = [pl.BlockSpec((CTX_LEN, LANES), lambda b, p: (b, p)),
             pl.BlockSpec((SEQ, LANES), lambda b, p: (b, p))]
    shapes = [jax.ShapeDtypeStruct((batch * CTX_LEN, GROUP_WIDTH), BF16),
              jax.ShapeDtypeStruct((batch * SEQ, GROUP_WIDTH), BF16)]
    return specs, shapes


def _hgrn2(z_c, z_x, lb_logits, gain, consts, *, layer, batch):
    mats, masks = consts
    groups = (G_AQ, G_AFF, G_AFB, G_AV, G_AG)
    c_specs, x_specs = _z_specs(groups, lambda b, p: p)
    out_specs, out_shapes = _mix_out(batch)
    n_c = len(groups)
    return pl.pallas_call(
        functools.partial(_hgrn2_kernel, layer=layer),
        grid=(batch, SLABS),
        in_specs=c_specs + x_specs + [
            pl.BlockSpec((DEPTH, 2, LANES), lambda b, p: (0, 0, p)),
            pl.BlockSpec((1, LANES), lambda b, p: (0, 0)),
            pl.BlockSpec(mats.shape, lambda b, p: (0, 0, 0)),
            pl.BlockSpec(masks.shape, lambda b, p: (0, 0, 0, 0)),
        ],
        out_specs=out_specs,
        out_shape=out_shapes,
        scratch_shapes=[pltpu.VMEM((CTX_LEN, LANES), F32), pltpu.VMEM((SEQ, LANES), F32)],
        compiler_params=_cparams(("parallel", "parallel")),
        name="hgrn2",
    )(*([z_c] * n_c), *([z_x] * n_c), lb_logits, gain, mats, masks)


def _retention_kernel(zq_c, zk_c, zv_c, zg_c, zq_x, zk_x, zv_x, zg_x, cos_ref, sin_ref,
                      dq_ref, dk_ref, gt_ref, dm_ref, gain_ref, oc_ref, ox_ref, acc_c, acc_x):
    c = GLA_CHUNK
    hm0, hm1 = _head_lane_masks()
    zq, zk, zv = (zq_c, zq_x), (zk_c, zk_x), (zv_c, zv_x)
    gtot = gt_ref[...]

    def chunk_fn(d, seg, r0, st):
        q = zq[seg][pl.ds(r0, c), :]
        k = zk[seg][pl.ds(r0, c), :] * (HEAD_DIM ** -0.5)
        v = zv[seg][pl.ds(r0, c), :]
        if seg == 1:
            cos = cos_ref[pl.ds(r0, c), :]
            sin = sin_ref[pl.ds(r0, c), :]
            q, k = _rope(q, cos, sin), _rope(k, cos, sin)
        kb = k.astype(BF16)
        s_heads = [dm_ref[d, hh] * _dot_nt(jnp.where(hm, q, 0.0).astype(BF16), kb)
                   for hh, hm in enumerate((hm0, hm1))]
        return _gla_chunk_update(q * dq_ref[d], k * dk_ref[d], v, gtot, s_heads, st)

    _gla_scan(chunk_fn, acc_c, acc_x)
    gain = gain_ref[...]
    _gla_finalize(acc_c, zg_c, gain, oc_ref)
    _gla_finalize(acc_x, zg_x, gain, ox_ref)


def _retention(z_c, z_x, gain, rope, consts, *, batch):
    dq, dk, gt, dm = consts
    cos, sin = rope
    groups = (G_BQ, G_BK, G_BV, G_BG)
    c_specs, x_specs = _z_specs(groups, lambda b, p: p)
    out_specs, out_shapes = _mix_out(batch)
    n_c = len(groups)
    c = GLA_CHUNK
    return pl.pallas_call(
        _retention_kernel,
        grid=(batch, SLABS),
        in_specs=c_specs + x_specs + [
            pl.BlockSpec((SEQ, LANES), lambda b, p: (0, 0)),
            pl.BlockSpec((SEQ, LANES), lambda b, p: (0, 0)),
            pl.BlockSpec((None, 2, c, LANES), lambda b, p: (p, 0, 0, 0)),
            pl.BlockSpec((None, 2, c, LANES), lambda b, p: (p, 0, 0, 0)),
            pl.BlockSpec((None, 1, LANES), lambda b, p: (p, 0, 0)),
            pl.BlockSpec((None, 2, 2, c, c), lambda b, p: (p, 0, 0, 0, 0)),
            pl.BlockSpec((1, LANES), lambda b, p: (0, 0)),
        ],
        out_specs=out_specs,
        out_shape=out_shapes,
        scratch_shapes=[pltpu.VMEM((CTX_LEN, LANES), F32), pltpu.VMEM((SEQ, LANES), F32)],
        compiler_params=_cparams(("parallel", "parallel")),
        name="retention",
    )(*([z_c] * n_c), *([z_x] * n_c), cos, sin, dq, dk, gt, dm, gain)


ROPE_ROWS = 512


def _qkv_prep_kernel(zq_ref, zk_ref, zv_ref, cos_ref, sin_ref, q_ref, k_ref, v_ref, *, use_rope):
    for s in range(SLABS):
        cols = slice(s * LANES, (s + 1) * LANES)
        q, k = zq_ref[:, cols], zk_ref[:, cols]
        if use_rope:
            cos, sin = cos_ref[...], sin_ref[...]
            q, k = _rope(q, cos, sin), _rope(k, cos, sin)
        q_ref[:, cols] = (q * (HEAD_DIM ** -0.5)).astype(BF16)
        k_ref[:, cols] = k.astype(BF16)
    v_ref[...] = zv_ref[...].astype(BF16)


def _qkv_prep(z, rope, *, use_rope, rows):
    n = z.shape[0]
    cos, sin = rope
    pos_tiles = SEQ // rows
    zspec = lambda g: pl.BlockSpec((rows, GROUP_WIDTH), lambda i, g=g: (i, g))
    rspec = pl.BlockSpec((rows, LANES), lambda i: (i % pos_tiles, 0))
    ospec = pl.BlockSpec((rows, GROUP_WIDTH), lambda i: (i, 0))
    oshape = jax.ShapeDtypeStruct((n, GROUP_WIDTH), BF16)
    return pl.pallas_call(
        functools.partial(_qkv_prep_kernel, use_rope=use_rope),
        grid=(n // rows,),
        in_specs=[zspec(G_CQ), zspec(G_CK), zspec(G_CV), rspec, rspec],
        out_specs=[ospec, ospec, ospec],
        out_shape=[oshape, oshape, oshape],
        compiler_params=_cparams(("parallel",)),
        name="qkv_prep",
    )(z, z, z, cos, sin)


def _diff_attn_kernel(*refs, lam_init, with_latent):
    if with_latent:
        q_ref, kc_ref, vc_ref, kx_ref, vx_ref, lam_ref, gain_ref, o_ref, kk, vv = refs
    else:
        q_ref, kc_ref, vc_ref, lam_ref, gain_ref, o_ref, kk, vv = refs

    @pl.when(pl.program_id(2) == 0)
    def _():
        kk[0:CTX_LEN, :] = kc_ref[...]
        vv[0:CTX_LEN, :] = vc_ref[...]
        if with_latent:
            kk[CTX_LEN:, :] = kx_ref[...]
            vv[CTX_LEN:, :] = vx_ref[...]

    lv = lam_ref[...]
    lam = (jnp.exp(jnp.sum(lv[0:1] * lv[1:2], axis=1, keepdims=True))
           - jnp.exp(jnp.sum(lv[2:3] * lv[3:4], axis=1, keepdims=True)) + lam_init)
    q = q_ref[...]
    lane = lax.broadcasted_iota(jnp.int32, (1, LANES), 1)
    keys = kk[...]

    def softmax(qh):
        s = _dot_nt(qh, keys)
        p = jnp.exp(s - jnp.max(s, axis=-1, keepdims=True))
        return p / jnp.sum(p, axis=-1, keepdims=True)

    zero = jnp.zeros_like(q)
    a = softmax(jnp.where(lane < HEAD_DIM, q, zero)) - lam * softmax(jnp.where(lane < HEAD_DIM, zero, q))
    o = _dot(a.astype(BF16), vv[...])
    o_ref[...] = (_rms(o, gain_ref[...]) * (1.0 - lam_init)).astype(o_ref.dtype)


def _diff_attn(q, k_c, v_c, k_x, v_x, lam_vecs, gain, *, lam_init, batch, tq):
    with_latent = k_x is not None
    q_len = q.shape[0] // batch
    nq = q_len // tq
    heads = GROUP_WIDTH // LANES
    n_keys = CTX_LEN + (SEQ if with_latent else 0)
    cspec = pl.BlockSpec((CTX_LEN, LANES), lambda b, h, i: (b, h))
    xspec = pl.BlockSpec((SEQ, LANES), lambda b, h, i: (b, h))
    qspec = pl.BlockSpec((tq, LANES), lambda b, h, i: (b * nq + i, h))
    in_specs = [qspec, cspec, cspec] + ([xspec, xspec] if with_latent else []) + [
        pl.BlockSpec((4, HEAD_DIM), lambda b, h, i: (0, 0)),
        pl.BlockSpec((1, LANES), lambda b, h, i: (0, 0)),
    ]
    args = [q, k_c, v_c] + ([k_x, v_x] if with_latent else []) + [lam_vecs, gain]
    return pl.pallas_call(
        functools.partial(_diff_attn_kernel, lam_init=lam_init, with_latent=with_latent),
        grid=(batch, heads, nq),
        in_specs=in_specs,
        out_specs=qspec,
        out_shape=jax.ShapeDtypeStruct(q.shape, BF16),
        scratch_shapes=[pltpu.VMEM((n_keys, LANES), BF16), pltpu.VMEM((n_keys, LANES), BF16)],
        compiler_params=_cparams(("parallel", "parallel", "arbitrary")),
        name="diff_attn",
    )(*args)


CONV_PAD = 8


def _gelu_tanh(t):
    return 0.5 * t * (1.0 + jnp.tanh(math.sqrt(2.0 / math.pi) * (t + 0.044715 * (t * t * t))))


def _softplus(t):
    return jnp.maximum(t, 0.0) + jnp.log1p(jnp.exp(-jnp.abs(t)))


def _rglru_kernel(x_c, g_c, x_x, g_x, cw_ref, cb_ref, wr_ref, br_ref, wi_ref, bi_ref, lam_ref,
                  oc_ref, ox_ref, pad, xc, hs):
    n_tok = CTX_LEN + SEQ
    sc = SCAN_CHUNK
    cw = cw_ref[...]
    cb = cb_ref[...]
    zeros = jnp.zeros((CONV_PAD, LANES), F32)

    for x_ref, base, n in ((x_c, 0, CTX_LEN), (x_x, CTX_LEN, SEQ)):
        pad[0:CONV_PAD, :] = zeros
        pad[CONV_PAD:CONV_PAD + n, :] = x_ref[...]
        pad[CONV_PAD + n:2 * CONV_PAD + n, :] = zeros
        for r in range(n // sc):
            t0 = CONV_PAD + r * sc
            acc = cb + cw[0:1] * pad[t0 - 1:t0 - 1 + sc, :]
            acc += cw[1:2] * pad[t0:t0 + sc, :]
            acc += cw[2:3] * pad[t0 + 1:t0 + 1 + sc, :]
            acc += cw[3:4] * pad[t0 + 2:t0 + 2 + sc, :]
            xc[base + r * sc:base + (r + 1) * sc, :] = acc

    row = lax.broadcasted_iota(jnp.int32, (sc, 1), 0)
    n_chunks = n_tok // sc
    for d in (0, 1):
        wr, wi = wr_ref[d], wi_ref[d]
        br, bi = br_ref[d:d + 1, :], bi_ref[d:d + 1, :]
        neg_c_softplus = -RG_C * _softplus(-lam_ref[d:d + 1, :])

        def body(t, carry, d=d, wr=wr, wi=wi, br=br, bi=bi, neg_c_softplus=neg_c_softplus):
            if d == 0:
                ci = t
            else:
                ci = jnp.where(t == 0, 0, n_chunks - t)
            r0 = pl.multiple_of(ci * sc, sc)
            x = xc[pl.ds(r0, sc), :]
            xb = x.astype(BF16)
            r = _sigmoid(_dot(xb, wr) + br)
            i = _sigmoid(_dot(xb, wi) + bi)
            log_a = neg_c_softplus * r
            a = jnp.exp(log_a)
            u = jnp.sqrt(-jnp.tanh(log_a) * (a * a + 1.0)) * (i * x)
            s = 1
            while s < sc:
                if d == 0:
                    ok = row >= s
                    shift = s
                else:
                    ok = row < sc - s
                    shift = sc - s
                u_prev = jnp.where(ok, pltpu.roll(u, shift, axis=0), 0.0)
                a_prev = jnp.where(ok, pltpu.roll(a, shift, axis=0), 1.0)
                u = u + a * u_prev
                a = a * a_prev
                s *= 2
            h = u + a * carry
            if d == 0:
                hs[pl.ds(r0, sc), :] = h
                return h[sc - 1:sc, :]
            hs[pl.ds(r0, sc), :] += h
            return h[0:1, :]

        lax.fori_loop(0, n_chunks, body, jnp.zeros((1, LANES), F32))

    oc_ref[...] = (_gelu_tanh(g_c[...]) * hs[0:CTX_LEN, :]).astype(oc_ref.dtype)
    for r in range(SEQ // sc):
        rows = slice(r * sc, (r + 1) * sc)
        ox_ref[rows, :] = (_gelu_tanh(g_x[rows, :]) * hs[CTX_LEN + r * sc:CTX_LEN + (r + 1) * sc, :]
                           ).astype(ox_ref.dtype)


def _rglru(z_c, z_x, conv_w, conv_b, w_r, b_r, w_i, b_i, lam, *, batch):
    groups = (G_DX, G_DG)
    c_specs, x_specs = _z_specs(groups, lambda b, p: p)
    out_specs, out_shapes = _mix_out(batch)
    vec2 = pl.BlockSpec((2, LANES), lambda b, p: (0, p))
    wspec = pl.BlockSpec((2, LANES, LANES), lambda b, p: (0, p, p))
    return pl.pallas_call(
        _rglru_kernel,
        grid=(batch, SLABS),
        in_specs=c_specs + x_specs + [
            pl.BlockSpec((4, LANES), lambda b, p: (0, p)),
            pl.BlockSpec((1, LANES), lambda b, p: (0, p)),
            wspec, vec2, wspec, vec2, vec2,
        ],
        out_specs=out_specs,
        out_shape=out_shapes,
        scratch_shapes=[pltpu.VMEM((SEQ + 2 * CONV_PAD, LANES), F32),
                        pltpu.VMEM((CTX_LEN + SEQ, LANES), F32),
                        pltpu.VMEM((CTX_LEN + SEQ, LANES), F32)],
        compiler_params=_cparams(("parallel", "parallel")),
        name="rglru",
    )(z_c, z_c, z_x, z_x, conv_w, conv_b, w_r, b_r, w_i, b_i, lam)


def _block_diag_dense(w):
    two, nb, bs, _ = w.shape
    eye = jnp.eye(nb, dtype=w.dtype)
    return jnp.einsum("dgio,gh->dgiho", w, eye).reshape(two, nb * bs, nb * bs)


def _pack_ffn_w_in(w):
    g, up = w[:, :D_FF], w[:, D_FF:]
    padc = ((0, 0), (0, D_FF_PAD - D_FF))
    g = jnp.pad(g, padc).reshape(D_MODEL, N_FF_TILES, FF_TILE)
    up = jnp.pad(up, padc).reshape(D_MODEL, N_FF_TILES, FF_TILE)
    return jnp.stack([g, up], axis=2).reshape(D_MODEL, N_FF_TILES * 2 * FF_TILE).astype(BF16)


def kernel(x, c, ctx, c_ctx, w_ada, b_ada, norm_pre, norm_post, ffn_w_in, ffn_w_out, w_in, w_out,
           lb_logits, a_norm, b_norm, c_lambda, c_norm, d_conv_w, d_conv_b, d_w_r, d_b_r, d_w_i,
           d_b_i, d_lambda):
    batch = x.shape[0]
    assert x.shape == (batch, SEQ, D_MODEL) and ctx.shape == (batch, CTX_LEN, D_MODEL)
    assert batch < MOD_ROWS

    gla_consts = _gla_constants()
    ret_consts = _retention_constants()
    rope = _rope_constants()

    c_rows = jnp.zeros((MOD_ROWS, D_MODEL), F32).at[:batch].set(c).at[batch].set(c_ctx)
    mod = _ada_table(c_rows, w_ada, b_ada).reshape(DEPTH * MOD_ROWS, N_MOD, D_MODEL)

    tm_x, tm_c = 512, 512
    tiles_per_batch = SEQ // tm_x
    row_x = lambda i: i // tiles_per_batch
    row_c = lambda i: batch

    hx = x.reshape(batch * SEQ, D_MODEL)
    hc = ctx.reshape(batch * CTX_LEN, D_MODEL)
    tile2 = lambda g: jnp.tile(g.reshape(1, -1), (1, LANES // g.shape[-1]))

    for l in range(DEPTH):
        ctx_out = l < DEPTH - 1
        gpre = [norm_pre[l, k].reshape(1, D_MODEL) for k in range(3)]
        gpost = [norm_post[l, k].reshape(1, D_MODEL) for k in range(3)]
        ffn_in = [_pack_ffn_w_in(ffn_w_in[l, k]) for k in range(2)]
        ffn_out = [jnp.pad(ffn_w_out[l, k], ((0, D_FF_PAD - D_FF), (0, 0))).astype(BF16) for k in range(2)]
        w_in_l = w_in[l].astype(BF16)
        w_out_l = w_out[l].astype(BF16)

        hc = _ffn(hc, mod, gpre[0], gpost[0], ffn_in[0], ffn_out[0], layer=l, k0=0, row_of_tile=row_c, tm=tm_c)
        hx = _ffn(hx, mod, gpre[0], gpost[0], ffn_in[0], ffn_out[0], layer=l, k0=0, row_of_tile=row_x, tm=tm_x)

        z_c = _inproj(hc, mod, gpre[1], w_in_l, layer=l, row_of_tile=row_c, tm=tm_c)
        z_x = _inproj(hx, mod, gpre[1], w_in_l, layer=l, row_of_tile=row_x, tm=tm_x)

        a_c, a_x = _hgrn2(z_c, z_x, lb_logits, tile2(a_norm[l]), gla_consts, layer=l, batch=batch)
        b_c, b_x = _retention(z_c, z_x, tile2(b_norm[l]), rope, ret_consts, batch=batch)

        lam_init = 0.8 - 0.6 * math.exp(-0.3 * l)
        cgain = c_norm[l].reshape(1, LANES)
        qc, kc, vc = _qkv_prep(z_c, rope, use_rope=False, rows=CTX_LEN)
        qx, kx, vx = _qkv_prep(z_x, rope, use_rope=True, rows=ROPE_ROWS)
        c_x = _diff_attn(qx, kc, vc, kx, vx, c_lambda[l], cgain, lam_init=lam_init, batch=batch, tq=256)

        d_c, d_x = _rglru(z_c, z_x, d_conv_w[l], d_conv_b[l].reshape(1, -1),
                          _block_diag_dense(d_w_r[l]).astype(BF16), d_b_r[l],
                          _block_diag_dense(d_w_i[l]).astype(BF16), d_b_i[l], d_lambda[l], batch=batch)

        hx_mid = _outproj((a_x, b_x, c_x, d_x), w_out_l, hx, mod, gpost[1], layer=l, row_of_tile=row_x, tm=tm_x)
        hx = _ffn(hx_mid, mod, gpre[2], gpost[2], ffn_in[1], ffn_out[1], layer=l, k0=6, row_of_tile=row_x, tm=tm_x)

        if ctx_out:
            c_c = _diff_attn(qc, kc, vc, None, None, c_lambda[l], cgain, lam_init=lam_init, batch=batch, tq=CTX_LEN)
            hc_mid = _outproj((a_c, b_c, c_c, d_c), w_out_l, hc, mod, gpost[1], layer=l, row_of_tile=row_c, tm=tm_c)
            hc = _ffn(hc_mid, mod, gpre[2], gpost[2], ffn_in[1], ffn_out[1], layer=l, k0=6, row_of_tile=row_c, tm=tm_c)

    return hx.reshape(batch, SEQ, D_MODEL)
```

```python
import functools
import math

import numpy as np
import jax
import jax.numpy as jnp
from jax import lax
from jax.experimental import pallas as pl
from jax.experimental.pallas import tpu as pltpu

F32 = jnp.float32
BF16 = jnp.bfloat16

D_MODEL = 2048
SEQ = 2048
CTX_LEN = 256
DEPTH = 2
GRID_W = 64
HEAD_DIM = 64
GROUP_WIDTH = 512
IN_WIDTH = 7168
D_FF = 5504
N_MOD = 9
ROPE_BASE = 10000.0
EPS = 1e-6
RG_C = 8.0
FFN_RESIDUAL = 0.5

LANES = 128
FF_TILE = 512
N_FF_TILES = -(-D_FF // FF_TILE)
FF_OVERLAP = N_FF_TILES * FF_TILE - D_FF
MOD_ROWS = 8
GLA_CHUNK = 128
GLA_LEVELS = 7
SCAN_CHUNK = 256
VMEM_LIMIT = 56 * 1024 * 1024

G_AQ, G_AFF, G_AFB, G_AV, G_AG = 0, 1, 2, 3, 4
G_BQ, G_BK, G_BV, G_BG = 5, 6, 7, 8
G_CQ, G_CK, G_CV = 9, 10, 11
G_DX, G_DG = 12, 13
SLABS = GROUP_WIDTH // LANES


def _cparams(sem):
    return pltpu.CompilerParams(dimension_semantics=sem, vmem_limit_bytes=VMEM_LIMIT)


def _sigmoid(t):
    return jax.nn.sigmoid(t)


def _silu(t):
    return t * _sigmoid(t)


def _rms(t, gain):
    return t * lax.rsqrt(jnp.mean(t * t, axis=-1, keepdims=True) + EPS) * gain


def _dot(a, b):
    return jnp.dot(a, b, preferred_element_type=F32)


def _dot_nt(a, b):
    return lax.dot_general(a, b, (((1,), (1,)), ((), ())), preferred_element_type=F32)


def _split3(t):
    hi = t.astype(BF16)
    r1 = t - hi.astype(F32)
    mid = r1.astype(BF16)
    lo = (r1 - mid.astype(F32)).astype(BF16)
    return hi, mid, lo


def _ada_kernel(c_ref, w_ref, b_ref, o_ref):
    s = _silu(c_ref[...])
    s_hi, s_mid, s_lo = _split3(s)
    w = w_ref[...]
    w_hi = w.astype(BF16)
    r1 = w - w_hi.astype(F32)
    w_mid = r1.astype(BF16)
    w_lo = (r1 - w_mid.astype(F32)).astype(BF16)
    acc = _dot(s_hi, w_hi)
    acc += _dot(s_hi, w_mid) + _dot(s_mid, w_hi)
    acc += _dot(s_hi, w_lo) + _dot(s_mid, w_mid) + _dot(s_lo, w_hi)
    o_ref[...] = acc + b_ref[...]


def _ada_table(c_rows, w_ada, b_ada):
    tn = 1024
    n = N_MOD * D_MODEL
    return pl.pallas_call(
        _ada_kernel,
        grid=(DEPTH, n // tn),
        in_specs=[
            pl.BlockSpec((MOD_ROWS, D_MODEL), lambda l, j: (0, 0)),
            pl.BlockSpec((None, D_MODEL, tn), lambda l, j: (l, 0, j)),
            pl.BlockSpec((None, 1, tn), lambda l, j: (l, 0, j)),
        ],
        out_specs=pl.BlockSpec((None, MOD_ROWS, tn), lambda l, j: (l, 0, j)),
        out_shape=jax.ShapeDtypeStruct((DEPTH, MOD_ROWS, n), F32),
        compiler_params=_cparams(("arbitrary", "arbitrary")),
        name="ada_table",
    )(c_rows, w_ada, b_ada.reshape(DEPTH, 1, n))


ROW_CHUNK = 256
MM_ROWS = 512
MM_COLS = 512


def _norm_modulate_to(u_scr, h_ref, gain_ref, mod_ref, k_shift, k_scale):
    tm = h_ref.shape[0]
    gain = gain_ref[...]
    shift = mod_ref[k_shift:k_shift + 1, :]
    scale1 = 1.0 + mod_ref[k_scale:k_scale + 1, :]

    def body(r, carry):
        r0 = pl.multiple_of(r * ROW_CHUNK, ROW_CHUNK)
        h = h_ref[pl.ds(r0, ROW_CHUNK), :]
        u_scr[pl.ds(r0, ROW_CHUNK), :] = (_rms(h, gain) * scale1 + shift).astype(BF16)
        return carry

    lax.fori_loop(0, tm // ROW_CHUNK, body, 0)


def _ffn_kernel(h_ref, mod_ref, gpre_ref, gpost_ref, wg_ref, wu_ref, wout_ref, o_ref, u_scr, *, k0):
    j = pl.program_id(1)
    last = pl.num_programs(1) - 1
    tm = h_ref.shape[0]

    @pl.when(j == 0)
    def _():
        _norm_modulate_to(u_scr, h_ref, gpre_ref, mod_ref, k0, k0 + 1)

    col = lax.broadcasted_iota(jnp.int32, (1, FF_TILE), 1)
    fresh = jnp.logical_or(j < last, col >= FF_OVERLAP)
    for r in range(tm // MM_ROWS):
        rows = slice(r * MM_ROWS, (r + 1) * MM_ROWS)
        u = u_scr[rows, :]
        g = _dot(u, wg_ref[...])
        up = _dot(u, wu_ref[...])
        act = jnp.where(fresh, _silu(g) * up, 0.0).astype(BF16)
        for n0 in range(0, D_MODEL, MM_COLS):
            part = _dot(act, wout_ref[:, n0:n0 + MM_COLS])

            @pl.when(j == 0)
            def _():
                o_ref[rows, n0:n0 + MM_COLS] = part

            @pl.when(j > 0)
            def _():
                o_ref[rows, n0:n0 + MM_COLS] += part

    @pl.when(j == last)
    def _():
        gpost = gpost_ref[...]
        gate = FFN_RESIDUAL * mod_ref[k0 + 2:k0 + 3, :]

        def body(r, carry):
            r0 = pl.multiple_of(r * ROW_CHUNK, ROW_CHUNK)
            y = o_ref[pl.ds(r0, ROW_CHUNK), :]
            o_ref[pl.ds(r0, ROW_CHUNK), :] = h_ref[pl.ds(r0, ROW_CHUNK), :] + gate * _rms(y, gpost)
            return carry

        lax.fori_loop(0, tm // ROW_CHUNK, body, 0)


def _mod_spec(layer, row_of_tile):
    return pl.BlockSpec((None, N_MOD, D_MODEL), lambda i, *_: (layer * MOD_ROWS + row_of_tile(i), 0, 0))


def _ff_offset(j, base=0):
    return (base // LANES + jnp.minimum(j * (FF_TILE // LANES), (D_FF - FF_TILE) // LANES)) * LANES


def _ffn(h, mod, gpre, gpost, w_in, w_out, *, layer, idx, k0, row_of_tile, tm):
    n = h.shape[0]
    el = pl.Element
    return pl.pallas_call(
        functools.partial(_ffn_kernel, k0=k0),
        grid=(n // tm, N_FF_TILES),
        in_specs=[
            pl.BlockSpec((tm, D_MODEL), lambda i, j: (i, 0), pipeline_mode=pl.Buffered(1)),
            _mod_spec(layer, row_of_tile),
            pl.BlockSpec((1, D_MODEL), lambda i, j: (0, 0)),
            pl.BlockSpec((1, D_MODEL), lambda i, j: (0, 0)),
            pl.BlockSpec((None, None, el(D_MODEL), el(FF_TILE)), lambda i, j: (layer, idx, 0, _ff_offset(j))),
            pl.BlockSpec((None, None, el(D_MODEL), el(FF_TILE)),
                         lambda i, j: (layer, idx, 0, _ff_offset(j, D_FF))),
            pl.BlockSpec((None, None, el(FF_TILE), el(D_MODEL)), lambda i, j: (layer, idx, _ff_offset(j), 0)),
        ],
        out_specs=pl.BlockSpec((tm, D_MODEL), lambda i, j: (i, 0)),
        out_shape=jax.ShapeDtypeStruct((n, D_MODEL), F32),
        scratch_shapes=[pltpu.VMEM((tm, D_MODEL), BF16)],
        compiler_params=_cparams(("parallel", "arbitrary")),
        name="ffn",
    )(h, mod, gpre, gpost, w_in, w_in, w_out)


def _inproj_kernel(h_ref, mod_ref, gpre_ref, w_ref, z_ref, u_scr):
    j = pl.program_id(1)
    tm = h_ref.shape[0]

    @pl.when(j == 0)
    def _():
        _norm_modulate_to(u_scr, h_ref, gpre_ref, mod_ref, 3, 4)

    for r in range(tm // MM_ROWS):
        rows = slice(r * MM_ROWS, (r + 1) * MM_ROWS)
        z_ref[rows, :] = _dot(u_scr[rows, :], w_ref[...])


def _inproj(h, mod, gpre, w, *, layer, row_of_tile, tm):
    n = h.shape[0]
    tn = 1024
    return pl.pallas_call(
        _inproj_kernel,
        grid=(n // tm, IN_WIDTH // tn),
        in_specs=[
            pl.BlockSpec((tm, D_MODEL), lambda i, j: (i, 0), pipeline_mode=pl.Buffered(1)),
            _mod_spec(layer, row_of_tile),
            pl.BlockSpec((1, D_MODEL), lambda i, j: (0, 0)),
            pl.BlockSpec((None, D_MODEL, tn), lambda i, j: (layer, 0, j)),
        ],
        out_specs=pl.BlockSpec((tm, tn), lambda i, j: (i, j)),
        out_shape=jax.ShapeDtypeStruct((n, IN_WIDTH), F32),
        scratch_shapes=[pltpu.VMEM((tm, D_MODEL), BF16)],
        compiler_params=_cparams(("parallel", "arbitrary")),
        name="inproj",
    )(h, mod, gpre, w)


def _outproj_kernel(a_ref, b_ref, c_ref, d_ref, w_ref, h_ref, mod_ref, gpost_ref, o_ref):
    tm = h_ref.shape[0]
    gpost = gpost_ref[...]
    gate = mod_ref[5:6, :]
    for r in range(tm // ROW_CHUNK):
        rows = slice(r * ROW_CHUNK, (r + 1) * ROW_CHUNK)
        y = _dot(a_ref[rows, :], w_ref[0 * GROUP_WIDTH:1 * GROUP_WIDTH, :])
        y += _dot(b_ref[rows, :], w_ref[1 * GROUP_WIDTH:2 * GROUP_WIDTH, :])
        y += _dot(c_ref[rows, :], w_ref[2 * GROUP_WIDTH:3 * GROUP_WIDTH, :])
        y += _dot(d_ref[rows, :], w_ref[3 * GROUP_WIDTH:4 * GROUP_WIDTH, :])
        o_ref[rows, :] = h_ref[rows, :] + gate * _rms(y, gpost)


def _outproj(mix, w, h, mod, gpost, *, layer, row_of_tile, tm):
    n = h.shape[0]
    mix_spec = pl.BlockSpec((tm, GROUP_WIDTH), lambda i: (i, 0))
    return pl.pallas_call(
        _outproj_kernel,
        grid=(n // tm,),
        in_specs=[
            mix_spec, mix_spec, mix_spec, mix_spec,
            pl.BlockSpec((D_MODEL, D_MODEL), lambda i: (0, 0)),
            pl.BlockSpec((tm, D_MODEL), lambda i: (i, 0)),
            _mod_spec(layer, row_of_tile),
            pl.BlockSpec((1, D_MODEL), lambda i: (0, 0)),
        ],
        out_specs=pl.BlockSpec((tm, D_MODEL), lambda i: (i, 0)),
        out_shape=jax.ShapeDtypeStruct((n, D_MODEL), F32),
        compiler_params=_cparams(("parallel",)),
        name="outproj",
    )(*mix, w, h, mod, gpost)


def _gla_constants():
    c = GLA_CHUNK
    idx = np.arange(c)
    i, r = idx[:, None], idx[None, :]
    mats = np.zeros((2, GLA_LEVELS + 2, c, c), np.float32)
    masks = np.zeros((2, GLA_LEVELS + 1, c, c), np.float32)
    mats[0, 0] = r <= i
    mats[0, 1] = r > i
    mats[1, 0] = r >= i
    mats[1, 1] = r < i
    for m in range(GLA_LEVELS):
        hs = c >> (m + 1)
        start = (idx // (2 * hs)) * (2 * hs)
        mid = (start + hs - 1)[:, None]
        right = ((idx // hs) % 2 == 1)[:, None]
        mats[0, 2 + m] = np.where(right, (r > mid) & (r <= i), (r > i) & (r <= mid))
        mats[1, 2 + m] = np.where(right, (r > mid) & (r < i), (r >= i) & (r <= mid))
        same = (start[:, None] == start[None, :])
        q_right = right & ~right.T & same
        masks[0, m] = q_right
        masks[1, m] = q_right.T
    masks[0, GLA_LEVELS] = np.eye(c)
    masks[1, GLA_LEVELS] = np.eye(c)
    mats = mats.reshape(2, (GLA_LEVELS + 2) * c, c)
    masks = np.tile(masks, (1, 1, 2, 1))
    return jnp.asarray(mats, BF16), jnp.asarray(masks, F32)


def _retention_constants():
    c = GLA_CHUNK
    heads = GROUP_WIDTH // HEAD_DIM
    log_decay = np.log1p(-np.exp2(-5.0 - np.arange(heads, dtype=np.float64)))
    ld_lane = np.repeat(log_decay, HEAD_DIM).reshape(SLABS, 1, LANES)
    idx = np.arange(c, dtype=np.float64)
    i, j = idx[:, None], idx[None, :]
    dq = np.zeros((SLABS, 2, c, LANES))
    dk = np.zeros((SLABS, 2, c, LANES))
    dq[:, 0] = np.exp((idx[None, :, None] + 1.0) * ld_lane)
    dk[:, 0] = np.exp((c - 1.0 - idx[None, :, None]) * ld_lane)
    dq[:, 1] = np.exp((c - idx[None, :, None]) * ld_lane)
    dk[:, 1] = np.exp(idx[None, :, None] * ld_lane)
    gt = np.exp(c * ld_lane)
    dm = np.zeros((SLABS, 2, 2, c, c))
    for p in range(SLABS):
        for hh in range(2):
            ld = log_decay[2 * p + hh]
            dm[p, 0, hh] = np.where(i >= j, np.exp((i - j) * ld), 0.0)
            dm[p, 1, hh] = np.where(j >= i, np.exp((j - i) * ld), 0.0)
    dm = dm.reshape(SLABS, 2, 2 * c, c)
    f = lambda t: jnp.asarray(t.astype(np.float32))
    return f(dq), f(dk), f(gt), f(dm)


def _rope_constants():
    quarter = HEAD_DIM // 4
    inv_freq = ROPE_BASE ** (-jnp.arange(quarter, dtype=F32) / quarter)
    rows = SEQ // GRID_W
    row = jnp.repeat(jnp.arange(rows, dtype=F32), GRID_W)
    col = jnp.tile(jnp.arange(GRID_W, dtype=F32), rows)
    ang = jnp.concatenate([row[:, None] * inv_freq, col[:, None] * inv_freq], axis=-1)
    cos, sin = jnp.cos(ang), jnp.sin(ang)
    reps = LANES // HEAD_DIM
    return (jnp.tile(jnp.concatenate([cos, cos], axis=-1), (1, reps)),
            jnp.tile(jnp.concatenate([-sin, sin], axis=-1), (1, reps)))


def _rope(t, cos, sin):
    lane = lax.broadcasted_iota(jnp.int32, t.shape, 1)
    first_half = (lane % HEAD_DIM) < (HEAD_DIM // 2)
    partner = jnp.where(first_half,
                        pltpu.roll(t, LANES - HEAD_DIM // 2, axis=1),
                        pltpu.roll(t, HEAD_DIM // 2, axis=1))
    return t * cos + partner * sin


def _head_lane_masks():
    lane = lax.broadcasted_iota(jnp.int32, (1, LANES), 1)
    return lane < HEAD_DIM, lane >= HEAD_DIM


def _stack_heads(t):
    hm0, hm1 = _head_lane_masks()
    return jnp.concatenate([jnp.where(hm0, t, 0.0).astype(BF16),
                            jnp.where(hm1, t, 0.0).astype(BF16)], axis=0)


def _gla_chunk_update(qe, ke, v, gtot, s_stack, st):
    c = GLA_CHUNK
    o = _dot_nt(qe.astype(BF16), st.astype(BF16))
    s_cat = jnp.concatenate([s_stack[:c].astype(BF16), s_stack[c:].astype(BF16)], axis=1)
    o += _dot(s_cat, _stack_heads(v))
    upd = _dot(v.T.astype(BF16), ke.astype(BF16))
    row_head = lax.broadcasted_iota(jnp.int32, (LANES, LANES), 0) // HEAD_DIM
    col_head = lax.broadcasted_iota(jnp.int32, (LANES, LANES), 1) // HEAD_DIM
    st = st * gtot + jnp.where(row_head == col_head, upd, 0.0)
    return o, st


def _gla_scan(chunk_fn, acc_c, acc_x):
    acc_c[...] = jnp.zeros_like(acc_c)
    acc_x[...] = jnp.zeros_like(acc_x)

    def run(seg, acc_ref, states):
        n = acc_ref.shape[0] // GLA_CHUNK

        def body(t, states):
            out = []
            for d in (0, 1):
                r0 = pl.multiple_of((t if d == 0 else n - 1 - t) * GLA_CHUNK, GLA_CHUNK)
                o, st = chunk_fn(d, seg, r0, states[d])
                acc_ref[pl.ds(r0, GLA_CHUNK), :] += o
                out.append(st)
            return tuple(out)

        return lax.fori_loop(0, n, body, states)

    zero = jnp.zeros((LANES, LANES), F32)
    run(1, acc_x, run(0, acc_c, (zero, zero)))


def _gla_finalize(acc_ref, zg_ref, gain, o_ref):
    hm0, _ = _head_lane_masks()

    def body(c, carry):
        r0 = pl.multiple_of(c * GLA_CHUNK, GLA_CHUNK)
        o = acc_ref[pl.ds(r0, GLA_CHUNK), :]
        sq = o * o
        ms0 = jnp.sum(jnp.where(hm0, sq, 0.0), axis=1, keepdims=True) * (1.0 / HEAD_DIM)
        ms1 = jnp.sum(jnp.where(hm0, 0.0, sq), axis=1, keepdims=True) * (1.0 / HEAD_DIM)
        inv = jnp.where(hm0, lax.rsqrt(ms0 + EPS), lax.rsqrt(ms1 + EPS))
        g = zg_ref[pl.ds(r0, GLA_CHUNK), :]
        o_ref[pl.ds(r0, GLA_CHUNK), :] = (o * inv * gain * _silu(g)).astype(o_ref.dtype)
        return carry

    lax.fori_loop(0, acc_ref.shape[0] // GLA_CHUNK, body, 0)


def _hgrn2_kernel(zq_c, zff_c, zfb_c, zv_c, zg_c, zq_x, zff_x, zfb_x, zv_x, zg_x,
                  lbl_ref, gain_ref, mats_ref, masks_ref, oc_ref, ox_ref, acc_c, acc_x, *, layer):
    c = GLA_CHUNK
    lbl = lbl_ref[...]
    e = jnp.exp(lbl - jnp.max(lbl, axis=0, keepdims=True))
    sm = e / jnp.sum(e, axis=0, keepdims=True)
    lower = jnp.zeros_like(sm[0])
    for l in range(1, layer + 1):
        lower = lower + sm[l]
    zq = (zq_c, zq_x)
    zf = ((zff_c, zff_x), (zfb_c, zfb_x))
    zv = (zv_c, zv_x)
    row = lax.broadcasted_iota(jnp.int32, (c, 1), 0)

    def chunk_fn(d, seg, r0, st):
        lb = lower[d:d + 1, :]
        f = lb + (1.0 - lb) * _sigmoid(zf[d][seg][pl.ds(r0, c), :])
        lf = jnp.log(f)
        k = 1.0 - f
        q = _silu(zq[seg][pl.ds(r0, c), :])
        v = zv[seg][pl.ds(r0, c), :]
        lf_hi, lf_mid, lf_lo = _split3(lf)
        m = mats_ref[d]
        ex = jnp.exp(_dot(m, lf_hi) + _dot(m, lf_mid) + _dot(m, lf_lo))
        qe = q * ex[0:c]
        ke = k * ex[c:2 * c]
        gtot = jnp.exp(jnp.sum(lf, axis=0, keepdims=True))
        s_stack = jnp.zeros((2 * c, c), F32)
        for lvl in range(GLA_LEVELS + 1):
            if lvl < GLA_LEVELS:
                hs = c >> (lvl + 1)
                q_side = ((row // hs) % 2) == (1 if d == 0 else 0)
                both = jnp.where(q_side, q, k) * ex[(2 + lvl) * c:(3 + lvl) * c]
                lhs_src, rhs = both, both.astype(BF16)
            else:
                lhs_src, rhs = q, k.astype(BF16)
            s_stack = s_stack + masks_ref[d, lvl] * _dot_nt(_stack_heads(lhs_src), rhs)
        return _gla_chunk_update(qe, ke, v, gtot, s_stack, st)

    _gla_scan(chunk_fn, acc_c, acc_x)
    gain = gain_ref[...]
    _gla_finalize(acc_c, zg_c, gain, oc_ref)
    _gla_finalize(acc_x, zg_x, gain, ox_ref)


def _z_specs(groups, slab_of):
    c_specs = [pl.BlockSpec((CTX_LEN, LANES), lambda b, p, g=g: (b, g * SLABS + slab_of(b, p)))
               for g in groups]
    x_specs = [pl.BlockSpec((SEQ, LANES), lambda b, p, g=g: (b, g * SLABS + slab_of(b, p)))
               for g in groups]
    return c_specs, x_specs


def _mix_out(batch):
    specs = [pl.BlockSpec((CTX_LEN, LANES), lambda b, p: (b, p)),
             pl.BlockSpec((SEQ, LANES), lambda b, p: (b, p))]
    shapes = [jax.ShapeDtypeStruct((batch * CTX_LEN, GROUP_WIDTH), BF16),
              jax.ShapeDtypeStruct((batch * SEQ, GROUP_WIDTH), BF16)]
    return specs, shapes


def _hgrn2(z_c, z_x, lb_logits, gain, consts, *, layer, batch):
    mats, masks = consts
    groups = (G_AQ, G_AFF, G_AFB, G_AV, G_AG)
    c_specs, x_specs = _z_specs(groups, lambda b, p: p)
    out_specs, out_shapes = _mix_out(batch)
    n_c = len(groups)
    return pl.pallas_call(
        functools.partial(_hgrn2_kernel, layer=layer),
        grid=(batch, SLABS),
        in_specs=c_specs + x_specs + [
            pl.BlockSpec((DEPTH, 2, LANES), lambda b, p: (0, 0, p)),
            pl.BlockSpec((1, LANES), lambda b, p: (0, 0)),
            pl.BlockSpec(mats.shape, lambda b, p: (0, 0, 0)),
            pl.BlockSpec(masks.shape, lambda b, p: (0, 0, 0, 0)),
        ],
        out_specs=out_specs,
        out_shape=out_shapes,
        scratch_shapes=[pltpu.VMEM((CTX_LEN, LANES), F32), pltpu.VMEM((SEQ, LANES), F32)],
        compiler_params=_cparams(("parallel", "parallel")),
        name="hgrn2",
    )(*([z_c] * n_c), *([z_x] * n_c), lb_logits, gain, mats, masks)


def _retention_kernel(zq_c, zk_c, zv_c, zg_c, zq_x, zk_x, zv_x, zg_x, cos_ref, sin_ref,
                      dq_ref, dk_ref, gt_ref, dm_ref, gain_ref, oc_ref, ox_ref, acc_c, acc_x):
    c = GLA_CHUNK
    zq, zk, zv = (zq_c, zq_x), (zk_c, zk_x), (zv_c, zv_x)
    gtot = gt_ref[...]

    def chunk_fn(d, seg, r0, st):
        q = zq[seg][pl.ds(r0, c), :]
        k = zk[seg][pl.ds(r0, c), :] * (HEAD_DIM ** -0.5)
        v = zv[seg][pl.ds(r0, c), :]
        if seg == 1:
            cos = cos_ref[pl.ds(r0, c), :]
            sin = sin_ref[pl.ds(r0, c), :]
            q, k = _rope(q, cos, sin), _rope(k, cos, sin)
        s_stack = dm_ref[d] * _dot_nt(_stack_heads(q), k.astype(BF16))
        return _gla_chunk_update(q * dq_ref[d], k * dk_ref[d], v, gtot, s_stack, st)

    _gla_scan(chunk_fn, acc_c, acc_x)
    gain = gain_ref[...]
    _gla_finalize(acc_c, zg_c, gain, oc_ref)
    _gla_finalize(acc_x, zg_x, gain, ox_ref)


def _retention(z_c, z_x, gain, rope, consts, *, batch):
    dq, dk, gt, dm = consts
    cos, sin = rope
    groups = (G_BQ, G_BK, G_BV, G_BG)
    c_specs, x_specs = _z_specs(groups, lambda b, p: p)
    out_specs, out_shapes = _mix_out(batch)
    n_c = len(groups)
    c = GLA_CHUNK
    return pl.pallas_call(
        _retention_kernel,
        grid=(batch, SLABS),
        in_specs=c_specs + x_specs + [
            pl.BlockSpec((SEQ, LANES), lambda b, p: (0, 0)),
            pl.BlockSpec((SEQ, LANES), lambda b, p: (0, 0)),
            pl.BlockSpec((None, 2, c, LANES), lambda b, p: (p, 0, 0, 0)),
            pl.BlockSpec((None, 2, c, LANES), lambda b, p: (p, 0, 0, 0)),
            pl.BlockSpec((None, 1, LANES), lambda b, p: (p, 0, 0)),
            pl.BlockSpec((None, 2, 2 * c, c), lambda b, p: (p, 0, 0, 0)),
            pl.BlockSpec((1, LANES), lambda b, p: (0, 0)),
        ],
        out_specs=out_specs,
        out_shape=out_shapes,
        scratch_shapes=[pltpu.VMEM((CTX_LEN, LANES), F32), pltpu.VMEM((SEQ, LANES), F32)],
        compiler_params=_cparams(("parallel", "parallel")),
        name="retention",
    )(*([z_c] * n_c), *([z_x] * n_c), cos, sin, dq, dk, gt, dm, gain)


ROPE_ROWS = 512


def _qkv_prep_kernel(zq_ref, zk_ref, zv_ref, cos_ref, sin_ref, q_ref, k_ref, v_ref, *, use_rope):
    for s in range(SLABS):
        cols = slice(s * LANES, (s + 1) * LANES)
        q, k = zq_ref[:, cols], zk_ref[:, cols]
        if use_rope:
            cos, sin = cos_ref[...], sin_ref[...]
            q, k = _rope(q, cos, sin), _rope(k, cos, sin)
        q_ref[:, cols] = (q * (HEAD_DIM ** -0.5)).astype(BF16)
        k_ref[:, cols] = k.astype(BF16)
    v_ref[...] = zv_ref[...].astype(BF16)


def _qkv_prep(z, rope, *, use_rope, rows):
    n = z.shape[0]
    cos, sin = rope
    pos_tiles = SEQ // rows
    zspec = lambda g: pl.BlockSpec((rows, GROUP_WIDTH), lambda i, g=g: (i, g))
    rspec = pl.BlockSpec((rows, LANES), lambda i: (i % pos_tiles, 0))
    ospec = pl.BlockSpec((rows, GROUP_WIDTH), lambda i: (i, 0))
    oshape = jax.ShapeDtypeStruct((n, GROUP_WIDTH), BF16)
    return pl.pallas_call(
        functools.partial(_qkv_prep_kernel, use_rope=use_rope),
        grid=(n // rows,),
        in_specs=[zspec(G_CQ), zspec(G_CK), zspec(G_CV), rspec, rspec],
        out_specs=[ospec, ospec, ospec],
        out_shape=[oshape, oshape, oshape],
        compiler_params=_cparams(("parallel",)),
        name="qkv_prep",
    )(z, z, z, cos, sin)


def _diff_attn_kernel(*refs, lam_init, with_latent):
    if with_latent:
        q_ref, kc_ref, vc_ref, kx_ref, vx_ref, lam_ref, gain_ref, o_ref, kk, vv = refs
    else:
        q_ref, kc_ref, vc_ref, lam_ref, gain_ref, o_ref, kk, vv = refs

    @pl.when(pl.program_id(2) == 0)
    def _():
        kk[0:CTX_LEN, :] = kc_ref[...]
        vv[0:CTX_LEN, :] = vc_ref[...]
        if with_latent:
            kk[CTX_LEN:, :] = kx_ref[...]
            vv[CTX_LEN:, :] = vx_ref[...]

    lv = lam_ref[...]
    lam = (jnp.exp(jnp.sum(lv[0:1] * lv[1:2], axis=1, keepdims=True))
           - jnp.exp(jnp.sum(lv[2:3] * lv[3:4], axis=1, keepdims=True)) + lam_init)
    q = q_ref[...]
    lane = lax.broadcasted_iota(jnp.int32, (1, LANES), 1)
    keys = kk[...]

    def softmax(qh):
        s = _dot_nt(qh, keys)
        p = jnp.exp(s - jnp.max(s, axis=-1, keepdims=True))
        return p / jnp.sum(p, axis=-1, keepdims=True)

    zero = jnp.zeros_like(q)
    a = softmax(jnp.where(lane < HEAD_DIM, q, zero)) - lam * softmax(jnp.where(lane < HEAD_DIM, zero, q))
    o = _dot(a.astype(BF16), vv[...])
    o_ref[...] = (_rms(o, gain_ref[...]) * (1.0 - lam_init)).astype(o_ref.dtype)


def _diff_attn(q, k_c, v_c, k_x, v_x, lam_vecs, gain, *, lam_init, batch, tq):
    with_latent = k_x is not None
    q_len = q.shape[0] // batch
    nq = q_len // tq
    heads = GROUP_WIDTH // LANES
    n_keys = CTX_LEN + (SEQ if with_latent else 0)
    cspec = pl.BlockSpec((CTX_LEN, LANES), lambda b, h, i: (b, h))
    xspec = pl.BlockSpec((SEQ, LANES), lambda b, h, i: (b, h))
    qspec = pl.BlockSpec((tq, LANES), lambda b, h, i: (b * nq + i, h))
    in_specs = [qspec, cspec, cspec] + ([xspec, xspec] if with_latent else []) + [
        pl.BlockSpec((4, HEAD_DIM), lambda b, h, i: (0, 0)),
        pl.BlockSpec((1, LANES), lambda b, h, i: (0, 0)),
    ]
    args = [q, k_c, v_c] + ([k_x, v_x] if with_latent else []) + [lam_vecs, gain]
    return pl.pallas_call(
        functools.partial(_diff_attn_kernel, lam_init=lam_init, with_latent=with_latent),
        grid=(batch, heads, nq),
        in_specs=in_specs,
        out_specs=qspec,
        out_shape=jax.ShapeDtypeStruct(q.shape, BF16),
        scratch_shapes=[pltpu.VMEM((n_keys, LANES), BF16), pltpu.VMEM((n_keys, LANES), BF16)],
        compiler_params=_cparams(("parallel", "parallel", "arbitrary")),
        name="diff_attn",
    )(*args)


CONV_PAD = 8


def _gelu_tanh(t):
    return 0.5 * t * (1.0 + jnp.tanh(math.sqrt(2.0 / math.pi) * (t + 0.044715 * (t * t * t))))


def _softplus(t):
    return jnp.maximum(t, 0.0) + jnp.log1p(jnp.exp(-jnp.abs(t)))


def _rglru_kernel(x_c, g_c, x_x, g_x, cw_ref, cb_ref, wr_ref, br_ref, wi_ref, bi_ref, lam_ref,
                  oc_ref, ox_ref, pad, xc, hs):
    n_tok = CTX_LEN + SEQ
    sc = SCAN_CHUNK
    cw = cw_ref[...]
    cb = cb_ref[...]
    zeros = jnp.zeros((CONV_PAD, LANES), F32)

    for x_ref, base, n in ((x_c, 0, CTX_LEN), (x_x, CTX_LEN, SEQ)):
        pad[0:CONV_PAD, :] = zeros
        pad[CONV_PAD:CONV_PAD + n, :] = x_ref[...]
        pad[CONV_PAD + n:2 * CONV_PAD + n, :] = zeros
        for r in range(n // sc):
            t0 = CONV_PAD + r * sc
            acc = cb + cw[0:1] * pad[t0 - 1:t0 - 1 + sc, :]
            acc += cw[1:2] * pad[t0:t0 + sc, :]
            acc += cw[2:3] * pad[t0 + 1:t0 + 1 + sc, :]
            acc += cw[3:4] * pad[t0 + 2:t0 + 2 + sc, :]
            xc[base + r * sc:base + (r + 1) * sc, :] = acc

    row = lax.broadcasted_iota(jnp.int32, (sc, 1), 0)
    n_chunks = n_tok // sc
    neg_c_softplus = -RG_C * _softplus(-lam_ref[...])
    hs[...] = jnp.zeros_like(hs)

    def scan_chunk(d, r0, carry):
        x = xc[pl.ds(r0, sc), :]
        xb = x.astype(BF16)
        r = _sigmoid(_dot(xb, wr_ref[d]) + br_ref[d:d + 1, :])
        i = _sigmoid(_dot(xb, wi_ref[d]) + bi_ref[d:d + 1, :])
        log_a = neg_c_softplus[d:d + 1, :] * r
        a = jnp.exp(log_a)
        u = jnp.sqrt(-jnp.tanh(log_a) * (a * a + 1.0)) * (i * x)
        s = 1
        while s < sc:
            ok, shift = (row >= s, s) if d == 0 else (row < sc - s, sc - s)
            u_prev = jnp.where(ok, pltpu.roll(u, shift, axis=0), 0.0)
            a_prev = jnp.where(ok, pltpu.roll(a, shift, axis=0), 1.0)
            u = u + a * u_prev
            a = a * a_prev
            s *= 2
        h = u + a * carry
        hs[pl.ds(r0, sc), :] += h
        return h[sc - 1:sc, :] if d == 0 else h[0:1, :]

    def body(t, carries):
        fwd = scan_chunk(0, pl.multiple_of(t * sc, sc), carries[0])
        back = jnp.where(t == 0, 0, n_chunks - t)
        bwd = scan_chunk(1, pl.multiple_of(back * sc, sc), carries[1])
        return fwd, bwd

    zero = jnp.zeros((1, LANES), F32)
    lax.fori_loop(0, n_chunks, body, (zero, zero))

    oc_ref[...] = (_gelu_tanh(g_c[...]) * hs[0:CTX_LEN, :]).astype(oc_ref.dtype)
    for r in range(SEQ // sc):
        rows = slice(r * sc, (r + 1) * sc)
        ox_ref[rows, :] = (_gelu_tanh(g_x[rows, :]) * hs[CTX_LEN + r * sc:CTX_LEN + (r + 1) * sc, :]
                           ).astype(ox_ref.dtype)


def _rglru(z_c, z_x, conv_w, conv_b, w_r, b_r, w_i, b_i, lam, *, batch):
    groups = (G_DX, G_DG)
    c_specs, x_specs = _z_specs(groups, lambda b, p: p)
    out_specs, out_shapes = _mix_out(batch)
    vec2 = pl.BlockSpec((2, LANES), lambda b, p: (0, p))
    wspec = pl.BlockSpec((2, LANES, LANES), lambda b, p: (0, p, p))
    return pl.pallas_call(
        _rglru_kernel,
        grid=(batch, SLABS),
        in_specs=c_specs + x_specs + [
            pl.BlockSpec((4, LANES), lambda b, p: (0, p)),
            pl.BlockSpec((1, LANES), lambda b, p: (0, p)),
            wspec, vec2, wspec, vec2, vec2,
        ],
        out_specs=out_specs,
        out_shape=out_shapes,
        scratch_shapes=[pltpu.VMEM((SEQ + 2 * CONV_PAD, LANES), F32),
                        pltpu.VMEM((CTX_LEN + SEQ, LANES), F32),
                        pltpu.VMEM((CTX_LEN + SEQ, LANES), F32)],
        compiler_params=_cparams(("parallel", "parallel")),
        name="rglru",
    )(z_c, z_c, z_x, z_x, conv_w, conv_b, w_r, b_r, w_i, b_i, lam)


def _block_diag_dense(w):
    two, nb, bs, _ = w.shape
    eye = jnp.eye(nb, dtype=w.dtype)
    return jnp.einsum("dgio,gh->dgiho", w, eye).reshape(two, nb * bs, nb * bs)


def kernel(x, c, ctx, c_ctx, w_ada, b_ada, norm_pre, norm_post, ffn_w_in, ffn_w_out, w_in, w_out,
           lb_logits, a_norm, b_norm, c_lambda, c_norm, d_conv_w, d_conv_b, d_w_r, d_b_r, d_w_i,
           d_b_i, d_lambda):
    batch = x.shape[0]
    assert x.shape == (batch, SEQ, D_MODEL) and ctx.shape == (batch, CTX_LEN, D_MODEL)
    assert batch < MOD_ROWS

    gla_consts = _gla_constants()
    ret_consts = _retention_constants()
    rope = _rope_constants()

    c_rows = jnp.zeros((MOD_ROWS, D_MODEL), F32).at[:batch].set(c).at[batch].set(c_ctx)
    mod = _ada_table(c_rows, w_ada, b_ada).reshape(DEPTH * MOD_ROWS, N_MOD, D_MODEL)

    n_c = batch * CTX_LEN
    tm_ffn, tm_proj, tm_out = 1024, 1024, 512
    tm_c = min(n_c, 1024)
    assert n_c % tm_c == 0
    row_x = lambda tm: (lambda i: i // (SEQ // tm))
    row_c = lambda i: batch

    hx = x.reshape(batch * SEQ, D_MODEL)
    hc = ctx.reshape(n_c, D_MODEL)
    tile2 = lambda g: jnp.tile(g.reshape(1, -1), (1, LANES // g.shape[-1]))
    ffn_w_in_b = ffn_w_in.astype(BF16)
    ffn_w_out_b = ffn_w_out.astype(BF16)
    w_in_b = w_in.astype(BF16)
    w_out_b = w_out.astype(BF16)

    def ffn(h, l, idx, is_ctx):
        return _ffn(h, mod, norm_pre[l, 2 * idx].reshape(1, D_MODEL), norm_post[l, 2 * idx].reshape(1, D_MODEL),
                    ffn_w_in_b, ffn_w_out_b, layer=l, idx=idx, k0=6 * idx,
                    row_of_tile=row_c if is_ctx else row_x(tm_ffn), tm=tm_c if is_ctx else tm_ffn)

    for l in range(DEPTH):
        ctx_out = l < DEPTH - 1
        gpre1 = norm_pre[l, 1].reshape(1, D_MODEL)
        gpost1 = norm_post[l, 1].reshape(1, D_MODEL)

        hc = ffn(hc, l, 0, True)
        hx = ffn(hx, l, 0, False)

        z_c = _inproj(hc, mod, gpre1, w_in_b, layer=l, row_of_tile=row_c, tm=tm_c)
        z_x = _inproj(hx, mod, gpre1, w_in_b, layer=l, row_of_tile=row_x(tm_proj), tm=tm_proj)

        a_c, a_x = _hgrn2(z_c, z_x, lb_logits, tile2(a_norm[l]), gla_consts, layer=l, batch=batch)
        b_c, b_x = _retention(z_c, z_x, tile2(b_norm[l]), rope, ret_consts, batch=batch)

        lam_init = 0.8 - 0.6 * math.exp(-0.3 * l)
        cgain = c_norm[l].reshape(1, LANES)
        qc, kc, vc = _qkv_prep(z_c, rope, use_rope=False, rows=CTX_LEN)
        qx, kx, vx = _qkv_prep(z_x, rope, use_rope=True, rows=ROPE_ROWS)
        c_x = _diff_attn(qx, kc, vc, kx, vx, c_lambda[l], cgain, lam_init=lam_init, batch=batch, tq=256)

        d_c, d_x = _rglru(z_c, z_x, d_conv_w[l], d_conv_b[l].reshape(1, -1),
                          _block_diag_dense(d_w_r[l]).astype(BF16), d_b_r[l],
                          _block_diag_dense(d_w_i[l]).astype(BF16), d_b_i[l], d_lambda[l], batch=batch)

        hx = _outproj((a_x, b_x, c_x, d_x), w_out_b[l], hx, mod, gpost1, layer=l,
                      row_of_tile=row_x(tm_out), tm=tm_out)
        hx = ffn(hx, l, 1, False)

        if ctx_out:
            c_c = _diff_attn(qc, kc, vc, None, None, c_lambda[l], cgain, lam_init=lam_init, batch=batch, tq=CTX_LEN)
            hc = _outproj((a_c, b_c, c_c, d_c), w_out_b[l], hc, mod, gpost1, layer=l,
                          row_of_tile=row_c, tm=min(tm_c, tm_out))
            hc = ffn(hc, l, 1, True)

    return hx.reshape(batch, SEQ, D_MODEL)
```

```python
import functools
import math

import numpy as np
import jax
import jax.numpy as jnp
from jax import lax
from jax.experimental import pallas as pl
from jax.experimental.pallas import tpu as pltpu

F32 = jnp.float32
BF16 = jnp.bfloat16

D_MODEL = 2048
SEQ = 2048
CTX_LEN = 256
DEPTH = 2
GRID_W = 64
HEAD_DIM = 64
GROUP_WIDTH = 512
IN_WIDTH = 7168
D_FF = 5504
N_MOD = 9
ROPE_BASE = 10000.0
EPS = 1e-6
RG_C = 8.0
FFN_RESIDUAL = 0.5

LANES = 128
FF_TILE = 512
N_FF_TILES = -(-D_FF // FF_TILE)
FF_OVERLAP = N_FF_TILES * FF_TILE - D_FF
MOD_ROWS = 8
GLA_CHUNK = 128
GLA_WIDTH = 256
GLA_LEVELS = 7
SCAN_CHUNK = 256
VMEM_LIMIT = 56 * 1024 * 1024

G_AQ, G_AFF, G_AFB, G_AV, G_AG = 0, 1, 2, 3, 4
G_BQ, G_BK, G_BV, G_BG = 5, 6, 7, 8
G_CQ, G_CK, G_CV = 9, 10, 11
G_DX, G_DG = 12, 13
SLABS = GROUP_WIDTH // LANES


def _cparams(sem):
    return pltpu.CompilerParams(dimension_semantics=sem, vmem_limit_bytes=VMEM_LIMIT)


def _sigmoid(t):
    return jax.nn.sigmoid(t)


def _silu(t):
    return t * _sigmoid(t)


def _rms(t, gain):
    return t * lax.rsqrt(jnp.mean(t * t, axis=-1, keepdims=True) + EPS) * gain


def _dot(a, b):
    return jnp.dot(a, b, preferred_element_type=F32)


def _dot_nt(a, b):
    return lax.dot_general(a, b, (((1,), (1,)), ((), ())), preferred_element_type=F32)


def _split3(t):
    hi = t.astype(BF16)
    r1 = t - hi.astype(F32)
    mid = r1.astype(BF16)
    lo = (r1 - mid.astype(F32)).astype(BF16)
    return hi, mid, lo


def _ada_kernel(c_ref, w_ref, b_ref, o_ref):
    s = _silu(c_ref[...])
    s_hi, s_mid, s_lo = _split3(s)
    w = w_ref[...]
    w_hi = w.astype(BF16)
    r1 = w - w_hi.astype(F32)
    w_mid = r1.astype(BF16)
    w_lo = (r1 - w_mid.astype(F32)).astype(BF16)
    acc = _dot(s_hi, w_hi)
    acc += _dot(s_hi, w_mid) + _dot(s_mid, w_hi)
    acc += _dot(s_hi, w_lo) + _dot(s_mid, w_mid) + _dot(s_lo, w_hi)
    o_ref[...] = acc + b_ref[...]


def _ada_table(c_rows, w_ada, b_ada):
    tn = 1024
    n = N_MOD * D_MODEL
    return pl.pallas_call(
        _ada_kernel,
        grid=(DEPTH, n // tn),
        in_specs=[
            pl.BlockSpec((MOD_ROWS, D_MODEL), lambda l, j: (0, 0)),
            pl.BlockSpec((None, D_MODEL, tn), lambda l, j: (l, 0, j)),
            pl.BlockSpec((None, 1, tn), lambda l, j: (l, 0, j)),
        ],
        out_specs=pl.BlockSpec((None, MOD_ROWS, tn), lambda l, j: (l, 0, j)),
        out_shape=jax.ShapeDtypeStruct((DEPTH, MOD_ROWS, n), F32),
        compiler_params=_cparams(("arbitrary", "arbitrary")),
        name="ada_table",
    )(c_rows, w_ada, b_ada.reshape(DEPTH, 1, n))


ROW_CHUNK = 256
MM_ROWS = 512
MM_COLS = 512


def _norm_modulate_to(u_scr, h_ref, gain_ref, mod_ref, k_shift, k_scale):
    tm = h_ref.shape[0]
    gain = gain_ref[...]
    shift = mod_ref[k_shift:k_shift + 1, :]
    scale1 = 1.0 + mod_ref[k_scale:k_scale + 1, :]

    def body(r, carry):
        r0 = pl.multiple_of(r * ROW_CHUNK, ROW_CHUNK)
        h = h_ref[pl.ds(r0, ROW_CHUNK), :]
        u_scr[pl.ds(r0, ROW_CHUNK), :] = (_rms(h, gain) * scale1 + shift).astype(BF16)
        return carry

    lax.fori_loop(0, tm // ROW_CHUNK, body, 0)


def _ffn_kernel(h_ref, mod_ref, gpre_ref, gpost_ref, wg_ref, wu_ref, wout_ref, o_ref, u_scr, *, k0):
    j = pl.program_id(1)
    last = pl.num_programs(1) - 1
    tm = h_ref.shape[0]

    @pl.when(j == 0)
    def _():
        _norm_modulate_to(u_scr, h_ref, gpre_ref, mod_ref, k0, k0 + 1)
        o_ref[...] = jnp.zeros_like(o_ref)

    col = lax.broadcasted_iota(jnp.int32, (1, FF_TILE), 1)
    fresh = jnp.logical_or(j < last, col >= FF_OVERLAP)
    for r in range(tm // MM_ROWS):
        rows = slice(r * MM_ROWS, (r + 1) * MM_ROWS)
        u = u_scr[rows, :]
        g = _dot(u, wg_ref[...])
        up = _dot(u, wu_ref[...])
        act = jnp.where(fresh, _silu(g) * up, 0.0).astype(BF16)
        for n0 in range(0, D_MODEL, MM_COLS):
            o_ref[rows, n0:n0 + MM_COLS] += _dot(act, wout_ref[:, n0:n0 + MM_COLS])

    @pl.when(j == last)
    def _():
        gpost = gpost_ref[...]
        gate = FFN_RESIDUAL * mod_ref[k0 + 2:k0 + 3, :]

        def body(r, carry):
            r0 = pl.multiple_of(r * ROW_CHUNK, ROW_CHUNK)
            y = o_ref[pl.ds(r0, ROW_CHUNK), :]
            o_ref[pl.ds(r0, ROW_CHUNK), :] = h_ref[pl.ds(r0, ROW_CHUNK), :] + gate * _rms(y, gpost)
            return carry

        lax.fori_loop(0, tm // ROW_CHUNK, body, 0)


def _mod_spec(layer, row_of_tile):
    return pl.BlockSpec((None, N_MOD, D_MODEL), lambda i, *_: (layer * MOD_ROWS + row_of_tile(i), 0, 0))


def _ff_offset(j, base=0):
    return (base // LANES + jnp.minimum(j * (FF_TILE // LANES), (D_FF - FF_TILE) // LANES)) * LANES


def _ffn(h, mod, gpre, gpost, w_in, w_out, *, layer, idx, k0, row_of_tile, tm):
    n = h.shape[0]
    el = pl.Element
    return pl.pallas_call(
        functools.partial(_ffn_kernel, k0=k0),
        grid=(n // tm, N_FF_TILES),
        in_specs=[
            pl.BlockSpec((tm, D_MODEL), lambda i, j: (i, 0), pipeline_mode=pl.Buffered(1)),
            _mod_spec(layer, row_of_tile),
            pl.BlockSpec((1, D_MODEL), lambda i, j: (0, 0)),
            pl.BlockSpec((1, D_MODEL), lambda i, j: (0, 0)),
            pl.BlockSpec((None, None, el(D_MODEL), el(FF_TILE)), lambda i, j: (layer, idx, 0, _ff_offset(j))),
            pl.BlockSpec((None, None, el(D_MODEL), el(FF_TILE)),
                         lambda i, j: (layer, idx, 0, _ff_offset(j, D_FF))),
            pl.BlockSpec((None, None, el(FF_TILE), el(D_MODEL)), lambda i, j: (layer, idx, _ff_offset(j), 0)),
        ],
        out_specs=pl.BlockSpec((tm, D_MODEL), lambda i, j: (i, 0)),
        out_shape=jax.ShapeDtypeStruct((n, D_MODEL), F32),
        scratch_shapes=[pltpu.VMEM((tm, D_MODEL), BF16)],
        compiler_params=_cparams(("parallel", "arbitrary")),
        name="ffn",
    )(h, mod, gpre, gpost, w_in, w_in, w_out)


def _inproj_kernel(h_ref, mod_ref, gpre_ref, w_ref, z_ref, u_scr):
    j = pl.program_id(1)
    tm = h_ref.shape[0]

    @pl.when(j == 0)
    def _():
        _norm_modulate_to(u_scr, h_ref, gpre_ref, mod_ref, 3, 4)

    for r in range(tm // MM_ROWS):
        rows = slice(r * MM_ROWS, (r + 1) * MM_ROWS)
        z_ref[rows, :] = _dot(u_scr[rows, :], w_ref[...])


def _inproj(h, mod, gpre, w, *, layer, row_of_tile, tm):
    n = h.shape[0]
    tn = 1024
    return pl.pallas_call(
        _inproj_kernel,
        grid=(n // tm, IN_WIDTH // tn),
        in_specs=[
            pl.BlockSpec((tm, D_MODEL), lambda i, j: (i, 0), pipeline_mode=pl.Buffered(1)),
            _mod_spec(layer, row_of_tile),
            pl.BlockSpec((1, D_MODEL), lambda i, j: (0, 0)),
            pl.BlockSpec((None, D_MODEL, tn), lambda i, j: (layer, 0, j)),
        ],
        out_specs=pl.BlockSpec((tm, tn), lambda i, j: (i, j)),
        out_shape=jax.ShapeDtypeStruct((n, IN_WIDTH), F32),
        scratch_shapes=[pltpu.VMEM((tm, D_MODEL), BF16)],
        compiler_params=_cparams(("parallel", "arbitrary")),
        name="inproj",
    )(h, mod, gpre, w)


def _outproj_kernel(a_ref, b_ref, c_ref, d_ref, w_ref, h_ref, mod_ref, gpost_ref, o_ref):
    tm = h_ref.shape[0]
    gpost = gpost_ref[...]
    gate = mod_ref[5:6, :]
    for r in range(tm // ROW_CHUNK):
        rows = slice(r * ROW_CHUNK, (r + 1) * ROW_CHUNK)
        y = _dot(a_ref[rows, :], w_ref[0 * GROUP_WIDTH:1 * GROUP_WIDTH, :])
        y += _dot(b_ref[rows, :], w_ref[1 * GROUP_WIDTH:2 * GROUP_WIDTH, :])
        y += _dot(c_ref[rows, :], w_ref[2 * GROUP_WIDTH:3 * GROUP_WIDTH, :])
        y += _dot(d_ref[rows, :], w_ref[3 * GROUP_WIDTH:4 * GROUP_WIDTH, :])
        o_ref[rows, :] = h_ref[rows, :] + gate * _rms(y, gpost)


def _outproj(mix, w, h, mod, gpost, *, layer, row_of_tile, tm):
    n = h.shape[0]
    mix_spec = pl.BlockSpec((tm, GROUP_WIDTH), lambda i: (i, 0))
    return pl.pallas_call(
        _outproj_kernel,
        grid=(n // tm,),
        in_specs=[
            mix_spec, mix_spec, mix_spec, mix_spec,
            pl.BlockSpec((D_MODEL, D_MODEL), lambda i: (0, 0)),
            pl.BlockSpec((tm, D_MODEL), lambda i: (i, 0)),
            _mod_spec(layer, row_of_tile),
            pl.BlockSpec((1, D_MODEL), lambda i: (0, 0)),
        ],
        out_specs=pl.BlockSpec((tm, D_MODEL), lambda i: (i, 0)),
        out_shape=jax.ShapeDtypeStruct((n, D_MODEL), F32),
        compiler_params=_cparams(("parallel",)),
        name="outproj",
    )(*mix, w, h, mod, gpost)


def _gla_constants():
    c = GLA_CHUNK
    idx = np.arange(c)
    i, r = idx[:, None], idx[None, :]
    mats = np.zeros((2, GLA_LEVELS + 2, c, c), np.float32)
    masks = np.zeros((2, GLA_LEVELS + 1, c, c), np.float32)
    mats[0, 0] = r <= i
    mats[0, 1] = r > i
    mats[1, 0] = r >= i
    mats[1, 1] = r < i
    for m in range(GLA_LEVELS):
        hs = c >> (m + 1)
        start = (idx // (2 * hs)) * (2 * hs)
        mid = (start + hs - 1)[:, None]
        right = ((idx // hs) % 2 == 1)[:, None]
        mats[0, 2 + m] = np.where(right, (r > mid) & (r <= i), (r > i) & (r <= mid))
        mats[1, 2 + m] = np.where(right, (r > mid) & (r < i), (r >= i) & (r <= mid))
        same = (start[:, None] == start[None, :])
        q_right = right & ~right.T & same
        masks[0, m] = q_right
        masks[1, m] = q_right.T
    masks[0, GLA_LEVELS] = np.eye(c)
    masks[1, GLA_LEVELS] = np.eye(c)
    mats = mats.reshape(2, (GLA_LEVELS + 2) * c, c)
    mats = np.concatenate([mats, mats], axis=-1)
    masks = np.tile(masks, (1, 1, 1, 2))
    return jnp.asarray(mats, BF16), jnp.asarray(masks, F32)


def _retention_constants():
    c = GLA_CHUNK
    heads = GROUP_WIDTH // HEAD_DIM
    log_decay = np.log1p(-np.exp2(-5.0 - np.arange(heads, dtype=np.float64)))
    ld_lane = np.repeat(log_decay, HEAD_DIM).reshape(SLABS, 1, LANES)
    idx = np.arange(c, dtype=np.float64)
    i, j = idx[:, None], idx[None, :]
    dq = np.zeros((SLABS, 2, c, LANES))
    dk = np.zeros((SLABS, 2, c, LANES))
    dq[:, 0] = np.exp((idx[None, :, None] + 1.0) * ld_lane)
    dk[:, 0] = np.exp((c - 1.0 - idx[None, :, None]) * ld_lane)
    dq[:, 1] = np.exp((c - idx[None, :, None]) * ld_lane)
    dk[:, 1] = np.exp(idx[None, :, None] * ld_lane)
    gt = np.exp(c * ld_lane)
    dm = np.zeros((SLABS, 2, 2, c, c))
    for p in range(SLABS):
        for hh in range(2):
            ld = log_decay[2 * p + hh]
            dm[p, 0, hh] = np.where(i >= j, np.exp((i - j) * ld), 0.0)
            dm[p, 1, hh] = np.where(j >= i, np.exp((j - i) * ld), 0.0)
    dm = dm.transpose(0, 1, 3, 2, 4).reshape(SLABS, 2, c, 2 * c)
    f = lambda t: jnp.asarray(t.astype(np.float32))
    return f(dq), f(dk), f(gt), f(dm)


def _rope_constants():
    quarter = HEAD_DIM // 4
    inv_freq = ROPE_BASE ** (-jnp.arange(quarter, dtype=F32) / quarter)
    rows = SEQ // GRID_W
    row = jnp.repeat(jnp.arange(rows, dtype=F32), GRID_W)
    col = jnp.tile(jnp.arange(GRID_W, dtype=F32), rows)
    ang = jnp.concatenate([row[:, None] * inv_freq, col[:, None] * inv_freq], axis=-1)
    cos, sin = jnp.cos(ang), jnp.sin(ang)
    reps = LANES // HEAD_DIM
    return (jnp.tile(jnp.concatenate([cos, cos], axis=-1), (1, reps)),
            jnp.tile(jnp.concatenate([-sin, sin], axis=-1), (1, reps)))


def _rope(t, cos, sin):
    lane = lax.broadcasted_iota(jnp.int32, t.shape, 1)
    first_half = (lane % HEAD_DIM) < (HEAD_DIM // 2)
    partner = jnp.where(first_half,
                        pltpu.roll(t, LANES - HEAD_DIM // 2, axis=1),
                        pltpu.roll(t, HEAD_DIM // 2, axis=1))
    return t * cos + partner * sin


def _head_lane_masks():
    lane = lax.broadcasted_iota(jnp.int32, (1, LANES), 1)
    return lane < HEAD_DIM, lane >= HEAD_DIM


def _stack_heads(t):
    hm0, hm1 = _head_lane_masks()
    return jnp.concatenate([jnp.where(hm0, t, 0.0).astype(BF16),
                            jnp.where(hm1, t, 0.0).astype(BF16)], axis=0)


def _gla_chunk_update(qe, ke, v, gtot, s_cat, st):
    o = _dot_nt(qe.astype(BF16), st.astype(BF16))
    o += _dot(s_cat.astype(BF16), _stack_heads(v))
    upd = _dot(v.T.astype(BF16), ke.astype(BF16))
    row_head = lax.broadcasted_iota(jnp.int32, (LANES, LANES), 0) // HEAD_DIM
    col_head = lax.broadcasted_iota(jnp.int32, (LANES, LANES), 1) // HEAD_DIM
    st = st * gtot + jnp.where(row_head == col_head, upd, 0.0)
    return o, st


def _gla_scan(chunk_fn, acc_c, acc_x):
    n_slabs = acc_c.shape[1] // LANES
    acc_c[...] = jnp.zeros_like(acc_c)
    acc_x[...] = jnp.zeros_like(acc_x)

    def run(seg, acc_ref, states):
        n = acc_ref.shape[0] // GLA_CHUNK

        def body(t, states):
            out = []
            for d in (0, 1):
                r0 = pl.multiple_of((t if d == 0 else n - 1 - t) * GLA_CHUNK, GLA_CHUNK)
                o, st = chunk_fn(d, seg, r0, states[d])
                acc_ref[pl.ds(r0, GLA_CHUNK), :] += o
                out.append(st)
            return tuple(out)

        return lax.fori_loop(0, n, body, states)

    zero = tuple(jnp.zeros((LANES, LANES), F32) for _ in range(n_slabs))
    run(1, acc_x, run(0, acc_c, (zero, zero)))


def _slab(t, s):
    return t[:, s * LANES:(s + 1) * LANES]


def _gla_finalize(acc_ref, zg_ref, gain, o_ref):
    hm0, _ = _head_lane_masks()
    n_slabs = acc_ref.shape[1] // LANES

    def body(c, carry):
        r0 = pl.multiple_of(c * GLA_CHUNK, GLA_CHUNK)
        for s in range(n_slabs):
            cols = slice(s * LANES, (s + 1) * LANES)
            o = acc_ref[pl.ds(r0, GLA_CHUNK), cols]
            sq = o * o
            ms0 = jnp.sum(jnp.where(hm0, sq, 0.0), axis=1, keepdims=True) * (1.0 / HEAD_DIM)
            ms1 = jnp.sum(jnp.where(hm0, 0.0, sq), axis=1, keepdims=True) * (1.0 / HEAD_DIM)
            inv = jnp.where(hm0, lax.rsqrt(ms0 + EPS), lax.rsqrt(ms1 + EPS))
            g = zg_ref[pl.ds(r0, GLA_CHUNK), cols]
            o_ref[pl.ds(r0, GLA_CHUNK), cols] = (o * inv * gain * _silu(g)).astype(o_ref.dtype)
        return carry

    lax.fori_loop(0, acc_ref.shape[0] // GLA_CHUNK, body, 0)


def _hgrn2_kernel(zq_c, zff_c, zfb_c, zv_c, zg_c, zq_x, zff_x, zfb_x, zv_x, zg_x,
                  lbl_ref, gain_ref, mats_ref, masks_ref, oc_ref, ox_ref, acc_c, acc_x, *, layer):
    c = GLA_CHUNK
    n_slabs = acc_c.shape[1] // LANES
    lbl = lbl_ref[...]
    e = jnp.exp(lbl - jnp.max(lbl, axis=0, keepdims=True))
    sm = e / jnp.sum(e, axis=0, keepdims=True)
    lower = jnp.zeros_like(sm[0])
    for l in range(1, layer + 1):
        lower = lower + sm[l]
    zq = (zq_c, zq_x)
    zf = ((zff_c, zff_x), (zfb_c, zfb_x))
    zv = (zv_c, zv_x)
    row = lax.broadcasted_iota(jnp.int32, (c, 1), 0)

    def chunk_fn(d, seg, r0, states):
        lb = lower[d:d + 1, :]
        f = lb + (1.0 - lb) * _sigmoid(zf[d][seg][pl.ds(r0, c), :])
        lf = jnp.log(f)
        k = 1.0 - f
        q = _silu(zq[seg][pl.ds(r0, c), :])
        v = zv[seg][pl.ds(r0, c), :]
        lf_hi = lf.astype(BF16)
        lf_lo = (lf - lf_hi.astype(F32)).astype(BF16)
        ex = jnp.exp(_dot(mats_ref[d], jnp.concatenate([lf_hi, lf_lo], axis=0)))
        qe = q * ex[0:c]
        ke = k * ex[c:2 * c]
        gtot = jnp.exp(jnp.sum(lf, axis=0, keepdims=True))
        s_cats = [jnp.zeros((c, 2 * c), F32) for _ in range(n_slabs)]
        for lvl in range(GLA_LEVELS + 1):
            if lvl < GLA_LEVELS:
                hs = c >> (lvl + 1)
                q_side = ((row // hs) % 2) == (1 if d == 0 else 0)
                rhs_src = jnp.where(q_side, q, k) * ex[(2 + lvl) * c:(3 + lvl) * c]
                lhs = rhs_src.astype(BF16)
            else:
                lhs, rhs_src = q.astype(BF16), k
            mask = masks_ref[d, lvl]
            for s in range(n_slabs):
                s_cats[s] = s_cats[s] + mask * _dot_nt(_slab(lhs, s), _stack_heads(_slab(rhs_src, s)))
        outs = [_gla_chunk_update(_slab(qe, s), _slab(ke, s), _slab(v, s), _slab(gtot, s), s_cats[s],
                                  states[s]) for s in range(n_slabs)]
        return jnp.concatenate([o for o, _ in outs], axis=1), tuple(st for _, st in outs)

    _gla_scan(chunk_fn, acc_c, acc_x)
    gain = gain_ref[...]
    _gla_finalize(acc_c, zg_c, gain, oc_ref)
    _gla_finalize(acc_x, zg_x, gain, ox_ref)


def _z_specs(groups, width):
    per_group = GROUP_WIDTH // width
    c_specs = [pl.BlockSpec((CTX_LEN, width), lambda b, p, g=g: (b, g * per_group + p)) for g in groups]
    x_specs = [pl.BlockSpec((SEQ, width), lambda b, p, g=g: (b, g * per_group + p)) for g in groups]
    return c_specs, x_specs


def _mix_out(batch, width):
    specs = [pl.BlockSpec((CTX_LEN, width), lambda b, p: (b, p)),
             pl.BlockSpec((SEQ, width), lambda b, p: (b, p))]
    shapes = [jax.ShapeDtypeStruct((batch * CTX_LEN, GROUP_WIDTH), BF16),
              jax.ShapeDtypeStruct((batch * SEQ, GROUP_WIDTH), BF16)]
    return specs, shapes


def _gla_scratch():
    return [pltpu.VMEM((CTX_LEN, GLA_WIDTH), F32), pltpu.VMEM((SEQ, GLA_WIDTH), F32)]


def _hgrn2(z_c, z_x, lb_logits, gain, consts, *, layer, batch):
    mats, masks = consts
    groups = (G_AQ, G_AFF, G_AFB, G_AV, G_AG)
    c_specs, x_specs = _z_specs(groups, GLA_WIDTH)
    out_specs, out_shapes = _mix_out(batch, GLA_WIDTH)
    n_c = len(groups)
    return pl.pallas_call(
        functools.partial(_hgrn2_kernel, layer=layer),
        grid=(batch, GROUP_WIDTH // GLA_WIDTH),
        in_specs=c_specs + x_specs + [
            pl.BlockSpec((DEPTH, 2, GLA_WIDTH), lambda b, p: (0, 0, p)),
            pl.BlockSpec((1, LANES), lambda b, p: (0, 0)),
            pl.BlockSpec(mats.shape, lambda b, p: (0, 0, 0)),
            pl.BlockSpec(masks.shape, lambda b, p: (0, 0, 0, 0)),
        ],
        out_specs=out_specs,
        out_shape=out_shapes,
        scratch_shapes=_gla_scratch(),
        compiler_params=_cparams(("parallel", "parallel")),
        name="hgrn2",
    )(*([z_c] * n_c), *([z_x] * n_c), lb_logits, gain, mats, masks)


def _retention_kernel(zq_c, zk_c, zv_c, zg_c, zq_x, zk_x, zv_x, zg_x, cos_ref, sin_ref,
                      dq_ref, dk_ref, gt_ref, dm_ref, gain_ref, oc_ref, ox_ref, acc_c, acc_x):
    c = GLA_CHUNK
    n_slabs = acc_c.shape[1] // LANES
    zq, zk, zv = (zq_c, zq_x), (zk_c, zk_x), (zv_c, zv_x)

    def chunk_fn(d, seg, r0, states):
        outs = []
        for s in range(n_slabs):
            cols = slice(s * LANES, (s + 1) * LANES)
            q = zq[seg][pl.ds(r0, c), cols]
            k = zk[seg][pl.ds(r0, c), cols] * (HEAD_DIM ** -0.5)
            v = zv[seg][pl.ds(r0, c), cols]
            if seg == 1:
                cos = cos_ref[pl.ds(r0, c), :]
                sin = sin_ref[pl.ds(r0, c), :]
                q, k = _rope(q, cos, sin), _rope(k, cos, sin)
            s_cat = dm_ref[s, d] * _dot_nt(q.astype(BF16), _stack_heads(k))
            outs.append(_gla_chunk_update(q * dq_ref[s, d], k * dk_ref[s, d], v, gt_ref[s], s_cat, states[s]))
        return jnp.concatenate([o for o, _ in outs], axis=1), tuple(st for _, st in outs)

    _gla_scan(chunk_fn, acc_c, acc_x)
    gain = gain_ref[...]
    _gla_finalize(acc_c, zg_c, gain, oc_ref)
    _gla_finalize(acc_x, zg_x, gain, ox_ref)


def _retention(z_c, z_x, gain, rope, consts, *, batch):
    dq, dk, gt, dm = consts
    cos, sin = rope
    groups = (G_BQ, G_BK, G_BV, G_BG)
    c_specs, x_specs = _z_specs(groups, GLA_WIDTH)
    out_specs, out_shapes = _mix_out(batch, GLA_WIDTH)
    n_c = len(groups)
    c = GLA_CHUNK
    ns = GLA_WIDTH // LANES
    return pl.pallas_call(
        _retention_kernel,
        grid=(batch, GROUP_WIDTH // GLA_WIDTH),
        in_specs=c_specs + x_specs + [
            pl.BlockSpec((SEQ, LANES), lambda b, p: (0, 0)),
            pl.BlockSpec((SEQ, LANES), lambda b, p: (0, 0)),
            pl.BlockSpec((ns, 2, c, LANES), lambda b, p: (p, 0, 0, 0)),
            pl.BlockSpec((ns, 2, c, LANES), lambda b, p: (p, 0, 0, 0)),
            pl.BlockSpec((ns, 1, LANES), lambda b, p: (p, 0, 0)),
            pl.BlockSpec((ns, 2, c, 2 * c), lambda b, p: (p, 0, 0, 0)),
            pl.BlockSpec((1, LANES), lambda b, p: (0, 0)),
        ],
        out_specs=out_specs,
        out_shape=out_shapes,
        scratch_shapes=_gla_scratch(),
        compiler_params=_cparams(("parallel", "parallel")),
        name="retention",
    )(*([z_c] * n_c), *([z_x] * n_c), cos, sin, dq, dk, gt, dm, gain)


ROPE_ROWS = 512


def _qkv_prep_kernel(zq_ref, zk_ref, zv_ref, cos_ref, sin_ref, q_ref, k_ref, v_ref, *, use_rope):
    for s in range(SLABS):
        cols = slice(s * LANES, (s + 1) * LANES)
        q, k = zq_ref[:, cols], zk_ref[:, cols]
        if use_rope:
            cos, sin = cos_ref[...], sin_ref[...]
            q, k = _rope(q, cos, sin), _rope(k, cos, sin)
        q_ref[:, cols] = (q * (HEAD_DIM ** -0.5)).astype(BF16)
        k_ref[:, cols] = k.astype(BF16)
    v_ref[...] = zv_ref[...].astype(BF16)


def _qkv_prep(z, rope, *, use_rope, rows):
    n = z.shape[0]
    cos, sin = rope
    pos_tiles = SEQ // rows
    zspec = lambda g: pl.BlockSpec((rows, GROUP_WIDTH), lambda i, g=g: (i, g))
    rspec = pl.BlockSpec((rows, LANES), lambda i: (i % pos_tiles, 0))
    ospec = pl.BlockSpec((rows, GROUP_WIDTH), lambda i: (i, 0))
    oshape = jax.ShapeDtypeStruct((n, GROUP_WIDTH), BF16)
    return pl.pallas_call(
        functools.partial(_qkv_prep_kernel, use_rope=use_rope),
        grid=(n // rows,),
        in_specs=[zspec(G_CQ), zspec(G_CK), zspec(G_CV), rspec, rspec],
        out_specs=[ospec, ospec, ospec],
        out_shape=[oshape, oshape, oshape],
        compiler_params=_cparams(("parallel",)),
        name="qkv_prep",
    )(z, z, z, cos, sin)


def _diff_attn_kernel(*refs, lam_init, with_latent):
    if with_latent:
        q_ref, kc_ref, vc_ref, kx_ref, vx_ref, lam_ref, gain_ref, o_ref, kk, vv = refs
    else:
        q_ref, kc_ref, vc_ref, lam_ref, gain_ref, o_ref, kk, vv = refs

    @pl.when(pl.program_id(2) == 0)
    def _():
        kk[0:CTX_LEN, :] = kc_ref[...]
        vv[0:CTX_LEN, :] = vc_ref[...]
        if with_latent:
            kk[CTX_LEN:, :] = kx_ref[...]
            vv[CTX_LEN:, :] = vx_ref[...]

    lv = lam_ref[...]
    lam = (jnp.exp(jnp.sum(lv[0:1] * lv[1:2], axis=1, keepdims=True))
           - jnp.exp(jnp.sum(lv[2:3] * lv[3:4], axis=1, keepdims=True)) + lam_init)
    q = q_ref[...]
    lane = lax.broadcasted_iota(jnp.int32, (1, LANES), 1)
    keys = kk[...]

    def softmax(qh):
        s = _dot_nt(qh, keys)
        p = jnp.exp(s - jnp.max(s, axis=-1, keepdims=True))
        return p / jnp.sum(p, axis=-1, keepdims=True)

    zero = jnp.zeros_like(q)
    a = softmax(jnp.where(lane < HEAD_DIM, q, zero)) - lam * softmax(jnp.where(lane < HEAD_DIM, zero, q))
    o = _dot(a.astype(BF16), vv[...])
    o_ref[...] = (_rms(o, gain_ref[...]) * (1.0 - lam_init)).astype(o_ref.dtype)


def _diff_attn(q, k_c, v_c, k_x, v_x, lam_vecs, gain, *, lam_init, batch, tq):
    with_latent = k_x is not None
    q_len = q.shape[0] // batch
    nq = q_len // tq
    heads = GROUP_WIDTH // LANES
    n_keys = CTX_LEN + (SEQ if with_latent else 0)
    cspec = pl.BlockSpec((CTX_LEN, LANES), lambda b, h, i: (b, h))
    xspec = pl.BlockSpec((SEQ, LANES), lambda b, h, i: (b, h))
    qspec = pl.BlockSpec((tq, LANES), lambda b, h, i: (b * nq + i, h))
    in_specs = [qspec, cspec, cspec] + ([xspec, xspec] if with_latent else []) + [
        pl.BlockSpec((4, HEAD_DIM), lambda b, h, i: (0, 0)),
        pl.BlockSpec((1, LANES), lambda b, h, i: (0, 0)),
    ]
    args = [q, k_c, v_c] + ([k_x, v_x] if with_latent else []) + [lam_vecs, gain]
    return pl.pallas_call(
        functools.partial(_diff_attn_kernel, lam_init=lam_init, with_latent=with_latent),
        grid=(batch, heads, nq),
        in_specs=in_specs,
        out_specs=qspec,
        out_shape=jax.ShapeDtypeStruct(q.shape, BF16),
        scratch_shapes=[pltpu.VMEM((n_keys, LANES), BF16), pltpu.VMEM((n_keys, LANES), BF16)],
        compiler_params=_cparams(("parallel", "parallel", "arbitrary")),
        name="diff_attn",
    )(*args)


CONV_PAD = 8


def _gelu_tanh(t):
    return 0.5 * t * (1.0 + jnp.tanh(math.sqrt(2.0 / math.pi) * (t + 0.044715 * (t * t * t))))


def _softplus(t):
    return jnp.maximum(t, 0.0) + jnp.log1p(jnp.exp(-jnp.abs(t)))


def _rglru_kernel(x_c, g_c, x_x, g_x, cw_ref, cb_ref, wr_ref, br_ref, wi_ref, bi_ref, lam_ref,
                  oc_ref, ox_ref, pad, xc, hs):
    n_tok = CTX_LEN + SEQ
    sc = SCAN_CHUNK
    cw = cw_ref[...]
    cb = cb_ref[...]
    zeros = jnp.zeros((CONV_PAD, LANES), F32)

    for x_ref, base, n in ((x_c, 0, CTX_LEN), (x_x, CTX_LEN, SEQ)):
        pad[0:CONV_PAD, :] = zeros
        pad[CONV_PAD:CONV_PAD + n, :] = x_ref[...]
        pad[CONV_PAD + n:2 * CONV_PAD + n, :] = zeros
        for r in range(n // sc):
            t0 = CONV_PAD + r * sc
            acc = cb + cw[0:1] * pad[t0 - 1:t0 - 1 + sc, :]
            acc += cw[1:2] * pad[t0:t0 + sc, :]
            acc += cw[2:3] * pad[t0 + 1:t0 + 1 + sc, :]
            acc += cw[3:4] * pad[t0 + 2:t0 + 2 + sc, :]
            xc[base + r * sc:base + (r + 1) * sc, :] = acc

    row = lax.broadcasted_iota(jnp.int32, (sc, 1), 0)
    n_chunks = n_tok // sc
    neg_c_softplus = -RG_C * _softplus(-lam_ref[...])
    hs[...] = jnp.zeros_like(hs)

    def scan_chunk(d, r0, carry):
        x = xc[pl.ds(r0, sc), :]
        xb = x.astype(BF16)
        r = _sigmoid(_dot(xb, wr_ref[d]) + br_ref[d:d + 1, :])
        i = _sigmoid(_dot(xb, wi_ref[d]) + bi_ref[d:d + 1, :])
        log_a = neg_c_softplus[d:d + 1, :] * r
        a = jnp.exp(log_a)
        u = jnp.sqrt(-jnp.tanh(log_a) * (a * a + 1.0)) * (i * x)
        s = 1
        while s < sc:
            ok, shift = (row >= s, s) if d == 0 else (row < sc - s, sc - s)
            u_prev = jnp.where(ok, pltpu.roll(u, shift, axis=0), 0.0)
            a_prev = jnp.where(ok, pltpu.roll(a, shift, axis=0), 1.0)
            u = u + a * u_prev
            a = a * a_prev
            s *= 2
        h = u + a * carry
        hs[pl.ds(r0, sc), :] += h
        return h[sc - 1:sc, :] if d == 0 else h[0:1, :]

    def body(t, carries):
        fwd = scan_chunk(0, pl.multiple_of(t * sc, sc), carries[0])
        back = jnp.where(t == 0, 0, n_chunks - t)
        bwd = scan_chunk(1, pl.multiple_of(back * sc, sc), carries[1])
        return fwd, bwd

    zero = jnp.zeros((1, LANES), F32)
    lax.fori_loop(0, n_chunks, body, (zero, zero))

    oc_ref[...] = (_gelu_tanh(g_c[...]) * hs[0:CTX_LEN, :]).astype(oc_ref.dtype)
    for r in range(SEQ // sc):
        rows = slice(r * sc, (r + 1) * sc)
        ox_ref[rows, :] = (_gelu_tanh(g_x[rows, :]) * hs[CTX_LEN + r * sc:CTX_LEN + (r + 1) * sc, :]
                           ).astype(ox_ref.dtype)


def _rglru(z_c, z_x, conv_w, conv_b, w_r, b_r, w_i, b_i, lam, *, batch):
    groups = (G_DX, G_DG)
    c_specs, x_specs = _z_specs(groups, LANES)
    out_specs, out_shapes = _mix_out(batch, LANES)
    vec2 = pl.BlockSpec((2, LANES), lambda b, p: (0, p))
    wspec = pl.BlockSpec((2, LANES, LANES), lambda b, p: (0, p, p))
    return pl.pallas_call(
        _rglru_kernel,
        grid=(batch, SLABS),
        in_specs=c_specs + x_specs + [
            pl.BlockSpec((4, LANES), lambda b, p: (0, p)),
            pl.BlockSpec((1, LANES), lambda b, p: (0, p)),
            wspec, vec2, wspec, vec2, vec2,
        ],
        out_specs=out_specs,
        out_shape=out_shapes,
        scratch_shapes=[pltpu.VMEM((SEQ + 2 * CONV_PAD, LANES), F32),
                        pltpu.VMEM((CTX_LEN + SEQ, LANES), F32),
                        pltpu.VMEM((CTX_LEN + SEQ, LANES), F32)],
        compiler_params=_cparams(("parallel", "parallel")),
        name="rglru",
    )(z_c, z_c, z_x, z_x, conv_w, conv_b, w_r, b_r, w_i, b_i, lam)


def _block_diag_dense(w):
    two, nb, bs, _ = w.shape
    eye = jnp.eye(nb, dtype=w.dtype)
    return jnp.einsum("dgio,gh->dgiho", w, eye).reshape(two, nb * bs, nb * bs)


def kernel(x, c, ctx, c_ctx, w_ada, b_ada, norm_pre, norm_post, ffn_w_in, ffn_w_out, w_in, w_out,
           lb_logits, a_norm, b_norm, c_lambda, c_norm, d_conv_w, d_conv_b, d_w_r, d_b_r, d_w_i,
           d_b_i, d_lambda):
    batch = x.shape[0]
    assert x.shape == (batch, SEQ, D_MODEL) and ctx.shape == (batch, CTX_LEN, D_MODEL)
    assert batch < MOD_ROWS

    gla_consts = _gla_constants()
    ret_consts = _retention_constants()
    rope = _rope_constants()

    c_rows = jnp.zeros((MOD_ROWS, D_MODEL), F32).at[:batch].set(c).at[batch].set(c_ctx)
    mod = _ada_table(c_rows, w_ada, b_ada).reshape(DEPTH * MOD_ROWS, N_MOD, D_MODEL)

    n_c = batch * CTX_LEN
    tm_ffn, tm_proj, tm_out = 1024, 1024, 512
    tm_c = min(n_c, 1024)
    assert n_c % tm_c == 0
    row_x = lambda tm: (lambda i: i // (SEQ // tm))
    row_c = lambda i: batch

    hx = x.reshape(batch * SEQ, D_MODEL)
    hc = ctx.reshape(n_c, D_MODEL)
    tile2 = lambda g: jnp.tile(g.reshape(1, -1), (1, LANES // g.shape[-1]))
    ffn_w_in_b = ffn_w_in.astype(BF16)
    ffn_w_out_b = ffn_w_out.astype(BF16)
    w_in_b = w_in.astype(BF16)
    w_out_b = w_out.astype(BF16)

    def ffn(h, l, idx, is_ctx):
        return _ffn(h, mod, norm_pre[l, 2 * idx].reshape(1, D_MODEL), norm_post[l, 2 * idx].reshape(1, D_MODEL),
                    ffn_w_in_b, ffn_w_out_b, layer=l, idx=idx, k0=6 * idx,
                    row_of_tile=row_c if is_ctx else row_x(tm_ffn), tm=tm_c if is_ctx else tm_ffn)

    for l in range(DEPTH):
        ctx_out = l < DEPTH - 1
        gpre1 = norm_pre[l, 1].reshape(1, D_MODEL)
        gpost1 = norm_post[l, 1].reshape(1, D_MODEL)

        hc = ffn(hc, l, 0, True)
        hx = ffn(hx, l, 0, False)

        z_c = _inproj(hc, mod, gpre1, w_in_b, layer=l, row_of_tile=row_c, tm=tm_c)
        z_x = _inproj(hx, mod, gpre1, w_in_b, layer=l, row_of_tile=row_x(tm_proj), tm=tm_proj)

        a_c, a_x = _hgrn2(z_c, z_x, lb_logits, tile2(a_norm[l]), gla_consts, layer=l, batch=batch)
        b_c, b_x = _retention(z_c, z_x, tile2(b_norm[l]), rope, ret_consts, batch=batch)

        lam_init = 0.8 - 0.6 * math.exp(-0.3 * l)
        cgain = c_norm[l].reshape(1, LANES)
        qc, kc, vc = _qkv_prep(z_c, rope, use_rope=False, rows=CTX_LEN)
        qx, kx, vx = _qkv_prep(z_x, rope, use_rope=True, rows=ROPE_ROWS)
        c_x = _diff_attn(qx, kc, vc, kx, vx, c_lambda[l], cgain, lam_init=lam_init, batch=batch, tq=256)

        d_c, d_x = _rglru(z_c, z_x, d_conv_w[l], d_conv_b[l].reshape(1, -1),
                          _block_diag_dense(d_w_r[l]).astype(BF16), d_b_r[l],
                          _block_diag_dense(d_w_i[l]).astype(BF16), d_b_i[l], d_lambda[l], batch=batch)

        hx = _outproj((a_x, b_x, c_x, d_x), w_out_b[l], hx, mod, gpost1, layer=l,
                      row_of_tile=row_x(tm_out), tm=tm_out)
        hx = ffn(hx, l, 1, False)

        if ctx_out:
            c_c = _diff_attn(qc, kc, vc, None, None, c_lambda[l], cgain, lam_init=lam_init, batch=batch, tq=CTX_LEN)
            hc = _outproj((a_c, b_c, c_c, d_c), w_out_b[l], hc, mod, gpost1, layer=l,
                          row_of_tile=row_c, tm=min(tm_c, tm_out))
            hc = ffn(hc, l, 1, True)

    return hx.reshape(batch, SEQ, D_MODEL)
```

```python
import functools
import math

import numpy as np
import jax
import jax.numpy as jnp
from jax import lax
from jax.experimental import pallas as pl
from jax.experimental.pallas import tpu as pltpu

F32 = jnp.float32
BF16 = jnp.bfloat16

D_MODEL = 2048
SEQ = 2048
CTX_LEN = 256
DEPTH = 2
GRID_W = 64
HEAD_DIM = 64
GROUP_WIDTH = 512
IN_WIDTH = 7168
D_FF = 5504
N_MOD = 9
ROPE_BASE = 10000.0
EPS = 1e-6
RG_C = 8.0
FFN_RESIDUAL = 0.5

LANES = 128
FF_TILE = 512
FF_TILE_EMIT = 256
MOD_ROWS = 8
GLA_CHUNK = 128
HGRN2_WIDTH = 512
RET_WIDTH = 512
GLA_LEVELS = 7
SCAN_CHUNK = 256
VMEM_LIMIT = 56 * 1024 * 1024

G_AQ, G_AFF, G_AFB, G_AV, G_AG = 0, 1, 2, 3, 4
G_BQ, G_BK, G_BV, G_BG = 5, 6, 7, 8
G_CQ, G_CK, G_CV = 9, 10, 11
G_DX, G_DG = 12, 13
SLABS = GROUP_WIDTH // LANES


def _cparams(sem):
    return pltpu.CompilerParams(dimension_semantics=sem, vmem_limit_bytes=VMEM_LIMIT)


def _sigmoid(t):
    return jax.nn.sigmoid(t)


def _silu(t):
    return t * _sigmoid(t)


def _rms(t, gain):
    return t * lax.rsqrt(jnp.mean(t * t, axis=-1, keepdims=True) + EPS) * gain


def _dot(a, b):
    return jnp.dot(a, b, preferred_element_type=F32)


def _dot_nt(a, b):
    return lax.dot_general(a, b, (((1,), (1,)), ((), ())), preferred_element_type=F32)


def _ada_kernel(c_ref, w_ref, b_ref, o_ref):
    s = _silu(c_ref[...])
    s_hi = s.astype(BF16).astype(F32)
    s_mid = (s - s_hi).astype(BF16).astype(F32)
    s_lo = s - s_hi - s_mid
    w = w_ref[...]
    w_hi = w.astype(BF16)
    w_lo = (w - w_hi.astype(F32)).astype(BF16)
    r_hi = _dot(jnp.concatenate([s_hi, s_mid, s_lo], axis=0).astype(BF16), w_hi)
    r_lo = _dot(jnp.concatenate([s_hi, s_mid], axis=0).astype(BF16), w_lo)
    m = MOD_ROWS
    o_ref[...] = (r_hi[0:m] + r_hi[m:2 * m] + r_hi[2 * m:3 * m]) + (r_lo[0:m] + r_lo[m:2 * m]) + b_ref[...]


def _ada_table(c_rows, w_ada, b_ada):
    tn = 1024
    n = N_MOD * D_MODEL
    return pl.pallas_call(
        _ada_kernel,
        grid=(DEPTH, n // tn),
        in_specs=[
            pl.BlockSpec((MOD_ROWS, D_MODEL), lambda l, j: (0, 0)),
            pl.BlockSpec((None, D_MODEL, tn), lambda l, j: (l, 0, j)),
            pl.BlockSpec((None, 1, tn), lambda l, j: (l, 0, j)),
        ],
        out_specs=pl.BlockSpec((None, MOD_ROWS, tn), lambda l, j: (l, 0, j)),
        out_shape=jax.ShapeDtypeStruct((DEPTH, MOD_ROWS, n), F32),
        compiler_params=_cparams(("arbitrary", "arbitrary")),
        name="ada_table",
    )(c_rows, w_ada, b_ada.reshape(DEPTH, 1, n))


ROW_CHUNK = 256
MM_ROWS = 512
MM_COLS = 512
OUT_ROWS = 256


def _norm_modulate_to(u_scr, h_ref, gain_ref, mod_ref, k_shift, k_scale):
    tm = h_ref.shape[0]
    shift = mod_ref[k_shift:k_shift + 1, :]
    gain_scale = gain_ref[...] * (1.0 + mod_ref[k_scale:k_scale + 1, :])

    def body(r, carry):
        r0 = pl.multiple_of(r * ROW_CHUNK, ROW_CHUNK)
        h = h_ref[pl.ds(r0, ROW_CHUNK), :]
        u_scr[pl.ds(r0, ROW_CHUNK), :] = (_rms(h, gain_scale) + shift).astype(BF16)
        return carry

    lax.fori_loop(0, tm // ROW_CHUNK, body, 0)


def _ffn_kernel(h_ref, mod_ref, gpre_ref, gpost_ref, wg_ref, wu_ref, wout_ref, *rest, k0, emit):
    if emit:
        o_ref, wg_out, wu_out, wout_out, u_scr = rest
    else:
        o_ref, u_scr = rest
    j = pl.program_id(1)
    last = pl.num_programs(1) - 1
    tm = h_ref.shape[0]
    tf = wg_ref.shape[1]
    overlap = pl.cdiv(D_FF, tf) * tf - D_FF

    @pl.when(j == 0)
    def _():
        _norm_modulate_to(u_scr, h_ref, gpre_ref, mod_ref, k0, k0 + 1)
        o_ref[...] = jnp.zeros_like(o_ref)

    wg, wu, wout = wg_ref[...], wu_ref[...], wout_ref[...]
    if emit:
        wg, wu, wout = wg.astype(BF16), wu.astype(BF16), wout.astype(BF16)

        def placed(t, axis):
            if overlap == 0:
                return t
            keep = lax.slice_in_dim(t, overlap, tf, axis=axis)
            fill = jnp.zeros_like(lax.slice_in_dim(t, 0, overlap, axis=axis))
            return jnp.where(j == last, jnp.concatenate([keep, fill], axis=axis), t)

        wg_out[...], wu_out[...], wout_out[...] = placed(wg, 1), placed(wu, 1), placed(wout, 0)

    col = lax.broadcasted_iota(jnp.int32, (1, tf), 1)
    fresh = jnp.logical_or(j < last, col >= overlap)
    for r in range(tm // MM_ROWS):
        rows = slice(r * MM_ROWS, (r + 1) * MM_ROWS)
        u = u_scr[rows, :]
        act = jnp.where(fresh, _silu(_dot(u, wg)) * _dot(u, wu), 0.0).astype(BF16)
        for n0 in range(0, D_MODEL, MM_COLS):
            o_ref[rows, n0:n0 + MM_COLS] += _dot(act, wout[:, n0:n0 + MM_COLS])

    @pl.when(j == last)
    def _():
        gate_gain = FFN_RESIDUAL * mod_ref[k0 + 2:k0 + 3, :] * gpost_ref[...]

        def body(r, carry):
            r0 = pl.multiple_of(r * ROW_CHUNK, ROW_CHUNK)
            y = o_ref[pl.ds(r0, ROW_CHUNK), :]
            o_ref[pl.ds(r0, ROW_CHUNK), :] = h_ref[pl.ds(r0, ROW_CHUNK), :] + _rms(y, gate_gain)
            return carry

        lax.fori_loop(0, tm // ROW_CHUNK, body, 0)


def _mod_spec(layer, row_of_tile):
    return pl.BlockSpec((None, N_MOD, D_MODEL), lambda i, *_: (layer * MOD_ROWS + row_of_tile(i), 0, 0))


def _ff_offset(j, tf, base=0):
    return (base // LANES + jnp.minimum(j * (tf // LANES), (D_FF - tf) // LANES)) * LANES


def _ffn(h, mod, gpre, gpost, weights, *, layer, idx, k0, row_of_tile, tm, emit):
    n = h.shape[0]
    el = pl.Element
    tf = FF_TILE_EMIT if emit else FF_TILE
    col_tile = pl.BlockSpec((el(D_MODEL), el(tf)), lambda i, j: (0, _ff_offset(j, tf)))
    row_tile = pl.BlockSpec((el(tf), el(D_MODEL)), lambda i, j: (_ff_offset(j, tf), 0))
    h_spec = pl.BlockSpec((tm, D_MODEL), lambda i, j: (i, 0))
    out_specs, out_shape = h_spec, jax.ShapeDtypeStruct((n, D_MODEL), F32)
    if emit:
        assert n == tm, "weight tiles are cast once: one row tile only"
        w_in, w_out = weights
        w_args = (w_in, w_in, w_out)
        w_specs = [
            pl.BlockSpec((None, None, el(D_MODEL), el(tf)), lambda i, j: (layer, idx, 0, _ff_offset(j, tf))),
            pl.BlockSpec((None, None, el(D_MODEL), el(tf)), lambda i, j: (layer, idx, 0, _ff_offset(j, tf, D_FF))),
            pl.BlockSpec((None, None, el(tf), el(D_MODEL)), lambda i, j: (layer, idx, _ff_offset(j, tf), 0)),
        ]
        d_ff_pad = pl.cdiv(D_FF, tf) * tf
        cols_out = pl.BlockSpec((D_MODEL, tf), lambda i, j: (0, j))
        out_specs = [h_spec, cols_out, cols_out, pl.BlockSpec((tf, D_MODEL), lambda i, j: (j, 0))]
        out_shape = [out_shape, jax.ShapeDtypeStruct((D_MODEL, d_ff_pad), BF16),
                     jax.ShapeDtypeStruct((D_MODEL, d_ff_pad), BF16), jax.ShapeDtypeStruct((d_ff_pad, D_MODEL), BF16)]
    else:
        w_args = weights
        w_specs = [col_tile, col_tile, row_tile]
    out = pl.pallas_call(
        functools.partial(_ffn_kernel, k0=k0, emit=emit),
        grid=(n // tm, pl.cdiv(D_FF, tf)),
        in_specs=[
            h_spec,
            _mod_spec(layer, row_of_tile),
            pl.BlockSpec((1, D_MODEL), lambda i, j: (0, 0)),
            pl.BlockSpec((1, D_MODEL), lambda i, j: (0, 0)),
        ] + w_specs,
        out_specs=out_specs,
        out_shape=out_shape,
        scratch_shapes=[pltpu.VMEM((tm, D_MODEL), BF16)],
        compiler_params=_cparams(("parallel", "arbitrary")),
        name="ffn_cast" if emit else "ffn",
    )(h, mod, gpre, gpost, *w_args)
    return (out[0], tuple(out[1:])) if emit else out


def _inproj_kernel(h_ref, mod_ref, gpre_ref, w_ref, *rest, emit):
    if emit:
        z_ref, w_out, u_scr = rest
    else:
        z_ref, u_scr = rest
    j = pl.program_id(1)
    tm = h_ref.shape[0]

    @pl.when(j == 0)
    def _():
        _norm_modulate_to(u_scr, h_ref, gpre_ref, mod_ref, 3, 4)

    w = w_ref[...]
    if emit:
        w = w.astype(BF16)
        w_out[...] = w
    for r in range(tm // MM_ROWS):
        rows = slice(r * MM_ROWS, (r + 1) * MM_ROWS)
        z_ref[rows, :] = _dot(u_scr[rows, :], w)


def _inproj(h, mod, gpre, w, *, layer, row_of_tile, tm, emit):
    n = h.shape[0]
    tn = 512 if emit else 1024
    if emit:
        assert n == tm, "weight tiles are cast once: one row tile only"
        w_spec = pl.BlockSpec((None, D_MODEL, tn), lambda i, j: (layer, 0, j))
    else:
        w_spec = pl.BlockSpec((D_MODEL, tn), lambda i, j: (0, j))
    z_spec = pl.BlockSpec((tm, tn), lambda i, j: (i, j))
    z_shape = jax.ShapeDtypeStruct((n, IN_WIDTH), F32)
    out = pl.pallas_call(
        functools.partial(_inproj_kernel, emit=emit),
        grid=(n // tm, IN_WIDTH // tn),
        in_specs=[
            pl.BlockSpec((tm, D_MODEL), lambda i, j: (i, 0)),
            _mod_spec(layer, row_of_tile),
            pl.BlockSpec((1, D_MODEL), lambda i, j: (0, 0)),
            w_spec,
        ],
        out_specs=[z_spec, pl.BlockSpec((D_MODEL, tn), lambda i, j: (0, j))] if emit else z_spec,
        out_shape=[z_shape, jax.ShapeDtypeStruct((D_MODEL, IN_WIDTH), BF16)] if emit else z_shape,
        scratch_shapes=[pltpu.VMEM((tm, D_MODEL), BF16)],
        compiler_params=_cparams(("parallel", "arbitrary")),
        name="inproj_cast" if emit else "inproj",
    )(h, mod, gpre, w)
    return tuple(out) if emit else out


def _outproj_kernel(a_ref, b_ref, c_ref, d_ref, w_ref, h_ref, mod_ref, gpost_ref, o_ref):
    tm = h_ref.shape[0]
    gate_gain = mod_ref[5:6, :] * gpost_ref[...]
    for r in range(tm // OUT_ROWS):
        rows = slice(r * OUT_ROWS, (r + 1) * OUT_ROWS)
        y = _dot(a_ref[rows, :], w_ref[0 * GROUP_WIDTH:1 * GROUP_WIDTH, :])
        y += _dot(b_ref[rows, :], w_ref[1 * GROUP_WIDTH:2 * GROUP_WIDTH, :])
        y += _dot(c_ref[rows, :], w_ref[2 * GROUP_WIDTH:3 * GROUP_WIDTH, :])
        y += _dot(d_ref[rows, :], w_ref[3 * GROUP_WIDTH:4 * GROUP_WIDTH, :])
        o_ref[rows, :] = h_ref[rows, :] + _rms(y, gate_gain)


def _outproj(mix, w, h, mod, gpost, *, layer, row_of_tile, tm):
    n = h.shape[0]
    mix_spec = pl.BlockSpec((tm, GROUP_WIDTH), lambda i: (i, 0))
    return pl.pallas_call(
        _outproj_kernel,
        grid=(n // tm,),
        in_specs=[
            mix_spec, mix_spec, mix_spec, mix_spec,
            pl.BlockSpec((D_MODEL, D_MODEL), lambda i: (0, 0)),
            pl.BlockSpec((tm, D_MODEL), lambda i: (i, 0)),
            _mod_spec(layer, row_of_tile),
            pl.BlockSpec((1, D_MODEL), lambda i: (0, 0)),
        ],
        out_specs=pl.BlockSpec((tm, D_MODEL), lambda i: (i, 0)),
        out_shape=jax.ShapeDtypeStruct((n, D_MODEL), F32),
        compiler_params=_cparams(("parallel",)),
        name="outproj",
    )(*mix, w, h, mod, gpost)


def _gla_constants():
    c = GLA_CHUNK
    idx = np.arange(c)
    i, r = idx[:, None], idx[None, :]
    mats = np.zeros((2, GLA_LEVELS + 2, c, c), np.float32)
    masks = np.zeros((2, GLA_LEVELS + 1, c, c), np.float32)
    mats[0, 0] = r <= i
    mats[0, 1] = r > i
    mats[1, 0] = r >= i
    mats[1, 1] = r < i
    for m in range(GLA_LEVELS):
        hs = c >> (m + 1)
        start = (idx // (2 * hs)) * (2 * hs)
        mid = (start + hs - 1)[:, None]
        right = ((idx // hs) % 2 == 1)[:, None]
        mats[0, 2 + m] = np.where(right, (r > mid) & (r <= i), (r > i) & (r <= mid))
        mats[1, 2 + m] = np.where(right, (r > mid) & (r < i), (r >= i) & (r <= mid))
        same = (start[:, None] == start[None, :])
        q_right = right & ~right.T & same
        masks[0, m] = q_right
        masks[1, m] = q_right.T
    masks[0, GLA_LEVELS] = np.eye(c)
    masks[1, GLA_LEVELS] = np.eye(c)
    mats = mats.reshape(2, (GLA_LEVELS + 2) * c, c)
    mats = np.concatenate([mats, mats], axis=-1)
    masks = np.tile(masks, (1, 1, 1, 2))
    return jnp.asarray(mats, BF16), jnp.asarray(masks, F32)


def _retention_constants():
    c = GLA_CHUNK
    heads = GROUP_WIDTH // HEAD_DIM
    log_decay = np.log1p(-np.exp2(-5.0 - np.arange(heads, dtype=np.float64)))
    ld_lane = np.repeat(log_decay, HEAD_DIM).reshape(SLABS, 1, LANES)
    idx = np.arange(c, dtype=np.float64)
    i, j = idx[:, None], idx[None, :]
    dq = np.zeros((SLABS, 2, c, LANES))
    dk = np.zeros((SLABS, 2, c, LANES))
    dq[:, 0] = np.exp((idx[None, :, None] + 1.0) * ld_lane)
    dk[:, 0] = np.exp((c - 1.0 - idx[None, :, None]) * ld_lane)
    dq[:, 1] = np.exp((c - idx[None, :, None]) * ld_lane)
    dk[:, 1] = np.exp(idx[None, :, None] * ld_lane)
    gt = np.exp(c * ld_lane)
    dm = np.zeros((SLABS, 2, 2, c, c))
    for p in range(SLABS):
        for hh in range(2):
            ld = log_decay[2 * p + hh]
            dm[p, 0, hh] = np.where(i >= j, np.exp((i - j) * ld), 0.0)
            dm[p, 1, hh] = np.where(j >= i, np.exp((j - i) * ld), 0.0)
    dm = dm.transpose(0, 1, 3, 2, 4).reshape(SLABS, 2, c, 2 * c)
    f = lambda t: jnp.asarray(t.astype(np.float32))
    return f(dq), f(dk), f(gt), f(dm)


def _rope_constants():
    quarter = HEAD_DIM // 4
    inv_freq = ROPE_BASE ** (-jnp.arange(quarter, dtype=F32) / quarter)
    rows = SEQ // GRID_W
    row = jnp.repeat(jnp.arange(rows, dtype=F32), GRID_W)
    col = jnp.tile(jnp.arange(GRID_W, dtype=F32), rows)
    ang = jnp.concatenate([row[:, None] * inv_freq, col[:, None] * inv_freq], axis=-1)
    cos, sin = jnp.cos(ang), jnp.sin(ang)
    reps = LANES // HEAD_DIM
    return (jnp.tile(jnp.concatenate([cos, cos], axis=-1), (1, reps)),
            jnp.tile(jnp.concatenate([-sin, sin], axis=-1), (1, reps)))


def _rope(t, cos, sin):
    lane = lax.broadcasted_iota(jnp.int32, t.shape, 1)
    first_half = (lane % HEAD_DIM) < (HEAD_DIM // 2)
    partner = jnp.where(first_half,
                        pltpu.roll(t, LANES - HEAD_DIM // 2, axis=1),
                        pltpu.roll(t, HEAD_DIM // 2, axis=1))
    return t * cos + partner * sin


def _head_lane_masks():
    lane = lax.broadcasted_iota(jnp.int32, (1, LANES), 1)
    return lane < HEAD_DIM, lane >= HEAD_DIM


def _stack_heads(t):
    hm0, hm1 = _head_lane_masks()
    return jnp.concatenate([jnp.where(hm0, t, 0.0).astype(BF16),
                            jnp.where(hm1, t, 0.0).astype(BF16)], axis=0)


def _gla_chunk_update(qe, ke, v, gtot, s_cat, st):
    o = _dot_nt(qe.astype(BF16), st.astype(BF16))
    o += _dot(s_cat.astype(BF16), _stack_heads(v))
    upd = _dot(v.T.astype(BF16), ke.astype(BF16))
    row_head = lax.broadcasted_iota(jnp.int32, (LANES, LANES), 0) // HEAD_DIM
    col_head = lax.broadcasted_iota(jnp.int32, (LANES, LANES), 1) // HEAD_DIM
    st = st * gtot + jnp.where(row_head == col_head, upd, 0.0)
    return o, st


def _gla_scan(chunk_fn, acc_c, acc_x):
    n_slabs = acc_c.shape[1] // LANES
    acc_c[...] = jnp.zeros_like(acc_c)
    acc_x[...] = jnp.zeros_like(acc_x)

    def run(seg, acc_ref, states):
        n = acc_ref.shape[0] // GLA_CHUNK

        def body(t, states):
            out = []
            for d in (0, 1):
                r0 = pl.multiple_of((t if d == 0 else n - 1 - t) * GLA_CHUNK, GLA_CHUNK)
                o, st = chunk_fn(d, seg, r0, states[d])
                acc_ref[pl.ds(r0, GLA_CHUNK), :] += o
                out.append(st)
            return tuple(out)

        return lax.fori_loop(0, n, body, states)

    zero = tuple(jnp.zeros((LANES, LANES), F32) for _ in range(n_slabs))
    run(1, acc_x, run(0, acc_c, (zero, zero)))


def _slab(t, s):
    return t[:, s * LANES:(s + 1) * LANES]


def _gla_finalize(acc_ref, zg_ref, gain, o_ref):
    hm0, _ = _head_lane_masks()
    n_slabs = acc_ref.shape[1] // LANES

    def body(c, carry):
        r0 = pl.multiple_of(c * GLA_CHUNK, GLA_CHUNK)
        for s in range(n_slabs):
            cols = slice(s * LANES, (s + 1) * LANES)
            o = acc_ref[pl.ds(r0, GLA_CHUNK), cols]
            sq = o * o
            ms0 = jnp.sum(jnp.where(hm0, sq, 0.0), axis=1, keepdims=True) * (1.0 / HEAD_DIM)
            ms1 = jnp.sum(jnp.where(hm0, 0.0, sq), axis=1, keepdims=True) * (1.0 / HEAD_DIM)
            inv = jnp.where(hm0, lax.rsqrt(ms0 + EPS), lax.rsqrt(ms1 + EPS))
            g = zg_ref[pl.ds(r0, GLA_CHUNK), cols]
            o_ref[pl.ds(r0, GLA_CHUNK), cols] = (o * inv * gain * _silu(g)).astype(o_ref.dtype)
        return carry

    lax.fori_loop(0, acc_ref.shape[0] // GLA_CHUNK, body, 0)


def _hgrn2_kernel(zq_c, zff_c, zfb_c, zv_c, zg_c, zq_x, zff_x, zfb_x, zv_x, zg_x,
                  lbl_ref, gain_ref, mats_ref, masks_ref, oc_ref, ox_ref, acc_c, acc_x, *, layer):
    c = GLA_CHUNK
    n_slabs = acc_c.shape[1] // LANES
    lbl = lbl_ref[...]
    e = jnp.exp(lbl - jnp.max(lbl, axis=0, keepdims=True))
    sm = e / jnp.sum(e, axis=0, keepdims=True)
    lower = jnp.zeros_like(sm[0])
    for l in range(1, layer + 1):
        lower = lower + sm[l]
    zq = (zq_c, zq_x)
    zf = ((zff_c, zff_x), (zfb_c, zfb_x))
    zv = (zv_c, zv_x)
    row = lax.broadcasted_iota(jnp.int32, (c, 1), 0)

    def chunk_fn(d, seg, r0, states):
        lb = lower[d:d + 1, :]
        f = lb + (1.0 - lb) * _sigmoid(zf[d][seg][pl.ds(r0, c), :])
        lf = jnp.log(f)
        k = 1.0 - f
        q = _silu(zq[seg][pl.ds(r0, c), :])
        v = zv[seg][pl.ds(r0, c), :]
        lf_hi = lf.astype(BF16)
        lf_lo = (lf - lf_hi.astype(F32)).astype(BF16)
        ex = jnp.exp(_dot(mats_ref[d], jnp.concatenate([lf_hi, lf_lo], axis=0)))
        qe = q * ex[0:c]
        ke = k * ex[c:2 * c]
        gtot = jnp.exp(jnp.sum(lf, axis=0, keepdims=True))
        s_cats = [jnp.zeros((c, 2 * c), F32) for _ in range(n_slabs)]
        for lvl in range(GLA_LEVELS + 1):
            if lvl < GLA_LEVELS:
                hs = c >> (lvl + 1)
                q_side = ((row // hs) % 2) == (1 if d == 0 else 0)
                rhs_src = jnp.where(q_side, q, k) * ex[(2 + lvl) * c:(3 + lvl) * c]
                lhs = rhs_src.astype(BF16)
            else:
                lhs, rhs_src = q.astype(BF16), k
            mask = masks_ref[d, lvl]
            for s in range(n_slabs):
                s_cats[s] = s_cats[s] + mask * _dot_nt(_slab(lhs, s), _stack_heads(_slab(rhs_src, s)))
        outs = [_gla_chunk_update(_slab(qe, s), _slab(ke, s), _slab(v, s), _slab(gtot, s), s_cats[s],
                                  states[s]) for s in range(n_slabs)]
        return jnp.concatenate([o for o, _ in outs], axis=1), tuple(st for _, st in outs)

    _gla_scan(chunk_fn, acc_c, acc_x)
    gain = gain_ref[...]
    _gla_finalize(acc_c, zg_c, gain, oc_ref)
    _gla_finalize(acc_x, zg_x, gain, ox_ref)


def _z_specs(groups, width, buffers=2):
    per_group = GROUP_WIDTH // width
    mode = dict(pipeline_mode=pl.Buffered(buffers)) if buffers != 2 else {}
    c_specs = [pl.BlockSpec((CTX_LEN, width), lambda b, p, g=g: (b, g * per_group + p), **mode) for g in groups]
    x_specs = [pl.BlockSpec((SEQ, width), lambda b, p, g=g: (b, g * per_group + p), **mode) for g in groups]
    return c_specs, x_specs


def _mix_out(batch, width):
    specs = [pl.BlockSpec((CTX_LEN, width), lambda b, p: (b, p)),
             pl.BlockSpec((SEQ, width), lambda b, p: (b, p))]
    shapes = [jax.ShapeDtypeStruct((batch * CTX_LEN, GROUP_WIDTH), BF16),
              jax.ShapeDtypeStruct((batch * SEQ, GROUP_WIDTH), BF16)]
    return specs, shapes


def _gla_scratch(width):
    return [pltpu.VMEM((CTX_LEN, width), F32), pltpu.VMEM((SEQ, width), F32)]


def _hgrn2(z_c, z_x, lb_logits, gain, consts, *, layer, batch):
    mats, masks = consts
    groups = (G_AQ, G_AFF, G_AFB, G_AV, G_AG)
    width = HGRN2_WIDTH
    c_specs, x_specs = _z_specs(groups, width, buffers=1)
    out_specs, out_shapes = _mix_out(batch, width)
    n_c = len(groups)
    return pl.pallas_call(
        functools.partial(_hgrn2_kernel, layer=layer),
        grid=(batch, GROUP_WIDTH // width),
        in_specs=c_specs + x_specs + [
            pl.BlockSpec((DEPTH, 2, width), lambda b, p: (0, 0, p)),
            pl.BlockSpec((1, LANES), lambda b, p: (0, 0)),
            pl.BlockSpec(mats.shape, lambda b, p: (0, 0, 0)),
            pl.BlockSpec(masks.shape, lambda b, p: (0, 0, 0, 0)),
        ],
        out_specs=out_specs,
        out_shape=out_shapes,
        scratch_shapes=_gla_scratch(width),
        compiler_params=_cparams(("parallel", "parallel")),
        name="hgrn2",
    )(*([z_c] * n_c), *([z_x] * n_c), lb_logits, gain, mats, masks)


def _retention_kernel(zq_c, zk_c, zv_c, zg_c, zq_x, zk_x, zv_x, zg_x, cos_ref, sin_ref,
                      dq_ref, dk_ref, gt_ref, dm_ref, gain_ref, oc_ref, ox_ref, acc_c, acc_x):
    c = GLA_CHUNK
    n_slabs = acc_c.shape[1] // LANES
    zq, zk, zv = (zq_c, zq_x), (zk_c, zk_x), (zv_c, zv_x)

    def chunk_fn(d, seg, r0, states):
        outs = []
        for s in range(n_slabs):
            cols = slice(s * LANES, (s + 1) * LANES)
            q = zq[seg][pl.ds(r0, c), cols]
            k = zk[seg][pl.ds(r0, c), cols] * (HEAD_DIM ** -0.5)
            v = zv[seg][pl.ds(r0, c), cols]
            if seg == 1:
                cos = cos_ref[pl.ds(r0, c), :]
                sin = sin_ref[pl.ds(r0, c), :]
                q, k = _rope(q, cos, sin), _rope(k, cos, sin)
            s_cat = dm_ref[s, d] * _dot_nt(q.astype(BF16), _stack_heads(k))
            outs.append(_gla_chunk_update(q * dq_ref[s, d], k * dk_ref[s, d], v, gt_ref[s], s_cat, states[s]))
        return jnp.concatenate([o for o, _ in outs], axis=1), tuple(st for _, st in outs)

    _gla_scan(chunk_fn, acc_c, acc_x)
    gain = gain_ref[...]
    _gla_finalize(acc_c, zg_c, gain, oc_ref)
    _gla_finalize(acc_x, zg_x, gain, ox_ref)


def _retention(z_c, z_x, gain, rope, consts, *, batch):
    dq, dk, gt, dm = consts
    cos, sin = rope
    groups = (G_BQ, G_BK, G_BV, G_BG)
    width = RET_WIDTH
    c_specs, x_specs = _z_specs(groups, width, buffers=1)
    out_specs, out_shapes = _mix_out(batch, width)
    n_c = len(groups)
    c = GLA_CHUNK
    ns = width // LANES
    return pl.pallas_call(
        _retention_kernel,
        grid=(batch, GROUP_WIDTH // width),
        in_specs=c_specs + x_specs + [
            pl.BlockSpec((SEQ, LANES), lambda b, p: (0, 0)),
            pl.BlockSpec((SEQ, LANES), lambda b, p: (0, 0)),
            pl.BlockSpec((ns, 2, c, LANES), lambda b, p: (p, 0, 0, 0)),
            pl.BlockSpec((ns, 2, c, LANES), lambda b, p: (p, 0, 0, 0)),
            pl.BlockSpec((ns, 1, LANES), lambda b, p: (p, 0, 0)),
            pl.BlockSpec((ns, 2, c, 2 * c), lambda b, p: (p, 0, 0, 0)),
            pl.BlockSpec((1, LANES), lambda b, p: (0, 0)),
        ],
        out_specs=out_specs,
        out_shape=out_shapes,
        scratch_shapes=_gla_scratch(width),
        compiler_params=_cparams(("parallel", "parallel")),
        name="retention",
    )(*([z_c] * n_c), *([z_x] * n_c), cos, sin, dq, dk, gt, dm, gain)


ATTN_FILL_ROWS = 256


def _diff_attn_kernel(*refs, lam_init, with_latent):
    if with_latent:
        q_ref, kc_ref, vc_ref, kx_ref, vx_ref, cos_ref, sin_ref, lam_ref, gain_ref, o_ref, kk, vv = refs
    else:
        q_ref, kc_ref, vc_ref, lam_ref, gain_ref, o_ref, kk, vv = refs
    tq = q_ref.shape[0]

    @pl.when(pl.program_id(2) == 0)
    def _():
        kk[0:CTX_LEN, :] = kc_ref[...].astype(BF16)
        vv[0:CTX_LEN, :] = vc_ref[...].astype(BF16)
        if with_latent:
            for r0 in range(0, SEQ, ATTN_FILL_ROWS):
                rows = slice(r0, r0 + ATTN_FILL_ROWS)
                kk[CTX_LEN + r0:CTX_LEN + r0 + ATTN_FILL_ROWS, :] = _rope(
                    kx_ref[rows, :], cos_ref[rows, :], sin_ref[rows, :]).astype(BF16)
                vv[CTX_LEN + r0:CTX_LEN + r0 + ATTN_FILL_ROWS, :] = vx_ref[rows, :].astype(BF16)

    lv = lam_ref[...]
    lam = (jnp.exp(jnp.sum(lv[0:1] * lv[1:2], axis=1, keepdims=True))
           - jnp.exp(jnp.sum(lv[2:3] * lv[3:4], axis=1, keepdims=True)) + lam_init)
    q = q_ref[...]
    if with_latent:
        q0 = pl.multiple_of(pl.program_id(2) * tq, tq)
        q = _rope(q, cos_ref[pl.ds(q0, tq), :], sin_ref[pl.ds(q0, tq), :])
    q = (q * (HEAD_DIM ** -0.5)).astype(BF16)
    lane = lax.broadcasted_iota(jnp.int32, (1, LANES), 1)
    keys = kk[...]

    def softmax(qh):
        s = _dot_nt(qh, keys)
        p = jnp.exp(s - jnp.max(s, axis=-1, keepdims=True))
        return p / jnp.sum(p, axis=-1, keepdims=True)

    zero = jnp.zeros_like(q)
    a = softmax(jnp.where(lane < HEAD_DIM, q, zero)) - lam * softmax(jnp.where(lane < HEAD_DIM, zero, q))
    o = _dot(a.astype(BF16), vv[...])
    o_ref[...] = (_rms(o, gain_ref[...]) * (1.0 - lam_init)).astype(o_ref.dtype)


def _diff_attn(z_q, z_c, z_x, rope, lam_vecs, gain, *, lam_init, batch, tq):
    with_latent = z_x is not None
    q_len = z_q.shape[0] // batch
    nq = q_len // tq
    heads = GROUP_WIDTH // LANES
    n_keys = CTX_LEN + (SEQ if with_latent else 0)
    slab = lambda rows, g: pl.BlockSpec((rows, LANES), lambda b, h, i, g=g: (b, g * SLABS + h))
    qspec = pl.BlockSpec((tq, LANES), lambda b, h, i: (b * nq + i, G_CQ * SLABS + h))
    table = pl.BlockSpec((SEQ, LANES), lambda b, h, i: (0, 0))
    in_specs = [qspec, slab(CTX_LEN, G_CK), slab(CTX_LEN, G_CV)]
    args = [z_q, z_c, z_c]
    if with_latent:
        in_specs += [slab(SEQ, G_CK), slab(SEQ, G_CV), table, table]
        args += [z_x, z_x, *rope]
    in_specs += [pl.BlockSpec((4, HEAD_DIM), lambda b, h, i: (0, 0)),
                 pl.BlockSpec((1, LANES), lambda b, h, i: (0, 0))]
    args += [lam_vecs, gain]
    return pl.pallas_call(
        functools.partial(_diff_attn_kernel, lam_init=lam_init, with_latent=with_latent),
        grid=(batch, heads, nq),
        in_specs=in_specs,
        out_specs=pl.BlockSpec((tq, LANES), lambda b, h, i: (b * nq + i, h)),
        out_shape=jax.ShapeDtypeStruct((z_q.shape[0], GROUP_WIDTH), BF16),
        scratch_shapes=[pltpu.VMEM((n_keys, LANES), BF16), pltpu.VMEM((n_keys, LANES), BF16)],
        compiler_params=_cparams(("parallel", "parallel", "arbitrary")),
        name="diff_attn",
    )(*args)


CONV_PAD = 8


def _gelu_tanh(t):
    return 0.5 * t * (1.0 + jnp.tanh(math.sqrt(2.0 / math.pi) * (t + 0.044715 * (t * t * t))))


def _softplus(t):
    return jnp.maximum(t, 0.0) + jnp.log1p(jnp.exp(-jnp.abs(t)))


def _rglru_kernel(x_c, g_c, x_x, g_x, cw_ref, cb_ref, wr_ref, br_ref, wi_ref, bi_ref, lam_ref,
                  oc_ref, ox_ref, pad, xc, hs):
    n_tok = CTX_LEN + SEQ
    sc = SCAN_CHUNK
    cw = cw_ref[...]
    cb = cb_ref[...]
    zeros = jnp.zeros((CONV_PAD, LANES), F32)

    for x_ref, base, n in ((x_c, 0, CTX_LEN), (x_x, CTX_LEN, SEQ)):
        pad[0:CONV_PAD, :] = zeros
        pad[CONV_PAD:CONV_PAD + n, :] = x_ref[...]
        pad[CONV_PAD + n:2 * CONV_PAD + n, :] = zeros
        for r in range(n // sc):
            t0 = CONV_PAD + r * sc
            acc = cb + cw[0:1] * pad[t0 - 1:t0 - 1 + sc, :]
            acc += cw[1:2] * pad[t0:t0 + sc, :]
            acc += cw[2:3] * pad[t0 + 1:t0 + 1 + sc, :]
            acc += cw[3:4] * pad[t0 + 2:t0 + 2 + sc, :]
            xc[base + r * sc:base + (r + 1) * sc, :] = acc

    row = lax.broadcasted_iota(jnp.int32, (sc, 1), 0)
    n_chunks = n_tok // sc
    neg_c_softplus = -RG_C * _softplus(-lam_ref[...])
    hs[...] = jnp.zeros_like(hs)

    def scan_chunk(d, r0, carry):
        x = xc[pl.ds(r0, sc), :]
        xb = x.astype(BF16)
        r = _sigmoid(_dot(xb, wr_ref[d]) + br_ref[d:d + 1, :])
        i = _sigmoid(_dot(xb, wi_ref[d]) + bi_ref[d:d + 1, :])
        log_a = neg_c_softplus[d:d + 1, :] * r
        a = jnp.exp(log_a)
        u = jnp.sqrt(-jnp.tanh(log_a) * (a * a + 1.0)) * (i * x)
        s = 1
        while s < sc:
            ok, shift = (row >= s, s) if d == 0 else (row < sc - s, sc - s)
            u_prev = jnp.where(ok, pltpu.roll(u, shift, axis=0), 0.0)
            a_prev = jnp.where(ok, pltpu.roll(a, shift, axis=0), 1.0)
            u = u + a * u_prev
            a = a * a_prev
            s *= 2
        h = u + a * carry
        hs[pl.ds(r0, sc), :] += h
        return h[sc - 1:sc, :] if d == 0 else h[0:1, :]

    def body(t, carries):
        fwd = scan_chunk(0, pl.multiple_of(t * sc, sc), carries[0])
        back = jnp.where(t == 0, 0, n_chunks - t)
        bwd = scan_chunk(1, pl.multiple_of(back * sc, sc), carries[1])
        return fwd, bwd

    zero = jnp.zeros((1, LANES), F32)
    lax.fori_loop(0, n_chunks, body, (zero, zero))

    oc_ref[...] = (_gelu_tanh(g_c[...]) * hs[0:CTX_LEN, :]).astype(oc_ref.dtype)
    for r in range(SEQ // sc):
        rows = slice(r * sc, (r + 1) * sc)
        ox_ref[rows, :] = (_gelu_tanh(g_x[rows, :]) * hs[CTX_LEN + r * sc:CTX_LEN + (r + 1) * sc, :]
                           ).astype(ox_ref.dtype)


def _rglru(z_c, z_x, conv_w, conv_b, w_r, b_r, w_i, b_i, lam, *, batch):
    groups = (G_DX, G_DG)
    c_specs, x_specs = _z_specs(groups, LANES)
    out_specs, out_shapes = _mix_out(batch, LANES)
    vec2 = pl.BlockSpec((2, LANES), lambda b, p: (0, p))
    wspec = pl.BlockSpec((2, LANES, LANES), lambda b, p: (0, p, p))
    return pl.pallas_call(
        _rglru_kernel,
        grid=(batch, SLABS),
        in_specs=c_specs + x_specs + [
            pl.BlockSpec((4, LANES), lambda b, p: (0, p)),
            pl.BlockSpec((1, LANES), lambda b, p: (0, p)),
            wspec, vec2, wspec, vec2, vec2,
        ],
        out_specs=out_specs,
        out_shape=out_shapes,
        scratch_shapes=[pltpu.VMEM((SEQ + 2 * CONV_PAD, LANES), F32),
                        pltpu.VMEM((CTX_LEN + SEQ, LANES), F32),
                        pltpu.VMEM((CTX_LEN + SEQ, LANES), F32)],
        compiler_params=_cparams(("parallel", "parallel")),
        name="rglru",
    )(z_c, z_c, z_x, z_x, conv_w, conv_b, w_r, b_r, w_i, b_i, lam)


def _block_diag_dense(w):
    two, nb, bs, _ = w.shape
    eye = jnp.eye(nb, dtype=w.dtype)
    return jnp.einsum("dgio,gh->dgiho", w, eye).reshape(two, nb * bs, nb * bs)


def kernel(x, c, ctx, c_ctx, w_ada, b_ada, norm_pre, norm_post, ffn_w_in, ffn_w_out, w_in, w_out,
           lb_logits, a_norm, b_norm, c_lambda, c_norm, d_conv_w, d_conv_b, d_w_r, d_b_r, d_w_i,
           d_b_i, d_lambda):
    batch = x.shape[0]
    assert x.shape == (batch, SEQ, D_MODEL) and ctx.shape == (batch, CTX_LEN, D_MODEL)
    assert batch < MOD_ROWS

    gla_consts = _gla_constants()
    ret_consts = _retention_constants()
    rope = _rope_constants()

    c_rows = jnp.zeros((MOD_ROWS, D_MODEL), F32).at[:batch].set(c).at[batch].set(c_ctx)
    mod = _ada_table(c_rows, w_ada, b_ada).reshape(DEPTH * MOD_ROWS, N_MOD, D_MODEL)

    n_c = batch * CTX_LEN
    tm_ffn, tm_proj, tm_out = 1024, 1024, 512
    tm_c = min(n_c, 1024)
    assert n_c % tm_c == 0
    row_x = lambda tm: (lambda i: i // (SEQ // tm))
    row_c = lambda i: batch

    hx = x.reshape(batch * SEQ, D_MODEL)
    hc = ctx.reshape(n_c, D_MODEL)
    tile2 = lambda g: jnp.tile(g.reshape(1, -1), (1, LANES // g.shape[-1]))
    w_out_b = w_out.astype(BF16)

    def ffn(h, l, idx, is_ctx, weights):
        return _ffn(h, mod, norm_pre[l, 2 * idx].reshape(1, D_MODEL), norm_post[l, 2 * idx].reshape(1, D_MODEL),
                    weights, layer=l, idx=idx, k0=6 * idx, emit=is_ctx,
                    row_of_tile=row_c if is_ctx else row_x(tm_ffn), tm=tm_c if is_ctx else tm_ffn)

    for l in range(DEPTH):
        ctx_out = l < DEPTH - 1
        gpre1 = norm_pre[l, 1].reshape(1, D_MODEL)
        gpost1 = norm_post[l, 1].reshape(1, D_MODEL)

        hc, ffn_w = ffn(hc, l, 0, True, (ffn_w_in, ffn_w_out))
        hx = ffn(hx, l, 0, False, ffn_w)

        z_c, w_in_l = _inproj(hc, mod, gpre1, w_in, layer=l, row_of_tile=row_c, tm=tm_c, emit=True)
        z_x = _inproj(hx, mod, gpre1, w_in_l, layer=l, row_of_tile=row_x(tm_proj), tm=tm_proj, emit=False)

        a_c, a_x = _hgrn2(z_c, z_x, lb_logits, tile2(a_norm[l]), gla_consts, layer=l, batch=batch)
        b_c, b_x = _retention(z_c, z_x, tile2(b_norm[l]), rope, ret_consts, batch=batch)

        lam_init = 0.8 - 0.6 * math.exp(-0.3 * l)
        cgain = c_norm[l].reshape(1, LANES)
        c_x = _diff_attn(z_x, z_c, z_x, rope, c_lambda[l], cgain, lam_init=lam_init, batch=batch, tq=256)

        d_c, d_x = _rglru(z_c, z_x, d_conv_w[l], d_conv_b[l].reshape(1, -1),
                          _block_diag_dense(d_w_r[l]).astype(BF16), d_b_r[l],
                          _block_diag_dense(d_w_i[l]).astype(BF16), d_b_i[l], d_lambda[l], batch=batch)

        hx = _outproj((a_x, b_x, c_x, d_x), w_out_b[l], hx, mod, gpost1, layer=l,
                      row_of_tile=row_x(tm_out), tm=tm_out)

        if ctx_out:
            c_c = _diff_attn(z_c, z_c, None, rope, c_lambda[l], cgain, lam_init=lam_init, batch=batch, tq=CTX_LEN)
            hc = _outproj((a_c, b_c, c_c, d_c), w_out_b[l], hc, mod, gpost1, layer=l,
                          row_of_tile=row_c, tm=min(tm_c, tm_out))
            hc, ffn_w = ffn(hc, l, 1, True, (ffn_w_in, ffn_w_out))
        else:
            ffn_w = (ffn_w_in[l, 1, :, :D_FF].astype(BF16), ffn_w_in[l, 1, :, D_FF:].astype(BF16),
                     ffn_w_out[l, 1].astype(BF16))
        hx = ffn(hx, l, 1, False, ffn_w)

    return hx.reshape(batch, SEQ, D_MODEL)
```

```python
import functools
import math

import numpy as np
import jax
import jax.numpy as jnp
from jax import lax
from jax.experimental import pallas as pl
from jax.experimental.pallas import tpu as pltpu

F32 = jnp.float32
BF16 = jnp.bfloat16

D_MODEL = 2048
SEQ = 2048
CTX_LEN = 256
DEPTH = 2
GRID_W = 64
HEAD_DIM = 64
GROUP_WIDTH = 512
IN_WIDTH = 7168
D_FF = 5504
N_MOD = 9
ROPE_BASE = 10000.0
EPS = 1e-6
RG_C = 8.0
FFN_RESIDUAL = 0.5

LANES = 128
FF_TILE = 512
FF_TILE_EMIT = 256
MOD_ROWS = 8
GLA_CHUNK = 128
HGRN2_WIDTH = 512
RET_WIDTH = 512
GLA_LEVELS = 7
SCAN_CHUNK = 256
VMEM_LIMIT = 56 * 1024 * 1024

G_AQ, G_AFF, G_AFB, G_AV, G_AG = 0, 1, 2, 3, 4
G_BQ, G_BK, G_BV, G_BG = 5, 6, 7, 8
G_CQ, G_CK, G_CV = 9, 10, 11
G_DX, G_DG = 12, 13
SLABS = GROUP_WIDTH // LANES


def _cparams(sem):
    return pltpu.CompilerParams(dimension_semantics=sem, vmem_limit_bytes=VMEM_LIMIT)


def _sigmoid(t):
    return jax.nn.sigmoid(t)


def _silu(t):
    return t * _sigmoid(t)


def _rms(t, gain):
    return t * lax.rsqrt(jnp.mean(t * t, axis=-1, keepdims=True) + EPS) * gain


def _dot(a, b):
    return jnp.dot(a, b, preferred_element_type=F32)


def _dot_nt(a, b):
    return lax.dot_general(a, b, (((1,), (1,)), ((), ())), preferred_element_type=F32)


def _ada_kernel(c_ref, w_ref, b_ref, o_ref):
    s = _silu(c_ref[...])
    s_hi = s.astype(BF16).astype(F32)
    s_mid = (s - s_hi).astype(BF16).astype(F32)
    s_lo = s - s_hi - s_mid
    w = w_ref[...]
    w_hi = w.astype(BF16)
    w_lo = (w - w_hi.astype(F32)).astype(BF16)
    r_hi = _dot(jnp.concatenate([s_hi, s_mid, s_lo], axis=0).astype(BF16), w_hi)
    r_lo = _dot(jnp.concatenate([s_hi, s_mid], axis=0).astype(BF16), w_lo)
    m = MOD_ROWS
    o_ref[...] = (r_hi[0:m] + r_hi[m:2 * m] + r_hi[2 * m:3 * m]) + (r_lo[0:m] + r_lo[m:2 * m]) + b_ref[...]


def _ada_table(c_rows, w_ada, b_ada):
    tn = 1024
    n = N_MOD * D_MODEL
    return pl.pallas_call(
        _ada_kernel,
        grid=(DEPTH, n // tn),
        in_specs=[
            pl.BlockSpec((MOD_ROWS, D_MODEL), lambda l, j: (0, 0)),
            pl.BlockSpec((None, D_MODEL, tn), lambda l, j: (l, 0, j)),
            pl.BlockSpec((None, 1, tn), lambda l, j: (l, 0, j)),
        ],
        out_specs=pl.BlockSpec((None, MOD_ROWS, tn), lambda l, j: (l, 0, j)),
        out_shape=jax.ShapeDtypeStruct((DEPTH, MOD_ROWS, n), F32),
        compiler_params=_cparams(("arbitrary", "arbitrary")),
        name="ada_table",
    )(c_rows, w_ada, b_ada.reshape(DEPTH, 1, n))


ROW_CHUNK = 256
MM_ROWS = 512
MM_COLS = 512
OUT_ROWS = 256


def _norm_modulate_to(u_scr, h_ref, gain_ref, mod_ref, k_shift, k_scale):
    tm = h_ref.shape[0]
    shift = mod_ref[k_shift:k_shift + 1, :]
    gain_scale = gain_ref[...] * (1.0 + mod_ref[k_scale:k_scale + 1, :])

    def body(r, carry):
        r0 = pl.multiple_of(r * ROW_CHUNK, ROW_CHUNK)
        h = h_ref[pl.ds(r0, ROW_CHUNK), :]
        u_scr[pl.ds(r0, ROW_CHUNK), :] = (_rms(h, gain_scale) + shift).astype(BF16)
        return carry

    lax.fori_loop(0, tm // ROW_CHUNK, body, 0)


def _cast_ffn_tiles(w_refs, out_refs, j, last):
    tf = w_refs[0].shape[1]
    overlap = pl.cdiv(D_FF, tf) * tf - D_FF
    tiles = [r[...].astype(BF16) for r in w_refs]
    for t, out, axis in zip(tiles, out_refs, (1, 1, 0)):
        if overlap:
            keep = lax.slice_in_dim(t, overlap, tf, axis=axis)
            fill = jnp.zeros_like(lax.slice_in_dim(t, 0, overlap, axis=axis))
            t = jnp.where(j == last, jnp.concatenate([keep, fill], axis=axis), t)
        out[...] = t
    return tiles


def _ffn_kernel(h_ref, mod_ref, gpre_ref, gpost_ref, wg_ref, wu_ref, wout_ref, *rest, k0, emit):
    if emit:
        o_ref, wg_out, wu_out, wout_out, u_scr = rest
    else:
        o_ref, u_scr = rest
    j = pl.program_id(1)
    last = pl.num_programs(1) - 1
    tm = h_ref.shape[0]
    tf = wg_ref.shape[1]
    overlap = pl.cdiv(D_FF, tf) * tf - D_FF

    @pl.when(j == 0)
    def _():
        _norm_modulate_to(u_scr, h_ref, gpre_ref, mod_ref, k0, k0 + 1)
        o_ref[...] = jnp.zeros_like(o_ref)

    if emit:
        wg, wu, wout = _cast_ffn_tiles((wg_ref, wu_ref, wout_ref), (wg_out, wu_out, wout_out), j, last)
    else:
        wg, wu, wout = wg_ref[...], wu_ref[...], wout_ref[...]

    col = lax.broadcasted_iota(jnp.int32, (1, tf), 1)
    fresh = jnp.logical_or(j < last, col >= overlap)
    for r in range(tm // MM_ROWS):
        rows = slice(r * MM_ROWS, (r + 1) * MM_ROWS)
        u = u_scr[rows, :]
        act = jnp.where(fresh, _silu(_dot(u, wg)) * _dot(u, wu), 0.0).astype(BF16)
        for n0 in range(0, D_MODEL, MM_COLS):
            o_ref[rows, n0:n0 + MM_COLS] += _dot(act, wout[:, n0:n0 + MM_COLS])

    @pl.when(j == last)
    def _():
        gate_gain = FFN_RESIDUAL * mod_ref[k0 + 2:k0 + 3, :] * gpost_ref[...]

        def body(r, carry):
            r0 = pl.multiple_of(r * ROW_CHUNK, ROW_CHUNK)
            y = o_ref[pl.ds(r0, ROW_CHUNK), :]
            o_ref[pl.ds(r0, ROW_CHUNK), :] = h_ref[pl.ds(r0, ROW_CHUNK), :] + _rms(y, gate_gain)
            return carry

        lax.fori_loop(0, tm // ROW_CHUNK, body, 0)


def _mod_spec(layer, row_of_tile):
    return pl.BlockSpec((None, N_MOD, D_MODEL), lambda i, *_: (layer * MOD_ROWS + row_of_tile(i), 0, 0))


def _ff_offset(j, tf, base=0):
    return (base // LANES + jnp.minimum(j * (tf // LANES), (D_FF - tf) // LANES)) * LANES


def _ffn_f32_weight_specs(layer, idx, tf):
    el = pl.Element
    in_specs = [
        pl.BlockSpec((None, None, el(D_MODEL), el(tf)), lambda *g: (layer, idx, 0, _ff_offset(g[-1], tf))),
        pl.BlockSpec((None, None, el(D_MODEL), el(tf)), lambda *g: (layer, idx, 0, _ff_offset(g[-1], tf, D_FF))),
        pl.BlockSpec((None, None, el(tf), el(D_MODEL)), lambda *g: (layer, idx, _ff_offset(g[-1], tf), 0)),
    ]
    d_ff_pad = pl.cdiv(D_FF, tf) * tf
    cols_out = pl.BlockSpec((D_MODEL, tf), lambda *g: (0, g[-1]))
    out_specs = [cols_out, cols_out, pl.BlockSpec((tf, D_MODEL), lambda *g: (g[-1], 0))]
    out_shape = [jax.ShapeDtypeStruct((D_MODEL, d_ff_pad), BF16), jax.ShapeDtypeStruct((D_MODEL, d_ff_pad), BF16),
                 jax.ShapeDtypeStruct((d_ff_pad, D_MODEL), BF16)]
    return in_specs, out_specs, out_shape


def _ffn_weight_cast_kernel(wg_ref, wu_ref, wout_ref, wg_out, wu_out, wout_out):
    _cast_ffn_tiles((wg_ref, wu_ref, wout_ref), (wg_out, wu_out, wout_out), pl.program_id(0), pl.num_programs(0) - 1)


def _cast_ffn_weights(w_in, w_out, *, layer, idx):
    tf = FF_TILE
    in_specs, out_specs, out_shape = _ffn_f32_weight_specs(layer, idx, tf)
    return tuple(pl.pallas_call(
        _ffn_weight_cast_kernel,
        grid=(pl.cdiv(D_FF, tf),),
        in_specs=in_specs,
        out_specs=out_specs,
        out_shape=out_shape,
        compiler_params=_cparams(("arbitrary",)),
        name="ffn_weight_cast",
    )(w_in, w_in, w_out))


def _ffn(h, mod, gpre, gpost, weights, *, layer, idx, k0, row_of_tile, tm, emit):
    n = h.shape[0]
    el = pl.Element
    tf = FF_TILE_EMIT if emit else FF_TILE
    col_tile = pl.BlockSpec((el(D_MODEL), el(tf)), lambda i, j: (0, _ff_offset(j, tf)))
    row_tile = pl.BlockSpec((el(tf), el(D_MODEL)), lambda i, j: (_ff_offset(j, tf), 0))
    h_spec = pl.BlockSpec((tm, D_MODEL), lambda i, j: (i, 0))
    out_specs, out_shape = h_spec, jax.ShapeDtypeStruct((n, D_MODEL), F32)
    if emit:
        assert n == tm, "weight tiles are cast once: one row tile only"
        w_in, w_out = weights
        w_args = (w_in, w_in, w_out)
        w_specs, w_out_specs, w_out_shape = _ffn_f32_weight_specs(layer, idx, tf)
        out_specs = [h_spec] + w_out_specs
        out_shape = [out_shape] + w_out_shape
    else:
        w_args = weights
        w_specs = [col_tile, col_tile, row_tile]
    out = pl.pallas_call(
        functools.partial(_ffn_kernel, k0=k0, emit=emit),
        grid=(n // tm, pl.cdiv(D_FF, tf)),
        in_specs=[
            h_spec,
            _mod_spec(layer, row_of_tile),
            pl.BlockSpec((1, D_MODEL), lambda i, j: (0, 0)),
            pl.BlockSpec((1, D_MODEL), lambda i, j: (0, 0)),
        ] + w_specs,
        out_specs=out_specs,
        out_shape=out_shape,
        scratch_shapes=[pltpu.VMEM((tm, D_MODEL), BF16)],
        compiler_params=_cparams(("parallel", "arbitrary")),
        name="ffn_cast" if emit else "ffn",
    )(h, mod, gpre, gpost, *w_args)
    return (out[0], tuple(out[1:])) if emit else out


def _inproj_kernel(h_ref, mod_ref, gpre_ref, w_ref, *rest, emit):
    if emit:
        z_ref, w_out, u_scr = rest
    else:
        z_ref, u_scr = rest
    j = pl.program_id(1)
    tm = h_ref.shape[0]

    @pl.when(j == 0)
    def _():
        _norm_modulate_to(u_scr, h_ref, gpre_ref, mod_ref, 3, 4)

    w = w_ref[...]
    if emit:
        w = w.astype(BF16)
        w_out[...] = w
    for r in range(tm // MM_ROWS):
        rows = slice(r * MM_ROWS, (r + 1) * MM_ROWS)
        z_ref[rows, :] = _dot(u_scr[rows, :], w)


def _inproj(h, mod, gpre, w, *, layer, row_of_tile, tm, emit):
    n = h.shape[0]
    tn = 512 if emit else 1024
    if emit:
        assert n == tm, "weight tiles are cast once: one row tile only"
        w_spec = pl.BlockSpec((None, D_MODEL, tn), lambda i, j: (layer, 0, j))
    else:
        w_spec = pl.BlockSpec((D_MODEL, tn), lambda i, j: (0, j))
    z_spec = pl.BlockSpec((tm, tn), lambda i, j: (i, j))
    z_shape = jax.ShapeDtypeStruct((n, IN_WIDTH), F32)
    out = pl.pallas_call(
        functools.partial(_inproj_kernel, emit=emit),
        grid=(n // tm, IN_WIDTH // tn),
        in_specs=[
            pl.BlockSpec((tm, D_MODEL), lambda i, j: (i, 0)),
            _mod_spec(layer, row_of_tile),
            pl.BlockSpec((1, D_MODEL), lambda i, j: (0, 0)),
            w_spec,
        ],
        out_specs=[z_spec, pl.BlockSpec((D_MODEL, tn), lambda i, j: (0, j))] if emit else z_spec,
        out_shape=[z_shape, jax.ShapeDtypeStruct((D_MODEL, IN_WIDTH), BF16)] if emit else z_shape,
        scratch_shapes=[pltpu.VMEM((tm, D_MODEL), BF16)],
        compiler_params=_cparams(("parallel", "arbitrary")),
        name="inproj_cast" if emit else "inproj",
    )(h, mod, gpre, w)
    return tuple(out) if emit else out


def _outproj_kernel(a_ref, b_ref, c_ref, d_ref, w_ref, h_ref, mod_ref, gpost_ref, o_ref):
    tm = h_ref.shape[0]
    gate_gain = mod_ref[5:6, :] * gpost_ref[...]
    for r in range(tm // OUT_ROWS):
        rows = slice(r * OUT_ROWS, (r + 1) * OUT_ROWS)
        y = _dot(a_ref[rows, :], w_ref[0 * GROUP_WIDTH:1 * GROUP_WIDTH, :])
        y += _dot(b_ref[rows, :], w_ref[1 * GROUP_WIDTH:2 * GROUP_WIDTH, :])
        y += _dot(c_ref[rows, :], w_ref[2 * GROUP_WIDTH:3 * GROUP_WIDTH, :])
        y += _dot(d_ref[rows, :], w_ref[3 * GROUP_WIDTH:4 * GROUP_WIDTH, :])
        o_ref[rows, :] = h_ref[rows, :] + _rms(y, gate_gain)


def _outproj(mix, w, h, mod, gpost, *, layer, row_of_tile, tm):
    n = h.shape[0]
    mix_spec = pl.BlockSpec((tm, GROUP_WIDTH), lambda i: (i, 0))
    return pl.pallas_call(
        _outproj_kernel,
        grid=(n // tm,),
        in_specs=[
            mix_spec, mix_spec, mix_spec, mix_spec,
            pl.BlockSpec((D_MODEL, D_MODEL), lambda i: (0, 0)),
            pl.BlockSpec((tm, D_MODEL), lambda i: (i, 0)),
            _mod_spec(layer, row_of_tile),
            pl.BlockSpec((1, D_MODEL), lambda i: (0, 0)),
        ],
        out_specs=pl.BlockSpec((tm, D_MODEL), lambda i: (i, 0)),
        out_shape=jax.ShapeDtypeStruct((n, D_MODEL), F32),
        compiler_params=_cparams(("parallel",)),
        name="outproj",
    )(*mix, w, h, mod, gpost)


def _gla_constants():
    c = GLA_CHUNK
    idx = np.arange(c)
    i, r = idx[:, None], idx[None, :]
    mats = np.zeros((2, GLA_LEVELS + 2, c, c), np.float32)
    masks = np.zeros((2, GLA_LEVELS + 1, c, c), np.float32)
    mats[0, 0] = r <= i
    mats[0, 1] = r > i
    mats[1, 0] = r >= i
    mats[1, 1] = r < i
    for m in range(GLA_LEVELS):
        hs = c >> (m + 1)
        start = (idx // (2 * hs)) * (2 * hs)
        mid = (start + hs - 1)[:, None]
        right = ((idx // hs) % 2 == 1)[:, None]
        mats[0, 2 + m] = np.where(right, (r > mid) & (r <= i), (r > i) & (r <= mid))
        mats[1, 2 + m] = np.where(right, (r > mid) & (r < i), (r >= i) & (r <= mid))
        same = (start[:, None] == start[None, :])
        q_right = right & ~right.T & same
        masks[0, m] = q_right
        masks[1, m] = q_right.T
    masks[0, GLA_LEVELS] = np.eye(c)
    masks[1, GLA_LEVELS] = np.eye(c)
    mats = mats.reshape(2, (GLA_LEVELS + 2) * c, c)
    mats = np.concatenate([mats, mats], axis=-1)
    masks = np.tile(masks, (1, 1, 1, 2))
    return jnp.asarray(mats, BF16), jnp.asarray(masks, F32)


def _retention_constants():
    c = GLA_CHUNK
    heads = GROUP_WIDTH // HEAD_DIM
    log_decay = np.log1p(-np.exp2(-5.0 - np.arange(heads, dtype=np.float64)))
    ld_lane = np.repeat(log_decay, HEAD_DIM).reshape(SLABS, 1, LANES)
    idx = np.arange(c, dtype=np.float64)
    i, j = idx[:, None], idx[None, :]
    dq = np.zeros((SLABS, 2, c, LANES))
    dk = np.zeros((SLABS, 2, c, LANES))
    dq[:, 0] = np.exp((idx[None, :, None] + 1.0) * ld_lane)
    dk[:, 0] = np.exp((c - 1.0 - idx[None, :, None]) * ld_lane)
    dq[:, 1] = np.exp((c - idx[None, :, None]) * ld_lane)
    dk[:, 1] = np.exp(idx[None, :, None] * ld_lane)
    gt = np.exp(c * ld_lane)
    dm = np.zeros((SLABS, 2, 2, c, c))
    for p in range(SLABS):
        for hh in range(2):
            ld = log_decay[2 * p + hh]
            dm[p, 0, hh] = np.where(i >= j, np.exp((i - j) * ld), 0.0)
            dm[p, 1, hh] = np.where(j >= i, np.exp((j - i) * ld), 0.0)
    dm = dm.transpose(0, 1, 3, 2, 4).reshape(SLABS, 2, c, 2 * c)
    f = lambda t: jnp.asarray(t.astype(np.float32))
    return f(dq), f(dk), f(gt), f(dm)


def _rope_constants():
    quarter = HEAD_DIM // 4
    inv_freq = ROPE_BASE ** (-jnp.arange(quarter, dtype=F32) / quarter)
    rows = SEQ // GRID_W
    row = jnp.repeat(jnp.arange(rows, dtype=F32), GRID_W)
    col = jnp.tile(jnp.arange(GRID_W, dtype=F32), rows)
    ang = jnp.concatenate([row[:, None] * inv_freq, col[:, None] * inv_freq], axis=-1)
    cos, sin = jnp.cos(ang), jnp.sin(ang)
    reps = LANES // HEAD_DIM
    return (jnp.tile(jnp.concatenate([cos, cos], axis=-1), (1, reps)),
            jnp.tile(jnp.concatenate([-sin, sin], axis=-1), (1, reps)))


def _rope(t, cos, sin):
    lane = lax.broadcasted_iota(jnp.int32, t.shape, 1)
    first_half = (lane % HEAD_DIM) < (HEAD_DIM // 2)
    partner = jnp.where(first_half,
                        pltpu.roll(t, LANES - HEAD_DIM // 2, axis=1),
                        pltpu.roll(t, HEAD_DIM // 2, axis=1))
    return t * cos + partner * sin


def _head_lane_masks():
    lane = lax.broadcasted_iota(jnp.int32, (1, LANES), 1)
    return lane < HEAD_DIM, lane >= HEAD_DIM


def _stack_heads(t):
    hm0, hm1 = _head_lane_masks()
    return jnp.concatenate([jnp.where(hm0, t, 0.0).astype(BF16),
                            jnp.where(hm1, t, 0.0).astype(BF16)], axis=0)


def _gla_chunk_update(qe, ke, v, gtot, s_cat, st):
    o = _dot_nt(qe.astype(BF16), st.astype(BF16))
    o += _dot(s_cat.astype(BF16), _stack_heads(v))
    upd = lax.dot_general(v.astype(BF16), ke.astype(BF16), (((0,), (0,)), ((), ())),
                          preferred_element_type=F32)
    row_head = lax.broadcasted_iota(jnp.int32, (LANES, LANES), 0) // HEAD_DIM
    col_head = lax.broadcasted_iota(jnp.int32, (LANES, LANES), 1) // HEAD_DIM
    st = st * gtot + jnp.where(row_head == col_head, upd, 0.0)
    return o, st


def _gla_scan(chunk_fn, acc_c, acc_x):
    n_slabs = acc_c.shape[1] // LANES
    acc_c[...] = jnp.zeros_like(acc_c)
    acc_x[...] = jnp.zeros_like(acc_x)

    def run(seg, acc_ref, states):
        n = acc_ref.shape[0] // GLA_CHUNK

        def body(t, states):
            out = []
            for d in (0, 1):
                r0 = pl.multiple_of((t if d == 0 else n - 1 - t) * GLA_CHUNK, GLA_CHUNK)
                o, st = chunk_fn(d, seg, r0, states[d])
                acc_ref[pl.ds(r0, GLA_CHUNK), :] += o
                out.append(st)
            return tuple(out)

        return lax.fori_loop(0, n, body, states)

    zero = tuple(jnp.zeros((LANES, LANES), F32) for _ in range(n_slabs))
    run(1, acc_x, run(0, acc_c, (zero, zero)))


def _slab(t, s):
    return t[:, s * LANES:(s + 1) * LANES]


def _gla_finalize(acc_ref, zg_ref, gain, o_ref):
    hm0, _ = _head_lane_masks()
    n_slabs = acc_ref.shape[1] // LANES

    def body(c, carry):
        r0 = pl.multiple_of(c * GLA_CHUNK, GLA_CHUNK)
        for s in range(n_slabs):
            cols = slice(s * LANES, (s + 1) * LANES)
            o = acc_ref[pl.ds(r0, GLA_CHUNK), cols]
            sq = o * o
            ms0 = jnp.sum(jnp.where(hm0, sq, 0.0), axis=1, keepdims=True) * (1.0 / HEAD_DIM)
            ms1 = jnp.sum(jnp.where(hm0, 0.0, sq), axis=1, keepdims=True) * (1.0 / HEAD_DIM)
            inv = jnp.where(hm0, lax.rsqrt(ms0 + EPS), lax.rsqrt(ms1 + EPS))
            g = zg_ref[pl.ds(r0, GLA_CHUNK), cols]
            o_ref[pl.ds(r0, GLA_CHUNK), cols] = (o * inv * gain * _silu(g)).astype(o_ref.dtype)
        return carry

    lax.fori_loop(0, acc_ref.shape[0] // GLA_CHUNK, body, 0)


def _hgrn2_kernel(zq_c, zff_c, zfb_c, zv_c, zg_c, zq_x, zff_x, zfb_x, zv_x, zg_x,
                  lbl_ref, gain_ref, mats_ref, masks_ref, oc_ref, ox_ref, acc_c, acc_x, *, layer):
    c = GLA_CHUNK
    n_slabs = acc_c.shape[1] // LANES
    lbl = lbl_ref[...]
    e = jnp.exp(lbl - jnp.max(lbl, axis=0, keepdims=True))
    sm = e / jnp.sum(e, axis=0, keepdims=True)
    lower = jnp.zeros_like(sm[0])
    for l in range(1, layer + 1):
        lower = lower + sm[l]
    zq = (zq_c, zq_x)
    zf = ((zff_c, zff_x), (zfb_c, zfb_x))
    zv = (zv_c, zv_x)
    row = lax.broadcasted_iota(jnp.int32, (c, 1), 0)

    def chunk_fn(d, seg, r0, states):
        lb = lower[d:d + 1, :]
        f = lb + (1.0 - lb) * _sigmoid(zf[d][seg][pl.ds(r0, c), :])
        lf = jnp.log(f)
        k = 1.0 - f
        q = _silu(zq[seg][pl.ds(r0, c), :])
        v = zv[seg][pl.ds(r0, c), :]
        lf_hi = lf.astype(BF16)
        lf_lo = (lf - lf_hi.astype(F32)).astype(BF16)
        ex = jnp.exp(_dot(mats_ref[d], jnp.concatenate([lf_hi, lf_lo], axis=0)))
        qe = q * ex[0:c]
        ke = k * ex[c:2 * c]
        gtot = jnp.exp(jnp.sum(lf, axis=0, keepdims=True))
        s_cats = [jnp.zeros((c, 2 * c), F32) for _ in range(n_slabs)]
        for lvl in range(GLA_LEVELS + 1):
            if lvl < GLA_LEVELS:
                hs = c >> (lvl + 1)
                q_side = ((row // hs) % 2) == (1 if d == 0 else 0)
                rhs_src = jnp.where(q_side, q, k) * ex[(2 + lvl) * c:(3 + lvl) * c]
                lhs = rhs_src.astype(BF16)
            else:
                lhs, rhs_src = q.astype(BF16), k
            mask = masks_ref[d, lvl]
            for s in range(n_slabs):
                s_cats[s] = s_cats[s] + mask * _dot_nt(_slab(lhs, s), _stack_heads(_slab(rhs_src, s)))
        outs = [_gla_chunk_update(_slab(qe, s), _slab(ke, s), _slab(v, s), _slab(gtot, s), s_cats[s],
                                  states[s]) for s in range(n_slabs)]
        return jnp.concatenate([o for o, _ in outs], axis=1), tuple(st for _, st in outs)

    _gla_scan(chunk_fn, acc_c, acc_x)
    gain = gain_ref[...]
    _gla_finalize(acc_c, zg_c, gain, oc_ref)
    _gla_finalize(acc_x, zg_x, gain, ox_ref)


def _z_specs(groups, width, buffers=2):
    per_group = GROUP_WIDTH // width
    mode = dict(pipeline_mode=pl.Buffered(buffers)) if buffers != 2 else {}
    c_specs = [pl.BlockSpec((CTX_LEN, width), lambda b, p, g=g: (b, g * per_group + p), **mode) for g in groups]
    x_specs = [pl.BlockSpec((SEQ, width), lambda b, p, g=g: (b, g * per_group + p), **mode) for g in groups]
    return c_specs, x_specs


def _mix_out(batch, width):
    specs = [pl.BlockSpec((CTX_LEN, width), lambda b, p: (b, p)),
             pl.BlockSpec((SEQ, width), lambda b, p: (b, p))]
    shapes = [jax.ShapeDtypeStruct((batch * CTX_LEN, GROUP_WIDTH), BF16),
              jax.ShapeDtypeStruct((batch * SEQ, GROUP_WIDTH), BF16)]
    return specs, shapes


def _gla_scratch(width):
    return [pltpu.VMEM((CTX_LEN, width), F32), pltpu.VMEM((SEQ, width), F32)]


def _hgrn2(z_c, z_x, lb_logits, gain, consts, *, layer, batch):
    mats, masks = consts
    groups = (G_AQ, G_AFF, G_AFB, G_AV, G_AG)
    width = HGRN2_WIDTH
    c_specs, x_specs = _z_specs(groups, width, buffers=1)
    out_specs, out_shapes = _mix_out(batch, width)
    n_c = len(groups)
    return pl.pallas_call(
        functools.partial(_hgrn2_kernel, layer=layer),
        grid=(batch, GROUP_WIDTH // width),
        in_specs=c_specs + x_specs + [
            pl.BlockSpec((DEPTH, 2, width), lambda b, p: (0, 0, p)),
            pl.BlockSpec((1, LANES), lambda b, p: (0, 0)),
            pl.BlockSpec(mats.shape, lambda b, p: (0, 0, 0)),
            pl.BlockSpec(masks.shape, lambda b, p: (0, 0, 0, 0)),
        ],
        out_specs=out_specs,
        out_shape=out_shapes,
        scratch_shapes=_gla_scratch(width),
        compiler_params=_cparams(("parallel", "parallel")),
        name="hgrn2",
    )(*([z_c] * n_c), *([z_x] * n_c), lb_logits, gain, mats, masks)


def _retention_kernel(zq_c, zk_c, zv_c, zg_c, zq_x, zk_x, zv_x, zg_x, cos_ref, sin_ref,
                      dq_ref, dk_ref, gt_ref, dm_ref, gain_ref, oc_ref, ox_ref, acc_c, acc_x):
    c = GLA_CHUNK
    n_slabs = acc_c.shape[1] // LANES
    zq, zk, zv = (zq_c, zq_x), (zk_c, zk_x), (zv_c, zv_x)

    def chunk_fn(d, seg, r0, states):
        outs = []
        for s in range(n_slabs):
            cols = slice(s * LANES, (s + 1) * LANES)
            q = zq[seg][pl.ds(r0, c), cols]
            k = zk[seg][pl.ds(r0, c), cols] * (HEAD_DIM ** -0.5)
            v = zv[seg][pl.ds(r0, c), cols]
            if seg == 1:
                cos = cos_ref[pl.ds(r0, c), :]
                sin = sin_ref[pl.ds(r0, c), :]
                q, k = _rope(q, cos, sin), _rope(k, cos, sin)
            s_cat = dm_ref[s, d] * _dot_nt(q.astype(BF16), _stack_heads(k))
            outs.append(_gla_chunk_update(q * dq_ref[s, d], k * dk_ref[s, d], v, gt_ref[s], s_cat, states[s]))
        return jnp.concatenate([o for o, _ in outs], axis=1), tuple(st for _, st in outs)

    _gla_scan(chunk_fn, acc_c, acc_x)
    gain = gain_ref[...]
    _gla_finalize(acc_c, zg_c, gain, oc_ref)
    _gla_finalize(acc_x, zg_x, gain, ox_ref)


def _retention(z_c, z_x, gain, rope, consts, *, batch):
    dq, dk, gt, dm = consts
    cos, sin = rope
    groups = (G_BQ, G_BK, G_BV, G_BG)
    width = RET_WIDTH
    c_specs, x_specs = _z_specs(groups, width, buffers=1)
    out_specs, out_shapes = _mix_out(batch, width)
    n_c = len(groups)
    c = GLA_CHUNK
    ns = width // LANES
    return pl.pallas_call(
        _retention_kernel,
        grid=(batch, GROUP_WIDTH // width),
        in_specs=c_specs + x_specs + [
            pl.BlockSpec((SEQ, LANES), lambda b, p: (0, 0)),
            pl.BlockSpec((SEQ, LANES), lambda b, p: (0, 0)),
            pl.BlockSpec((ns, 2, c, LANES), lambda b, p: (p, 0, 0, 0)),
            pl.BlockSpec((ns, 2, c, LANES), lambda b, p: (p, 0, 0, 0)),
            pl.BlockSpec((ns, 1, LANES), lambda b, p: (p, 0, 0)),
            pl.BlockSpec((ns, 2, c, 2 * c), lambda b, p: (p, 0, 0, 0)),
            pl.BlockSpec((1, LANES), lambda b, p: (0, 0)),
        ],
        out_specs=out_specs,
        out_shape=out_shapes,
        scratch_shapes=_gla_scratch(width),
        compiler_params=_cparams(("parallel", "parallel")),
        name="retention",
    )(*([z_c] * n_c), *([z_x] * n_c), cos, sin, dq, dk, gt, dm, gain)


ATTN_FILL_ROWS = 256
ATTN_WIDTH = 512
LOG2_E = 1.4426950408889634


def _diff_attn_kernel(*refs, lam_init, with_latent):
    if with_latent:
        q_ref, kc_ref, vc_ref, kx_ref, vx_ref, cos_ref, sin_ref, lam_ref, gain_ref, o_ref, kk, vv = refs
    else:
        q_ref, kc_ref, vc_ref, lam_ref, gain_ref, o_ref, kk, vv = refs
    tq, width = q_ref.shape
    n_heads = width // LANES

    @pl.when(pl.program_id(2) == 0)
    def _():
        kk[0:CTX_LEN, :] = kc_ref[...].astype(BF16)
        vv[0:CTX_LEN, :] = vc_ref[...].astype(BF16)
        if with_latent:
            for r0 in range(0, SEQ, ATTN_FILL_ROWS):
                rows = slice(r0, r0 + ATTN_FILL_ROWS)
                dst = slice(CTX_LEN + r0, CTX_LEN + r0 + ATTN_FILL_ROWS)
                vv[dst, :] = vx_ref[rows, :].astype(BF16)
                for h in range(n_heads):
                    cols = slice(h * LANES, (h + 1) * LANES)
                    kk[dst, cols] = _rope(kx_ref[rows, cols], cos_ref[rows, :], sin_ref[rows, :]).astype(BF16)

    lv = lam_ref[...]
    lam = (jnp.exp(jnp.sum(lv[0:1] * lv[1:2], axis=1, keepdims=True))
           - jnp.exp(jnp.sum(lv[2:3] * lv[3:4], axis=1, keepdims=True)) + lam_init)
    lane = lax.broadcasted_iota(jnp.int32, (1, LANES), 1)
    gain = gain_ref[...]
    if with_latent:
        q0 = pl.multiple_of(pl.program_id(2) * tq, tq)
        cos, sin = cos_ref[pl.ds(q0, tq), :], sin_ref[pl.ds(q0, tq), :]

    for h in range(n_heads):
        cols = slice(h * LANES, (h + 1) * LANES)
        q = q_ref[:, cols]
        if with_latent:
            q = _rope(q, cos, sin)
        q = (q * (HEAD_DIM ** -0.5 * LOG2_E)).astype(BF16)
        keys = kk[:, cols]

        def weights(qh):
            s = _dot_nt(qh, keys)
            p = jnp.exp2(s - jnp.max(s, axis=-1, keepdims=True))
            return p, jnp.sum(p, axis=-1, keepdims=True)

        zero = jnp.zeros_like(q)
        p1, l1 = weights(jnp.where(lane < HEAD_DIM, q, zero))
        p2, l2 = weights(jnp.where(lane < HEAD_DIM, zero, q))
        a = p1 - (lam * l1 / l2) * p2
        o = _dot(a.astype(BF16), vv[:, cols]) / l1
        o_ref[:, cols] = (_rms(o, gain) * (1.0 - lam_init)).astype(o_ref.dtype)


def _diff_attn(z_q, z_c, z_x, rope, lam_vecs, gain, *, lam_init, batch, tq):
    with_latent = z_x is not None
    q_len = z_q.shape[0] // batch
    nq = q_len // tq
    width = ATTN_WIDTH
    per_group = GROUP_WIDTH // width
    n_keys = CTX_LEN + (SEQ if with_latent else 0)
    block = lambda rows, g: pl.BlockSpec((rows, width), lambda b, h, i, g=g: (b, g * per_group + h))
    qspec = pl.BlockSpec((tq, width), lambda b, h, i: (b * nq + i, G_CQ * per_group + h))
    table = pl.BlockSpec((SEQ, LANES), lambda b, h, i: (0, 0))
    in_specs = [qspec, block(CTX_LEN, G_CK), block(CTX_LEN, G_CV)]
    args = [z_q, z_c, z_c]
    if with_latent:
        in_specs += [block(SEQ, G_CK), block(SEQ, G_CV), table, table]
        args += [z_x, z_x, *rope]
    in_specs += [pl.BlockSpec((4, HEAD_DIM), lambda b, h, i: (0, 0)),
                 pl.BlockSpec((1, LANES), lambda b, h, i: (0, 0))]
    args += [lam_vecs, gain]
    return pl.pallas_call(
        functools.partial(_diff_attn_kernel, lam_init=lam_init, with_latent=with_latent),
        grid=(batch, per_group, nq),
        in_specs=in_specs,
        out_specs=pl.BlockSpec((tq, width), lambda b, h, i: (b * nq + i, h)),
        out_shape=jax.ShapeDtypeStruct((z_q.shape[0], GROUP_WIDTH), BF16),
        scratch_shapes=[pltpu.VMEM((n_keys, width), BF16), pltpu.VMEM((n_keys, width), BF16)],
        compiler_params=_cparams(("parallel", "parallel", "arbitrary")),
        name="diff_attn",
    )(*args)


CONV_PAD = 8


def _gelu_tanh(t):
    return 0.5 * t * (1.0 + jnp.tanh(math.sqrt(2.0 / math.pi) * (t + 0.044715 * (t * t * t))))


def _softplus(t):
    return jnp.maximum(t, 0.0) + jnp.log1p(jnp.exp(-jnp.abs(t)))


def _rglru_kernel(x_c, g_c, x_x, g_x, cw_ref, cb_ref, wr_ref, br_ref, wi_ref, bi_ref, lam_ref,
                  oc_ref, ox_ref, pad, xc, hs):
    n_tok = CTX_LEN + SEQ
    sc = SCAN_CHUNK
    cw = cw_ref[...]
    cb = cb_ref[...]
    zeros = jnp.zeros((CONV_PAD, LANES), F32)

    for x_ref, base, n in ((x_c, 0, CTX_LEN), (x_x, CTX_LEN, SEQ)):
        pad[0:CONV_PAD, :] = zeros
        pad[CONV_PAD:CONV_PAD + n, :] = x_ref[...]
        pad[CONV_PAD + n:2 * CONV_PAD + n, :] = zeros
        for r in range(n // sc):
            t0 = CONV_PAD + r * sc
            acc = cb + cw[0:1] * pad[t0 - 1:t0 - 1 + sc, :]
            acc += cw[1:2] * pad[t0:t0 + sc, :]
            acc += cw[2:3] * pad[t0 + 1:t0 + 1 + sc, :]
            acc += cw[3:4] * pad[t0 + 2:t0 + 2 + sc, :]
            xc[base + r * sc:base + (r + 1) * sc, :] = acc

    row = lax.broadcasted_iota(jnp.int32, (sc, 1), 0)
    n_chunks = n_tok // sc
    neg_c_softplus = -RG_C * _softplus(-lam_ref[...])
    hs[...] = jnp.zeros_like(hs)

    def scan_chunk(d, r0, carry):
        x = xc[pl.ds(r0, sc), :]
        xb = x.astype(BF16)
        r = _sigmoid(_dot(xb, wr_ref[d]) + br_ref[d:d + 1, :])
        i = _sigmoid(_dot(xb, wi_ref[d]) + bi_ref[d:d + 1, :])
        log_a = neg_c_softplus[d:d + 1, :] * r
        a = jnp.exp(log_a)
        u = jnp.sqrt(-jnp.tanh(log_a) * (a * a + 1.0)) * (i * x)
        s = 1
        while s < sc:
            ok, shift = (row >= s, s) if d == 0 else (row < sc - s, sc - s)
            u_prev = jnp.where(ok, pltpu.roll(u, shift, axis=0), 0.0)
            a_prev = jnp.where(ok, pltpu.roll(a, shift, axis=0), 1.0)
            u = u + a * u_prev
            a = a * a_prev
            s *= 2
        h = u + a * carry
        hs[pl.ds(r0, sc), :] += h
        return h[sc - 1:sc, :] if d == 0 else h[0:1, :]

    def body(t, carries):
        fwd = scan_chunk(0, pl.multiple_of(t * sc, sc), carries[0])
        back = jnp.where(t == 0, 0, n_chunks - t)
        bwd = scan_chunk(1, pl.multiple_of(back * sc, sc), carries[1])
        return fwd, bwd

    zero = jnp.zeros((1, LANES), F32)
    lax.fori_loop(0, n_chunks, body, (zero, zero))

    oc_ref[...] = (_gelu_tanh(g_c[...]) * hs[0:CTX_LEN, :]).astype(oc_ref.dtype)
    for r in range(SEQ // sc):
        rows = slice(r * sc, (r + 1) * sc)
        ox_ref[rows, :] = (_gelu_tanh(g_x[rows, :]) * hs[CTX_LEN + r * sc:CTX_LEN + (r + 1) * sc, :]
                           ).astype(ox_ref.dtype)


def _rglru(z_c, z_x, conv_w, conv_b, w_r, b_r, w_i, b_i, lam, *, batch):
    groups = (G_DX, G_DG)
    c_specs, x_specs = _z_specs(groups, LANES)
    out_specs, out_shapes = _mix_out(batch, LANES)
    vec2 = pl.BlockSpec((2, LANES), lambda b, p: (0, p))
    wspec = pl.BlockSpec((2, LANES, LANES), lambda b, p: (0, p, p))
    return pl.pallas_call(
        _rglru_kernel,
        grid=(batch, SLABS),
        in_specs=c_specs + x_specs + [
            pl.BlockSpec((4, LANES), lambda b, p: (0, p)),
            pl.BlockSpec((1, LANES), lambda b, p: (0, p)),
            wspec, vec2, wspec, vec2, vec2,
        ],
        out_specs=out_specs,
        out_shape=out_shapes,
        scratch_shapes=[pltpu.VMEM((SEQ + 2 * CONV_PAD, LANES), F32),
                        pltpu.VMEM((CTX_LEN + SEQ, LANES), F32),
                        pltpu.VMEM((CTX_LEN + SEQ, LANES), F32)],
        compiler_params=_cparams(("parallel", "parallel")),
        name="rglru",
    )(z_c, z_c, z_x, z_x, conv_w, conv_b, w_r, b_r, w_i, b_i, lam)


def _block_diag_dense(w):
    two, nb, bs, _ = w.shape
    eye = jnp.eye(nb, dtype=w.dtype)
    return jnp.einsum("dgio,gh->dgiho", w, eye).reshape(two, nb * bs, nb * bs)


def kernel(x, c, ctx, c_ctx, w_ada, b_ada, norm_pre, norm_post, ffn_w_in, ffn_w_out, w_in, w_out,
           lb_logits, a_norm, b_norm, c_lambda, c_norm, d_conv_w, d_conv_b, d_w_r, d_b_r, d_w_i,
           d_b_i, d_lambda):
    batch = x.shape[0]
    assert x.shape == (batch, SEQ, D_MODEL) and ctx.shape == (batch, CTX_LEN, D_MODEL)
    assert batch < MOD_ROWS

    gla_consts = _gla_constants()
    ret_consts = _retention_constants()
    rope = _rope_constants()

    c_rows = jnp.zeros((MOD_ROWS, D_MODEL), F32).at[:batch].set(c).at[batch].set(c_ctx)
    mod = _ada_table(c_rows, w_ada, b_ada).reshape(DEPTH * MOD_ROWS, N_MOD, D_MODEL)

    n_c = batch * CTX_LEN
    tm_ffn, tm_proj, tm_out = 1024, 1024, 512
    tm_c = min(n_c, 1024)
    assert n_c % tm_c == 0
    row_x = lambda tm: (lambda i: i // (SEQ // tm))
    row_c = lambda i: batch

    hx = x.reshape(batch * SEQ, D_MODEL)
    hc = ctx.reshape(n_c, D_MODEL)
    tile2 = lambda g: jnp.tile(g.reshape(1, -1), (1, LANES // g.shape[-1]))
    w_out_b = w_out.astype(BF16)

    def ffn(h, l, idx, is_ctx, weights):
        return _ffn(h, mod, norm_pre[l, 2 * idx].reshape(1, D_MODEL), norm_post[l, 2 * idx].reshape(1, D_MODEL),
                    weights, layer=l, idx=idx, k0=6 * idx, emit=is_ctx,
                    row_of_tile=row_c if is_ctx else row_x(tm_ffn), tm=tm_c if is_ctx else tm_ffn)

    for l in range(DEPTH):
        ctx_out = l < DEPTH - 1
        gpre1 = norm_pre[l, 1].reshape(1, D_MODEL)
        gpost1 = norm_post[l, 1].reshape(1, D_MODEL)

        hc, ffn_w = ffn(hc, l, 0, True, (ffn_w_in, ffn_w_out))
        hx = ffn(hx, l, 0, False, ffn_w)

        z_c, w_in_l = _inproj(hc, mod, gpre1, w_in, layer=l, row_of_tile=row_c, tm=tm_c, emit=True)
        z_x = _inproj(hx, mod, gpre1, w_in_l, layer=l, row_of_tile=row_x(tm_proj), tm=tm_proj, emit=False)

        a_c, a_x = _hgrn2(z_c, z_x, lb_logits, tile2(a_norm[l]), gla_consts, layer=l, batch=batch)
        b_c, b_x = _retention(z_c, z_x, tile2(b_norm[l]), rope, ret_consts, batch=batch)

        lam_init = 0.8 - 0.6 * math.exp(-0.3 * l)
        cgain = c_norm[l].reshape(1, LANES)
        c_x = _diff_attn(z_x, z_c, z_x, rope, c_lambda[l], cgain, lam_init=lam_init, batch=batch, tq=256)

        d_c, d_x = _rglru(z_c, z_x, d_conv_w[l], d_conv_b[l].reshape(1, -1),
                          _block_diag_dense(d_w_r[l]).astype(BF16), d_b_r[l],
                          _block_diag_dense(d_w_i[l]).astype(BF16), d_b_i[l], d_lambda[l], batch=batch)

        hx = _outproj((a_x, b_x, c_x, d_x), w_out_b[l], hx, mod, gpost1, layer=l,
                      row_of_tile=row_x(tm_out), tm=tm_out)

        if ctx_out:
            c_c = _diff_attn(z_c, z_c, None, rope, c_lambda[l], cgain, lam_init=lam_init, batch=batch, tq=CTX_LEN)
            hc = _outproj((a_c, b_c, c_c, d_c), w_out_b[l], hc, mod, gpost1, layer=l,
                          row_of_tile=row_c, tm=min(tm_c, tm_out))
            hc, ffn_w = ffn(hc, l, 1, True, (ffn_w_in, ffn_w_out))
        else:
            ffn_w = _cast_ffn_weights(ffn_w_in, ffn_w_out, layer=l, idx=1)
        hx = ffn(hx, l, 1, False, ffn_w)

    return hx.reshape(batch, SEQ, D_MODEL)
```

```python
import functools
import math

import numpy as np
import jax
import jax.numpy as jnp
from jax import lax
from jax.experimental import pallas as pl
from jax.experimental.pallas import tpu as pltpu

F32 = jnp.float32
BF16 = jnp.bfloat16

D_MODEL = 2048
SEQ = 2048
CTX_LEN = 256
DEPTH = 2
GRID_W = 64
HEAD_DIM = 64
GROUP_WIDTH = 512
IN_WIDTH = 7168
D_FF = 5504
N_MOD = 9
ROPE_BASE = 10000.0
EPS = 1e-6
RG_C = 8.0
FFN_RESIDUAL = 0.5

LANES = 128
FF_TILE = 512
FF_TILE_EMIT = 256
MOD_ROWS = 8
GLA_CHUNK = 128
HGRN2_WIDTH = 512
RET_WIDTH = 512
GLA_LEVELS = 7
SCAN_CHUNK = 256
VMEM_LIMIT = 56 * 1024 * 1024

G_AQ, G_AFF, G_AFB, G_AV, G_AG = 0, 1, 2, 3, 4
G_BQ, G_BK, G_BV, G_BG = 5, 6, 7, 8
G_CQ, G_CK, G_CV = 9, 10, 11
G_DX, G_DG = 12, 13
SLABS = GROUP_WIDTH // LANES


def _cparams(sem):
    return pltpu.CompilerParams(dimension_semantics=sem, vmem_limit_bytes=VMEM_LIMIT)


def _sigmoid(t):
    return jax.nn.sigmoid(t)


def _silu(t):
    return t * _sigmoid(t)


def _rms(t, gain):
    return t * lax.rsqrt(jnp.mean(t * t, axis=-1, keepdims=True) + EPS) * gain


def _dot(a, b):
    return jnp.dot(a, b, preferred_element_type=F32)


def _dot_nt(a, b):
    return lax.dot_general(a, b, (((1,), (1,)), ((), ())), preferred_element_type=F32)


def _ada_kernel(c_ref, w_ref, b_ref, o_ref):
    s = _silu(c_ref[...])
    s_hi = s.astype(BF16).astype(F32)
    s_mid = (s - s_hi).astype(BF16).astype(F32)
    s_lo = s - s_hi - s_mid
    w = w_ref[...]
    w_hi = w.astype(BF16)
    w_lo = (w - w_hi.astype(F32)).astype(BF16)
    r_hi = _dot(jnp.concatenate([s_hi, s_mid, s_lo], axis=0).astype(BF16), w_hi)
    r_lo = _dot(jnp.concatenate([s_hi, s_mid], axis=0).astype(BF16), w_lo)
    m = MOD_ROWS
    o_ref[...] = (r_hi[0:m] + r_hi[m:2 * m] + r_hi[2 * m:3 * m]) + (r_lo[0:m] + r_lo[m:2 * m]) + b_ref[...]


def _ada_table(c_rows, w_ada, b_ada):
    tn = 1024
    n = N_MOD * D_MODEL
    return pl.pallas_call(
        _ada_kernel,
        grid=(DEPTH, n // tn),
        in_specs=[
            pl.BlockSpec((MOD_ROWS, D_MODEL), lambda l, j: (0, 0)),
            pl.BlockSpec((None, D_MODEL, tn), lambda l, j: (l, 0, j)),
            pl.BlockSpec((None, 1, tn), lambda l, j: (l, 0, j)),
        ],
        out_specs=pl.BlockSpec((None, MOD_ROWS, tn), lambda l, j: (l, 0, j)),
        out_shape=jax.ShapeDtypeStruct((DEPTH, MOD_ROWS, n), F32),
        compiler_params=_cparams(("arbitrary", "arbitrary")),
        name="ada_table",
    )(c_rows, w_ada, b_ada.reshape(DEPTH, 1, n))


ROW_CHUNK = 256
MM_ROWS = 512
MM_COLS = 512
OUT_ROWS = 256


def _norm_modulate_to(u_scr, h_ref, gain_ref, mod_ref, k_shift, k_scale):
    tm = h_ref.shape[0]
    shift = mod_ref[k_shift:k_shift + 1, :]
    gain_scale = gain_ref[...] * (1.0 + mod_ref[k_scale:k_scale + 1, :])

    def body(r, carry):
        r0 = pl.multiple_of(r * ROW_CHUNK, ROW_CHUNK)
        h = h_ref[pl.ds(r0, ROW_CHUNK), :]
        u_scr[pl.ds(r0, ROW_CHUNK), :] = (_rms(h, gain_scale) + shift).astype(BF16)
        return carry

    lax.fori_loop(0, tm // ROW_CHUNK, body, 0)


def _cast_ffn_tiles(w_refs, out_refs, j, last):
    tf = w_refs[0].shape[1]
    overlap = pl.cdiv(D_FF, tf) * tf - D_FF
    tiles = [r[...].astype(BF16) for r in w_refs]
    for t, out, axis in zip(tiles, out_refs, (1, 1, 0)):
        if overlap:
            keep = lax.slice_in_dim(t, overlap, tf, axis=axis)
            fill = jnp.zeros_like(lax.slice_in_dim(t, 0, overlap, axis=axis))
            t = jnp.where(j == last, jnp.concatenate([keep, fill], axis=axis), t)
        out[...] = t
    return tiles


def _ffn_kernel(h_ref, mod_ref, gpre_ref, gpost_ref, wg_ref, wu_ref, wout_ref, *rest, k0, emit):
    if emit:
        o_ref, wg_out, wu_out, wout_out, u_scr = rest
    else:
        o_ref, u_scr = rest
    j = pl.program_id(1)
    last = pl.num_programs(1) - 1
    tm = h_ref.shape[0]
    tf = wg_ref.shape[1]
    overlap = pl.cdiv(D_FF, tf) * tf - D_FF

    @pl.when(j == 0)
    def _():
        _norm_modulate_to(u_scr, h_ref, gpre_ref, mod_ref, k0, k0 + 1)
        o_ref[...] = jnp.zeros_like(o_ref)

    if emit:
        wg, wu, wout = _cast_ffn_tiles((wg_ref, wu_ref, wout_ref), (wg_out, wu_out, wout_out), j, last)
    else:
        wg, wu, wout = wg_ref[...], wu_ref[...], wout_ref[...]

    col = lax.broadcasted_iota(jnp.int32, (1, tf), 1)
    fresh = jnp.logical_or(j < last, col >= overlap)
    for r in range(tm // MM_ROWS):
        rows = slice(r * MM_ROWS, (r + 1) * MM_ROWS)
        u = u_scr[rows, :]
        act = jnp.where(fresh, _silu(_dot(u, wg)) * _dot(u, wu), 0.0).astype(BF16)
        for n0 in range(0, D_MODEL, MM_COLS):
            o_ref[rows, n0:n0 + MM_COLS] += _dot(act, wout[:, n0:n0 + MM_COLS])

    @pl.when(j == last)
    def _():
        gate_gain = FFN_RESIDUAL * mod_ref[k0 + 2:k0 + 3, :] * gpost_ref[...]

        def body(r, carry):
            r0 = pl.multiple_of(r * ROW_CHUNK, ROW_CHUNK)
            y = o_ref[pl.ds(r0, ROW_CHUNK), :]
            o_ref[pl.ds(r0, ROW_CHUNK), :] = h_ref[pl.ds(r0, ROW_CHUNK), :] + _rms(y, gate_gain)
            return carry

        lax.fori_loop(0, tm // ROW_CHUNK, body, 0)


def _mod_spec(layer, row_of_tile):
    return pl.BlockSpec((None, N_MOD, D_MODEL), lambda i, *_: (layer * MOD_ROWS + row_of_tile(i), 0, 0))


def _ff_offset(j, tf, base=0):
    return (base // LANES + jnp.minimum(j * (tf // LANES), (D_FF - tf) // LANES)) * LANES


def _ffn_f32_weight_specs(layer, idx, tf):
    el = pl.Element
    in_specs = [
        pl.BlockSpec((None, None, el(D_MODEL), el(tf)), lambda *g: (layer, idx, 0, _ff_offset(g[-1], tf))),
        pl.BlockSpec((None, None, el(D_MODEL), el(tf)), lambda *g: (layer, idx, 0, _ff_offset(g[-1], tf, D_FF))),
        pl.BlockSpec((None, None, el(tf), el(D_MODEL)), lambda *g: (layer, idx, _ff_offset(g[-1], tf), 0)),
    ]
    d_ff_pad = pl.cdiv(D_FF, tf) * tf
    cols_out = pl.BlockSpec((D_MODEL, tf), lambda *g: (0, g[-1]))
    out_specs = [cols_out, cols_out, pl.BlockSpec((tf, D_MODEL), lambda *g: (g[-1], 0))]
    out_shape = [jax.ShapeDtypeStruct((D_MODEL, d_ff_pad), BF16), jax.ShapeDtypeStruct((D_MODEL, d_ff_pad), BF16),
                 jax.ShapeDtypeStruct((d_ff_pad, D_MODEL), BF16)]
    return in_specs, out_specs, out_shape


def _ffn_weight_cast_kernel(wg_ref, wu_ref, wout_ref, wg_out, wu_out, wout_out):
    _cast_ffn_tiles((wg_ref, wu_ref, wout_ref), (wg_out, wu_out, wout_out), pl.program_id(0), pl.num_programs(0) - 1)


def _cast_ffn_weights(w_in, w_out, *, layer, idx):
    tf = FF_TILE
    in_specs, out_specs, out_shape = _ffn_f32_weight_specs(layer, idx, tf)
    return tuple(pl.pallas_call(
        _ffn_weight_cast_kernel,
        grid=(pl.cdiv(D_FF, tf),),
        in_specs=in_specs,
        out_specs=out_specs,
        out_shape=out_shape,
        compiler_params=_cparams(("arbitrary",)),
        name="ffn_weight_cast",
    )(w_in, w_in, w_out))


def _ffn(h, mod, gpre, gpost, weights, *, layer, idx, k0, row_of_tile, tm, emit):
    n = h.shape[0]
    el = pl.Element
    tf = FF_TILE_EMIT if emit else FF_TILE
    col_tile = pl.BlockSpec((el(D_MODEL), el(tf)), lambda i, j: (0, _ff_offset(j, tf)))
    row_tile = pl.BlockSpec((el(tf), el(D_MODEL)), lambda i, j: (_ff_offset(j, tf), 0))
    h_spec = pl.BlockSpec((tm, D_MODEL), lambda i, j: (i, 0))
    out_specs, out_shape = h_spec, jax.ShapeDtypeStruct((n, D_MODEL), F32)
    if emit:
        assert n == tm, "weight tiles are cast once: one row tile only"
        w_in, w_out = weights
        w_args = (w_in, w_in, w_out)
        w_specs, w_out_specs, w_out_shape = _ffn_f32_weight_specs(layer, idx, tf)
        out_specs = [h_spec] + w_out_specs
        out_shape = [out_shape] + w_out_shape
    else:
        w_args = weights
        w_specs = [col_tile, col_tile, row_tile]
    out = pl.pallas_call(
        functools.partial(_ffn_kernel, k0=k0, emit=emit),
        grid=(n // tm, pl.cdiv(D_FF, tf)),
        in_specs=[
            h_spec,
            _mod_spec(layer, row_of_tile),
            pl.BlockSpec((1, D_MODEL), lambda i, j: (0, 0)),
            pl.BlockSpec((1, D_MODEL), lambda i, j: (0, 0)),
        ] + w_specs,
        out_specs=out_specs,
        out_shape=out_shape,
        scratch_shapes=[pltpu.VMEM((tm, D_MODEL), BF16)],
        compiler_params=_cparams(("parallel", "arbitrary")),
        name="ffn_cast" if emit else "ffn",
    )(h, mod, gpre, gpost, *w_args)
    return (out[0], tuple(out[1:])) if emit else out


def _inproj_kernel(h_ref, mod_ref, gpre_ref, w_ref, *rest, emit):
    if emit:
        z_ref, w_out, u_scr = rest
    else:
        z_ref, u_scr = rest
    j = pl.program_id(1)
    tm = h_ref.shape[0]

    @pl.when(j == 0)
    def _():
        _norm_modulate_to(u_scr, h_ref, gpre_ref, mod_ref, 3, 4)

    w = w_ref[...]
    if emit:
        w = w.astype(BF16)
        w_out[...] = w
    for r in range(tm // MM_ROWS):
        rows = slice(r * MM_ROWS, (r + 1) * MM_ROWS)
        z_ref[rows, :] = _dot(u_scr[rows, :], w)


def _inproj(h, mod, gpre, w, *, layer, row_of_tile, tm, emit):
    n = h.shape[0]
    tn = 512 if emit else 1024
    if emit:
        assert n == tm, "weight tiles are cast once: one row tile only"
        w_spec = pl.BlockSpec((None, D_MODEL, tn), lambda i, j: (layer, 0, j))
    else:
        w_spec = pl.BlockSpec((D_MODEL, tn), lambda i, j: (0, j))
    z_spec = pl.BlockSpec((tm, tn), lambda i, j: (i, j))
    z_shape = jax.ShapeDtypeStruct((n, IN_WIDTH), F32)
    out = pl.pallas_call(
        functools.partial(_inproj_kernel, emit=emit),
        grid=(n // tm, IN_WIDTH // tn),
        in_specs=[
            pl.BlockSpec((tm, D_MODEL), lambda i, j: (i, 0)),
            _mod_spec(layer, row_of_tile),
            pl.BlockSpec((1, D_MODEL), lambda i, j: (0, 0)),
            w_spec,
        ],
        out_specs=[z_spec, pl.BlockSpec((D_MODEL, tn), lambda i, j: (0, j))] if emit else z_spec,
        out_shape=[z_shape, jax.ShapeDtypeStruct((D_MODEL, IN_WIDTH), BF16)] if emit else z_shape,
        scratch_shapes=[pltpu.VMEM((tm, D_MODEL), BF16)],
        compiler_params=_cparams(("parallel", "arbitrary")),
        name="inproj_cast" if emit else "inproj",
    )(h, mod, gpre, w)
    return tuple(out) if emit else out


def _outproj_kernel(a_ref, b_ref, c_ref, d_ref, w_ref, h_ref, mod_ref, gpost_ref, o_ref):
    tm = h_ref.shape[0]
    gate_gain = mod_ref[5:6, :] * gpost_ref[...]
    for r in range(tm // OUT_ROWS):
        rows = slice(r * OUT_ROWS, (r + 1) * OUT_ROWS)
        y = _dot(a_ref[rows, :], w_ref[0 * GROUP_WIDTH:1 * GROUP_WIDTH, :])
        y += _dot(b_ref[rows, :], w_ref[1 * GROUP_WIDTH:2 * GROUP_WIDTH, :])
        y += _dot(c_ref[rows, :], w_ref[2 * GROUP_WIDTH:3 * GROUP_WIDTH, :])
        y += _dot(d_ref[rows, :], w_ref[3 * GROUP_WIDTH:4 * GROUP_WIDTH, :])
        o_ref[rows, :] = h_ref[rows, :] + _rms(y, gate_gain)


def _outproj(mix, w, h, mod, gpost, *, layer, row_of_tile, tm):
    n = h.shape[0]
    mix_spec = pl.BlockSpec((tm, GROUP_WIDTH), lambda i: (i, 0))
    return pl.pallas_call(
        _outproj_kernel,
        grid=(n // tm,),
        in_specs=[
            mix_spec, mix_spec, mix_spec, mix_spec,
            pl.BlockSpec((D_MODEL, D_MODEL), lambda i: (0, 0)),
            pl.BlockSpec((tm, D_MODEL), lambda i: (i, 0)),
            _mod_spec(layer, row_of_tile),
            pl.BlockSpec((1, D_MODEL), lambda i: (0, 0)),
        ],
        out_specs=pl.BlockSpec((tm, D_MODEL), lambda i: (i, 0)),
        out_shape=jax.ShapeDtypeStruct((n, D_MODEL), F32),
        compiler_params=_cparams(("parallel",)),
        name="outproj",
    )(*mix, w, h, mod, gpost)


def _gla_constants():
    c = GLA_CHUNK
    idx = np.arange(c)
    i, r = idx[:, None], idx[None, :]
    mats = np.zeros((2, GLA_LEVELS + 2, c, c), np.float32)
    masks = np.zeros((2, GLA_LEVELS + 1, c, c), np.float32)
    mats[0, 0] = r <= i
    mats[0, 1] = r > i
    mats[1, 0] = r >= i
    mats[1, 1] = r < i
    for m in range(GLA_LEVELS):
        hs = c >> (m + 1)
        start = (idx // (2 * hs)) * (2 * hs)
        mid = (start + hs - 1)[:, None]
        right = ((idx // hs) % 2 == 1)[:, None]
        mats[0, 2 + m] = np.where(right, (r > mid) & (r <= i), (r > i) & (r <= mid))
        mats[1, 2 + m] = np.where(right, (r > mid) & (r < i), (r >= i) & (r <= mid))
        same = (start[:, None] == start[None, :])
        q_right = right & ~right.T & same
        masks[0, m] = q_right
        masks[1, m] = q_right.T
    masks[0, GLA_LEVELS] = np.eye(c)
    masks[1, GLA_LEVELS] = np.eye(c)
    mats = mats.reshape(2, (GLA_LEVELS + 2) * c, c)
    mats = np.concatenate([mats, mats], axis=-1)
    masks = np.tile(masks, (1, 1, 1, 2))
    return jnp.asarray(mats, BF16), jnp.asarray(masks, F32)


def _retention_constants():
    c = GLA_CHUNK
    heads = GROUP_WIDTH // HEAD_DIM
    log_decay = np.log1p(-np.exp2(-5.0 - np.arange(heads, dtype=np.float64)))
    ld_lane = np.repeat(log_decay, HEAD_DIM).reshape(SLABS, 1, LANES)
    idx = np.arange(c, dtype=np.float64)
    i, j = idx[:, None], idx[None, :]
    dq = np.zeros((SLABS, 2, c, LANES))
    dk = np.zeros((SLABS, 2, c, LANES))
    dq[:, 0] = np.exp((idx[None, :, None] + 1.0) * ld_lane)
    dk[:, 0] = np.exp((c - 1.0 - idx[None, :, None]) * ld_lane)
    dq[:, 1] = np.exp((c - idx[None, :, None]) * ld_lane)
    dk[:, 1] = np.exp(idx[None, :, None] * ld_lane)
    gt = np.exp(c * ld_lane)
    dm = np.zeros((SLABS, 2, 2, c, c))
    for p in range(SLABS):
        for hh in range(2):
            ld = log_decay[2 * p + hh]
            dm[p, 0, hh] = np.where(i >= j, np.exp((i - j) * ld), 0.0)
            dm[p, 1, hh] = np.where(j >= i, np.exp((j - i) * ld), 0.0)
    dm = dm.transpose(0, 1, 3, 2, 4).reshape(SLABS, 2, c, 2 * c)
    f = lambda t: jnp.asarray(t.astype(np.float32))
    return f(dq), f(dk), f(gt), f(dm)


def _rope_constants():
    quarter = HEAD_DIM // 4
    inv_freq = ROPE_BASE ** (-jnp.arange(quarter, dtype=F32) / quarter)
    rows = SEQ // GRID_W
    row = jnp.repeat(jnp.arange(rows, dtype=F32), GRID_W)
    col = jnp.tile(jnp.arange(GRID_W, dtype=F32), rows)
    ang = jnp.concatenate([row[:, None] * inv_freq, col[:, None] * inv_freq], axis=-1)
    cos, sin = jnp.cos(ang), jnp.sin(ang)
    reps = LANES // HEAD_DIM
    return (jnp.tile(jnp.concatenate([cos, cos], axis=-1), (1, reps)),
            jnp.tile(jnp.concatenate([-sin, sin], axis=-1), (1, reps)))


def _rope(t, cos, sin):
    lane = lax.broadcasted_iota(jnp.int32, t.shape, 1)
    first_half = (lane % HEAD_DIM) < (HEAD_DIM // 2)
    partner = jnp.where(first_half,
                        pltpu.roll(t, LANES - HEAD_DIM // 2, axis=1),
                        pltpu.roll(t, HEAD_DIM // 2, axis=1))
    return t * cos + partner * sin


def _head_lane_masks():
    lane = lax.broadcasted_iota(jnp.int32, (1, LANES), 1)
    return lane < HEAD_DIM, lane >= HEAD_DIM


def _stack_heads(t):
    hm0, hm1 = _head_lane_masks()
    return jnp.concatenate([jnp.where(hm0, t, 0.0).astype(BF16),
                            jnp.where(hm1, t, 0.0).astype(BF16)], axis=0)


def _gla_state_update(ke, v, gtot, st):
    upd = lax.dot_general(v.astype(BF16), ke.astype(BF16), (((0,), (0,)), ((), ())),
                          preferred_element_type=F32)
    row_head = lax.broadcasted_iota(jnp.int32, (LANES, LANES), 0) // HEAD_DIM
    col_head = lax.broadcasted_iota(jnp.int32, (LANES, LANES), 1) // HEAD_DIM
    return st * gtot + jnp.where(row_head == col_head, upd, 0.0)


def _gla_chunk_update(qe, ke, v, gtot, s_cat, st):
    o = _dot_nt(qe.astype(BF16), st.astype(BF16))
    o += _dot(s_cat.astype(BF16), _stack_heads(v))
    return o, _gla_state_update(ke, v, gtot, st)


def _gla_scan(chunk_fn, acc_c, acc_x, ctx_out):
    n_slabs = acc_x.shape[1] // LANES
    if ctx_out:
        acc_c[...] = jnp.zeros_like(acc_c)
    acc_x[...] = jnp.zeros_like(acc_x)

    def run(seg, acc_ref, states):
        n = acc_ref.shape[0] // GLA_CHUNK
        need_out = seg == 1 or ctx_out

        def body(t, states):
            out = []
            for d in (0, 1):
                r0 = pl.multiple_of((t if d == 0 else n - 1 - t) * GLA_CHUNK, GLA_CHUNK)
                o, st = chunk_fn(d, seg, r0, states[d], need_out)
                if need_out:
                    acc_ref[pl.ds(r0, GLA_CHUNK), :] += o
                out.append(st)
            return tuple(out)

        return lax.fori_loop(0, n, body, states)

    zero = tuple(jnp.zeros((LANES, LANES), F32) for _ in range(n_slabs))
    run(1, acc_x, run(0, acc_c, (zero, zero)))


def _gla_outputs(acc_c, acc_x, zg_c, zg_x, gain, oc_ref, ox_ref, ctx_out):
    if ctx_out:
        _gla_finalize(acc_c, zg_c, gain, oc_ref)
    else:
        oc_ref[...] = jnp.zeros_like(oc_ref)
    _gla_finalize(acc_x, zg_x, gain, ox_ref)


def _slab(t, s):
    return t[:, s * LANES:(s + 1) * LANES]


def _gla_finalize(acc_ref, zg_ref, gain, o_ref):
    hm0, _ = _head_lane_masks()
    n_slabs = acc_ref.shape[1] // LANES

    def body(c, carry):
        r0 = pl.multiple_of(c * GLA_CHUNK, GLA_CHUNK)
        for s in range(n_slabs):
            cols = slice(s * LANES, (s + 1) * LANES)
            o = acc_ref[pl.ds(r0, GLA_CHUNK), cols]
            sq = o * o
            ms0 = jnp.sum(jnp.where(hm0, sq, 0.0), axis=1, keepdims=True) * (1.0 / HEAD_DIM)
            ms1 = jnp.sum(jnp.where(hm0, 0.0, sq), axis=1, keepdims=True) * (1.0 / HEAD_DIM)
            inv = jnp.where(hm0, lax.rsqrt(ms0 + EPS), lax.rsqrt(ms1 + EPS))
            g = zg_ref[pl.ds(r0, GLA_CHUNK), cols]
            o_ref[pl.ds(r0, GLA_CHUNK), cols] = (o * inv * gain * _silu(g)).astype(o_ref.dtype)
        return carry

    lax.fori_loop(0, acc_ref.shape[0] // GLA_CHUNK, body, 0)


def _hgrn2_kernel(zq_c, zff_c, zfb_c, zv_c, zg_c, zq_x, zff_x, zfb_x, zv_x, zg_x,
                  lbl_ref, gain_ref, mats_ref, masks_ref, oc_ref, ox_ref, acc_c, acc_x, *, layer, ctx_out):
    c = GLA_CHUNK
    n_slabs = acc_c.shape[1] // LANES
    lbl = lbl_ref[...]
    e = jnp.exp(lbl - jnp.max(lbl, axis=0, keepdims=True))
    sm = e / jnp.sum(e, axis=0, keepdims=True)
    lower = jnp.zeros_like(sm[0])
    for l in range(1, layer + 1):
        lower = lower + sm[l]
    zq = (zq_c, zq_x)
    zf = ((zff_c, zff_x), (zfb_c, zfb_x))
    zv = (zv_c, zv_x)
    row = lax.broadcasted_iota(jnp.int32, (c, 1), 0)

    def chunk_fn(d, seg, r0, states, need_out):
        lb = lower[d:d + 1, :]
        f = lb + (1.0 - lb) * _sigmoid(zf[d][seg][pl.ds(r0, c), :])
        lf = jnp.log(f)
        k = 1.0 - f
        v = zv[seg][pl.ds(r0, c), :]
        lf_hi = lf.astype(BF16)
        lf_lo = (lf - lf_hi.astype(F32)).astype(BF16)
        lf_cat = jnp.concatenate([lf_hi, lf_lo], axis=0)
        gtot = jnp.exp(jnp.sum(lf, axis=0, keepdims=True))
        if not need_out:
            ke = k * jnp.exp(_dot(mats_ref[d, c:2 * c, :], lf_cat))
            return None, tuple(_gla_state_update(_slab(ke, s), _slab(v, s), _slab(gtot, s), states[s])
                               for s in range(n_slabs))
        q = _silu(zq[seg][pl.ds(r0, c), :])
        ex = jnp.exp(_dot(mats_ref[d], lf_cat))
        qe = q * ex[0:c]
        ke = k * ex[c:2 * c]
        s_cats = [jnp.zeros((c, 2 * c), F32) for _ in range(n_slabs)]
        for lvl in range(GLA_LEVELS + 1):
            if lvl < GLA_LEVELS:
                hs = c >> (lvl + 1)
                q_side = ((row // hs) % 2) == (1 if d == 0 else 0)
                rhs_src = jnp.where(q_side, q, k) * ex[(2 + lvl) * c:(3 + lvl) * c]
                lhs = rhs_src.astype(BF16)
            else:
                lhs, rhs_src = q.astype(BF16), k
            mask = masks_ref[d, lvl]
            for s in range(n_slabs):
                s_cats[s] = s_cats[s] + mask * _dot_nt(_slab(lhs, s), _stack_heads(_slab(rhs_src, s)))
        outs = [_gla_chunk_update(_slab(qe, s), _slab(ke, s), _slab(v, s), _slab(gtot, s), s_cats[s],
                                  states[s]) for s in range(n_slabs)]
        return jnp.concatenate([o for o, _ in outs], axis=1), tuple(st for _, st in outs)

    _gla_scan(chunk_fn, acc_c, acc_x, ctx_out)
    _gla_outputs(acc_c, acc_x, zg_c, zg_x, gain_ref[...], oc_ref, ox_ref, ctx_out)


def _z_specs(groups, width, single_buffered=()):
    per_group = GROUP_WIDTH // width
    mode = lambda g: dict(pipeline_mode=pl.Buffered(1)) if g in single_buffered else {}
    c_specs = [pl.BlockSpec((CTX_LEN, width), lambda b, p, g=g: (b, g * per_group + p)) for g in groups]
    x_specs = [pl.BlockSpec((SEQ, width), lambda b, p, g=g: (b, g * per_group + p), **mode(g)) for g in groups]
    return c_specs, x_specs


def _mix_out(batch, width):
    specs = [pl.BlockSpec((CTX_LEN, width), lambda b, p: (b, p)),
             pl.BlockSpec((SEQ, width), lambda b, p: (b, p))]
    shapes = [jax.ShapeDtypeStruct((batch * CTX_LEN, GROUP_WIDTH), BF16),
              jax.ShapeDtypeStruct((batch * SEQ, GROUP_WIDTH), BF16)]
    return specs, shapes


def _gla_scratch(width):
    return [pltpu.VMEM((CTX_LEN, width), F32), pltpu.VMEM((SEQ, width), F32)]


def _hgrn2(z_c, z_x, lb_logits, gain, consts, *, layer, batch, ctx_out):
    mats, masks = consts
    groups = (G_AQ, G_AFF, G_AFB, G_AV, G_AG)
    width = HGRN2_WIDTH
    c_specs, x_specs = _z_specs(groups, width, single_buffered=(G_AV, G_AG))
    out_specs, out_shapes = _mix_out(batch, width)
    n_c = len(groups)
    return pl.pallas_call(
        functools.partial(_hgrn2_kernel, layer=layer, ctx_out=ctx_out),
        grid=(batch, GROUP_WIDTH // width),
        in_specs=c_specs + x_specs + [
            pl.BlockSpec((DEPTH, 2, width), lambda b, p: (0, 0, p)),
            pl.BlockSpec((1, LANES), lambda b, p: (0, 0)),
            pl.BlockSpec(mats.shape, lambda b, p: (0, 0, 0)),
            pl.BlockSpec(masks.shape, lambda b, p: (0, 0, 0, 0)),
        ],
        out_specs=out_specs,
        out_shape=out_shapes,
        scratch_shapes=_gla_scratch(width),
        compiler_params=_cparams(("parallel", "parallel")),
        name="hgrn2",
    )(*([z_c] * n_c), *([z_x] * n_c), lb_logits, gain, mats, masks)


def _retention_kernel(zq_c, zk_c, zv_c, zg_c, zq_x, zk_x, zv_x, zg_x, cos_ref, sin_ref,
                      dq_ref, dk_ref, gt_ref, dm_ref, gain_ref, oc_ref, ox_ref, acc_c, acc_x, *, ctx_out):
    c = GLA_CHUNK
    n_slabs = acc_c.shape[1] // LANES
    zq, zk, zv = (zq_c, zq_x), (zk_c, zk_x), (zv_c, zv_x)

    def chunk_fn(d, seg, r0, states, need_out):
        def rotated(t):
            return _rope(t, cos_ref[pl.ds(r0, c), :], sin_ref[pl.ds(r0, c), :]) if seg == 1 else t

        outs = []
        for s in range(n_slabs):
            cols = slice(s * LANES, (s + 1) * LANES)
            k = rotated(zk[seg][pl.ds(r0, c), cols] * (HEAD_DIM ** -0.5))
            v = zv[seg][pl.ds(r0, c), cols]
            if not need_out:
                outs.append((None, _gla_state_update(k * dk_ref[s, d], v, gt_ref[s], states[s])))
                continue
            q = rotated(zq[seg][pl.ds(r0, c), cols])
            s_cat = dm_ref[s, d] * _dot_nt(q.astype(BF16), _stack_heads(k))
            outs.append(_gla_chunk_update(q * dq_ref[s, d], k * dk_ref[s, d], v, gt_ref[s], s_cat, states[s]))
        o = jnp.concatenate([o for o, _ in outs], axis=1) if need_out else None
        return o, tuple(st for _, st in outs)

    _gla_scan(chunk_fn, acc_c, acc_x, ctx_out)
    _gla_outputs(acc_c, acc_x, zg_c, zg_x, gain_ref[...], oc_ref, ox_ref, ctx_out)


def _retention(z_c, z_x, gain, rope, consts, *, batch, ctx_out):
    dq, dk, gt, dm = consts
    cos, sin = rope
    groups = (G_BQ, G_BK, G_BV, G_BG)
    width = RET_WIDTH
    c_specs, x_specs = _z_specs(groups, width)
    out_specs, out_shapes = _mix_out(batch, width)
    n_c = len(groups)
    c = GLA_CHUNK
    ns = width // LANES
    return pl.pallas_call(
        functools.partial(_retention_kernel, ctx_out=ctx_out),
        grid=(batch, GROUP_WIDTH // width),
        in_specs=c_specs + x_specs + [
            pl.BlockSpec((SEQ, LANES), lambda b, p: (0, 0)),
            pl.BlockSpec((SEQ, LANES), lambda b, p: (0, 0)),
            pl.BlockSpec((ns, 2, c, LANES), lambda b, p: (p, 0, 0, 0)),
            pl.BlockSpec((ns, 2, c, LANES), lambda b, p: (p, 0, 0, 0)),
            pl.BlockSpec((ns, 1, LANES), lambda b, p: (p, 0, 0)),
            pl.BlockSpec((ns, 2, c, 2 * c), lambda b, p: (p, 0, 0, 0)),
            pl.BlockSpec((1, LANES), lambda b, p: (0, 0)),
        ],
        out_specs=out_specs,
        out_shape=out_shapes,
        scratch_shapes=_gla_scratch(width),
        compiler_params=_cparams(("parallel", "parallel")),
        name="retention",
    )(*([z_c] * n_c), *([z_x] * n_c), cos, sin, dq, dk, gt, dm, gain)


ATTN_FILL_ROWS = 256
ATTN_WIDTH = 512
LOG2_E = 1.4426950408889634


def _diff_attn_kernel(*refs, lam_init, with_latent):
    if with_latent:
        q_ref, kc_ref, vc_ref, kx_ref, vx_ref, cos_ref, sin_ref, lam_ref, gain_ref, o_ref, kk, vv = refs
    else:
        q_ref, kc_ref, vc_ref, lam_ref, gain_ref, o_ref, kk, vv = refs
    tq, width = q_ref.shape
    n_heads = width // LANES

    @pl.when(pl.program_id(2) == 0)
    def _():
        kk[0:CTX_LEN, :] = kc_ref[...].astype(BF16)
        vv[0:CTX_LEN, :] = vc_ref[...].astype(BF16)
        if with_latent:
            for r0 in range(0, SEQ, ATTN_FILL_ROWS):
                rows = slice(r0, r0 + ATTN_FILL_ROWS)
                dst = slice(CTX_LEN + r0, CTX_LEN + r0 + ATTN_FILL_ROWS)
                vv[dst, :] = vx_ref[rows, :].astype(BF16)
                for h in range(n_heads):
                    cols = slice(h * LANES, (h + 1) * LANES)
                    kk[dst, cols] = _rope(kx_ref[rows, cols], cos_ref[rows, :], sin_ref[rows, :]).astype(BF16)

    lv = lam_ref[...]
    lam = (jnp.exp(jnp.sum(lv[0:1] * lv[1:2], axis=1, keepdims=True))
           - jnp.exp(jnp.sum(lv[2:3] * lv[3:4], axis=1, keepdims=True)) + lam_init)
    lane = lax.broadcasted_iota(jnp.int32, (1, LANES), 1)
    gain = gain_ref[...]
    if with_latent:
        q0 = pl.multiple_of(pl.program_id(2) * tq, tq)
        cos, sin = cos_ref[pl.ds(q0, tq), :], sin_ref[pl.ds(q0, tq), :]

    for h in range(n_heads):
        cols = slice(h * LANES, (h + 1) * LANES)
        q = q_ref[:, cols]
        if with_latent:
            q = _rope(q, cos, sin)
        q = (q * (HEAD_DIM ** -0.5 * LOG2_E)).astype(BF16)
        keys = kk[:, cols]

        def weights(qh):
            s = _dot_nt(qh, keys)
            p = jnp.exp2(s - jnp.max(s, axis=-1, keepdims=True))
            return p, jnp.sum(p, axis=-1, keepdims=True)

        zero = jnp.zeros_like(q)
        p1, l1 = weights(jnp.where(lane < HEAD_DIM, q, zero))
        p2, l2 = weights(jnp.where(lane < HEAD_DIM, zero, q))
        a = p1 - (lam * l1 / l2) * p2
        o = _dot(a.astype(BF16), vv[:, cols]) / l1
        o_ref[:, cols] = (_rms(o, gain) * (1.0 - lam_init)).astype(o_ref.dtype)


def _diff_attn(z_q, z_c, z_x, rope, lam_vecs, gain, *, lam_init, batch, tq):
    with_latent = z_x is not None
    q_len = z_q.shape[0] // batch
    nq = q_len // tq
    width = ATTN_WIDTH
    per_group = GROUP_WIDTH // width
    n_keys = CTX_LEN + (SEQ if with_latent else 0)
    block = lambda rows, g: pl.BlockSpec((rows, width), lambda b, h, i, g=g: (b, g * per_group + h))
    qspec = pl.BlockSpec((tq, width), lambda b, h, i: (b * nq + i, G_CQ * per_group + h))
    table = pl.BlockSpec((SEQ, LANES), lambda b, h, i: (0, 0))
    in_specs = [qspec, block(CTX_LEN, G_CK), block(CTX_LEN, G_CV)]
    args = [z_q, z_c, z_c]
    if with_latent:
        in_specs += [block(SEQ, G_CK), block(SEQ, G_CV), table, table]
        args += [z_x, z_x, *rope]
    in_specs += [pl.BlockSpec((4, HEAD_DIM), lambda b, h, i: (0, 0)),
                 pl.BlockSpec((1, LANES), lambda b, h, i: (0, 0))]
    args += [lam_vecs, gain]
    return pl.pallas_call(
        functools.partial(_diff_attn_kernel, lam_init=lam_init, with_latent=with_latent),
        grid=(batch, per_group, nq),
        in_specs=in_specs,
        out_specs=pl.BlockSpec((tq, width), lambda b, h, i: (b * nq + i, h)),
        out_shape=jax.ShapeDtypeStruct((z_q.shape[0], GROUP_WIDTH), BF16),
        scratch_shapes=[pltpu.VMEM((n_keys, width), BF16), pltpu.VMEM((n_keys, width), BF16)],
        compiler_params=_cparams(("parallel", "parallel", "arbitrary")),
        name="diff_attn",
    )(*args)


CONV_PAD = 8
F32_TINY = float(np.finfo(np.float32).tiny)


def _gelu_tanh(t):
    return 0.5 * t * (1.0 + jnp.tanh(math.sqrt(2.0 / math.pi) * (t + 0.044715 * (t * t * t))))


def _softplus(t):
    return jnp.maximum(t, 0.0) + jnp.log1p(jnp.exp(-jnp.abs(t)))


def _rglru_kernel(x_c, g_c, x_x, g_x, cw_ref, cb_ref, wr_ref, br_ref, wi_ref, bi_ref, lam_ref,
                  oc_ref, ox_ref, pad, xc, hs):
    n_tok = CTX_LEN + SEQ
    sc = SCAN_CHUNK
    cw = cw_ref[...]
    cb = cb_ref[...]
    zeros = jnp.zeros((CONV_PAD, LANES), F32)

    for x_ref, base, n in ((x_c, 0, CTX_LEN), (x_x, CTX_LEN, SEQ)):
        pad[0:CONV_PAD, :] = zeros
        pad[CONV_PAD:CONV_PAD + n, :] = x_ref[...]
        pad[CONV_PAD + n:2 * CONV_PAD + n, :] = zeros
        for r in range(n // sc):
            t0 = CONV_PAD + r * sc
            acc = cb + cw[0:1] * pad[t0 - 1:t0 - 1 + sc, :]
            acc += cw[1:2] * pad[t0:t0 + sc, :]
            acc += cw[2:3] * pad[t0 + 1:t0 + 1 + sc, :]
            acc += cw[3:4] * pad[t0 + 2:t0 + 2 + sc, :]
            xc[base + r * sc:base + (r + 1) * sc, :] = acc

    row = lax.broadcasted_iota(jnp.int32, (sc, 1), 0)
    n_chunks = n_tok // sc
    neg_c_softplus = -RG_C * _softplus(-lam_ref[...])
    hs[...] = jnp.zeros_like(hs)

    def scan_chunk(d, r0, carry):
        x = xc[pl.ds(r0, sc), :]
        xb = x.astype(BF16)
        r = _sigmoid(_dot(xb, wr_ref[d]) + br_ref[d:d + 1, :])
        i = _sigmoid(_dot(xb, wi_ref[d]) + bi_ref[d:d + 1, :])
        log_a = neg_c_softplus[d:d + 1, :] * r
        a = jnp.exp(log_a)
        var = -jnp.tanh(log_a) * (a * a + 1.0)
        u = var * lax.rsqrt(jnp.maximum(var, F32_TINY)) * (i * x)
        s = 1
        while s < sc:
            ok, shift = (row >= s, s) if d == 0 else (row < sc - s, sc - s)
            u_prev = jnp.where(ok, pltpu.roll(u, shift, axis=0), 0.0)
            a_prev = jnp.where(ok, pltpu.roll(a, shift, axis=0), 1.0)
            u = u + a * u_prev
            a = a * a_prev
            s *= 2
        h = u + a * carry
        hs[pl.ds(r0, sc), :] += h
        return h[sc - 1:sc, :] if d == 0 else h[0:1, :]

    def body(t, carries):
        fwd = scan_chunk(0, pl.multiple_of(t * sc, sc), carries[0])
        back = jnp.where(t == 0, 0, n_chunks - t)
        bwd = scan_chunk(1, pl.multiple_of(back * sc, sc), carries[1])
        return fwd, bwd

    zero = jnp.zeros((1, LANES), F32)
    lax.fori_loop(0, n_chunks, body, (zero, zero))

    oc_ref[...] = (_gelu_tanh(g_c[...]) * hs[0:CTX_LEN, :]).astype(oc_ref.dtype)
    for r in range(SEQ // sc):
        rows = slice(r * sc, (r + 1) * sc)
        ox_ref[rows, :] = (_gelu_tanh(g_x[rows, :]) * hs[CTX_LEN + r * sc:CTX_LEN + (r + 1) * sc, :]
                           ).astype(ox_ref.dtype)


def _rglru(z_c, z_x, conv_w, conv_b, w_r, b_r, w_i, b_i, lam, *, batch):
    groups = (G_DX, G_DG)
    c_specs, x_specs = _z_specs(groups, LANES)
    out_specs, out_shapes = _mix_out(batch, LANES)
    vec2 = pl.BlockSpec((2, LANES), lambda b, p: (0, p))
    wspec = pl.BlockSpec((2, LANES, LANES), lambda b, p: (0, p, p))
    return pl.pallas_call(
        _rglru_kernel,
        grid=(batch, SLABS),
        in_specs=c_specs + x_specs + [
            pl.BlockSpec((4, LANES), lambda b, p: (0, p)),
            pl.BlockSpec((1, LANES), lambda b, p: (0, p)),
            wspec, vec2, wspec, vec2, vec2,
        ],
        out_specs=out_specs,
        out_shape=out_shapes,
        scratch_shapes=[pltpu.VMEM((SEQ + 2 * CONV_PAD, LANES), F32),
                        pltpu.VMEM((CTX_LEN + SEQ, LANES), F32),
                        pltpu.VMEM((CTX_LEN + SEQ, LANES), F32)],
        compiler_params=_cparams(("parallel", "parallel")),
        name="rglru",
    )(z_c, z_c, z_x, z_x, conv_w, conv_b, w_r, b_r, w_i, b_i, lam)


def _block_diag_dense(w):
    two, nb, bs, _ = w.shape
    eye = jnp.eye(nb, dtype=w.dtype)
    return jnp.einsum("dgio,gh->dgiho", w, eye).reshape(two, nb * bs, nb * bs)


def kernel(x, c, ctx, c_ctx, w_ada, b_ada, norm_pre, norm_post, ffn_w_in, ffn_w_out, w_in, w_out,
           lb_logits, a_norm, b_norm, c_lambda, c_norm, d_conv_w, d_conv_b, d_w_r, d_b_r, d_w_i,
           d_b_i, d_lambda):
    batch = x.shape[0]
    assert x.shape == (batch, SEQ, D_MODEL) and ctx.shape == (batch, CTX_LEN, D_MODEL)
    assert batch < MOD_ROWS

    gla_consts = _gla_constants()
    ret_consts = _retention_constants()
    rope = _rope_constants()

    c_rows = jnp.zeros((MOD_ROWS, D_MODEL), F32).at[:batch].set(c).at[batch].set(c_ctx)
    mod = _ada_table(c_rows, w_ada, b_ada).reshape(DEPTH * MOD_ROWS, N_MOD, D_MODEL)

    n_c = batch * CTX_LEN
    tm_ffn, tm_proj, tm_out = 1024, 1024, 512
    tm_c = min(n_c, 1024)
    assert n_c % tm_c == 0
    row_x = lambda tm: (lambda i: i // (SEQ // tm))
    row_c = lambda i: batch

    hx = x.reshape(batch * SEQ, D_MODEL)
    hc = ctx.reshape(n_c, D_MODEL)
    tile2 = lambda g: jnp.tile(g.reshape(1, -1), (1, LANES // g.shape[-1]))
    w_out_b = w_out.astype(BF16)

    def ffn(h, l, idx, is_ctx, weights):
        return _ffn(h, mod, norm_pre[l, 2 * idx].reshape(1, D_MODEL), norm_post[l, 2 * idx].reshape(1, D_MODEL),
                    weights, layer=l, idx=idx, k0=6 * idx, emit=is_ctx,
                    row_of_tile=row_c if is_ctx else row_x(tm_ffn), tm=tm_c if is_ctx else tm_ffn)

    for l in range(DEPTH):
        ctx_out = l < DEPTH - 1
        gpre1 = norm_pre[l, 1].reshape(1, D_MODEL)
        gpost1 = norm_post[l, 1].reshape(1, D_MODEL)

        hc, ffn_w = ffn(hc, l, 0, True, (ffn_w_in, ffn_w_out))
        hx = ffn(hx, l, 0, False, ffn_w)

        z_c, w_in_l = _inproj(hc, mod, gpre1, w_in, layer=l, row_of_tile=row_c, tm=tm_c, emit=True)
        z_x = _inproj(hx, mod, gpre1, w_in_l, layer=l, row_of_tile=row_x(tm_proj), tm=tm_proj, emit=False)

        a_c, a_x = _hgrn2(z_c, z_x, lb_logits, tile2(a_norm[l]), gla_consts, layer=l, batch=batch, ctx_out=ctx_out)
        b_c, b_x = _retention(z_c, z_x, tile2(b_norm[l]), rope, ret_consts, batch=batch, ctx_out=ctx_out)

        lam_init = 0.8 - 0.6 * math.exp(-0.3 * l)
        cgain = c_norm[l].reshape(1, LANES)
        c_x = _diff_attn(z_x, z_c, z_x, rope, c_lambda[l], cgain, lam_init=lam_init, batch=batch, tq=256)

        d_c, d_x = _rglru(z_c, z_x, d_conv_w[l], d_conv_b[l].reshape(1, -1),
                          _block_diag_dense(d_w_r[l]).astype(BF16), d_b_r[l],
                          _block_diag_dense(d_w_i[l]).astype(BF16), d_b_i[l], d_lambda[l], batch=batch)

        hx = _outproj((a_x, b_x, c_x, d_x), w_out_b[l], hx, mod, gpost1, layer=l,
                      row_of_tile=row_x(tm_out), tm=tm_out)

        if ctx_out:
            c_c = _diff_attn(z_c, z_c, None, rope, c_lambda[l], cgain, lam_init=lam_init, batch=batch, tq=CTX_LEN)
            hc = _outproj((a_c, b_c, c_c, d_c), w_out_b[l], hc, mod, gpost1, layer=l,
                          row_of_tile=row_c, tm=min(tm_c, tm_out))
            hc, ffn_w = ffn(hc, l, 1, True, (ffn_w_in, ffn_w_out))
        else:
            ffn_w = _cast_ffn_weights(ffn_w_in, ffn_w_out, layer=l, idx=1)
        hx = ffn(hx, l, 1, False, ffn_w)

    return hx.reshape(batch, SEQ, D_MODEL)
```

```python
import functools
import math

import numpy as np
import jax
import jax.numpy as jnp
from jax import lax
from jax.experimental import pallas as pl
from jax.experimental.pallas import tpu as pltpu

F32 = jnp.float32
BF16 = jnp.bfloat16

D_MODEL = 2048
SEQ = 2048
CTX_LEN = 256
DEPTH = 2
GRID_W = 64
HEAD_DIM = 64
GROUP_WIDTH = 512
IN_WIDTH = 7168
D_FF = 5504
N_MOD = 9
ROPE_BASE = 10000.0
EPS = 1e-6
RG_C = 8.0
FFN_RESIDUAL = 0.5

LANES = 128
FF_TILE = 512
FF_TILE_EMIT = 256
MOD_ROWS = 8
GLA_CHUNK = 128
HGRN2_WIDTH = 512
RET_WIDTH = 512
GLA_LEVELS = 7
SCAN_CHUNK = 256
VMEM_LIMIT = 56 * 1024 * 1024

G_AQ, G_AFF, G_AFB, G_AV, G_AG = 0, 1, 2, 3, 4
G_BQ, G_BK, G_BV, G_BG = 5, 6, 7, 8
G_CQ, G_CK, G_CV = 9, 10, 11
G_DX, G_DG = 12, 13
SLABS = GROUP_WIDTH // LANES


def _cparams(sem):
    return pltpu.CompilerParams(dimension_semantics=sem, vmem_limit_bytes=VMEM_LIMIT)


def _sigmoid(t):
    return jax.nn.sigmoid(t)


def _silu(t):
    return t * _sigmoid(t)


def _rms(t, gain):
    return t * lax.rsqrt(jnp.mean(t * t, axis=-1, keepdims=True) + EPS) * gain


def _dot(a, b):
    return jnp.dot(a, b, preferred_element_type=F32)


def _dot_nt(a, b):
    return lax.dot_general(a, b, (((1,), (1,)), ((), ())), preferred_element_type=F32)


def _ada_kernel(c_ref, w_ref, b_ref, o_ref):
    s = _silu(c_ref[...])
    s_hi = s.astype(BF16).astype(F32)
    s_mid = (s - s_hi).astype(BF16).astype(F32)
    s_lo = s - s_hi - s_mid
    w = w_ref[...]
    w_hi = w.astype(BF16)
    w_lo = (w - w_hi.astype(F32)).astype(BF16)
    r_hi = _dot(jnp.concatenate([s_hi, s_mid, s_lo], axis=0).astype(BF16), w_hi)
    r_lo = _dot(jnp.concatenate([s_hi, s_mid], axis=0).astype(BF16), w_lo)
    m = MOD_ROWS
    o_ref[...] = (r_hi[0:m] + r_hi[m:2 * m] + r_hi[2 * m:3 * m]) + (r_lo[0:m] + r_lo[m:2 * m]) + b_ref[...]


def _ada_table(c_rows, w_ada, b_ada):
    tn = 1024
    n = N_MOD * D_MODEL
    return pl.pallas_call(
        _ada_kernel,
        grid=(DEPTH, n // tn),
        in_specs=[
            pl.BlockSpec((MOD_ROWS, D_MODEL), lambda l, j: (0, 0)),
            pl.BlockSpec((None, D_MODEL, tn), lambda l, j: (l, 0, j)),
            pl.BlockSpec((None, 1, tn), lambda l, j: (l, 0, j)),
        ],
        out_specs=pl.BlockSpec((None, MOD_ROWS, tn), lambda l, j: (l, 0, j)),
        out_shape=jax.ShapeDtypeStruct((DEPTH, MOD_ROWS, n), F32),
        compiler_params=_cparams(("arbitrary", "arbitrary")),
        name="ada_table",
    )(c_rows, w_ada, b_ada.reshape(DEPTH, 1, n))


ROW_CHUNK = 256
MM_ROWS = 512
MM_COLS = 512
OUT_ROWS = 256


def _norm_modulate_to(u_scr, h_ref, gain_ref, mod_ref, k_shift, k_scale):
    tm = h_ref.shape[0]
    shift = mod_ref[k_shift:k_shift + 1, :]
    gain_scale = gain_ref[...] * (1.0 + mod_ref[k_scale:k_scale + 1, :])

    def body(r, carry):
        r0 = pl.multiple_of(r * ROW_CHUNK, ROW_CHUNK)
        h = h_ref[pl.ds(r0, ROW_CHUNK), :]
        u_scr[pl.ds(r0, ROW_CHUNK), :] = (_rms(h, gain_scale) + shift).astype(BF16)
        return carry

    lax.fori_loop(0, tm // ROW_CHUNK, body, 0)


def _cast_ffn_tiles(w_refs, out_refs, j, last):
    tf = w_refs[0].shape[1]
    overlap = pl.cdiv(D_FF, tf) * tf - D_FF
    tiles = [r[...].astype(BF16) for r in w_refs]
    for t, out, axis in zip(tiles, out_refs, (1, 1, 0)):
        if overlap:
            keep = lax.slice_in_dim(t, overlap, tf, axis=axis)
            fill = jnp.zeros_like(lax.slice_in_dim(t, 0, overlap, axis=axis))
            t = jnp.where(j == last, jnp.concatenate([keep, fill], axis=axis), t)
        out[...] = t
    return tiles


def _ffn_kernel(h_ref, mod_ref, gpre_ref, gpost_ref, wg_ref, wu_ref, wout_ref, *rest, k0, emit):
    if emit:
        o_ref, wg_out, wu_out, wout_out, u_scr = rest
    else:
        o_ref, u_scr = rest
    j = pl.program_id(1)
    last = pl.num_programs(1) - 1
    tm = h_ref.shape[0]
    tf = wg_ref.shape[1]
    overlap = pl.cdiv(D_FF, tf) * tf - D_FF

    @pl.when(j == 0)
    def _():
        _norm_modulate_to(u_scr, h_ref, gpre_ref, mod_ref, k0, k0 + 1)
        o_ref[...] = jnp.zeros_like(o_ref)

    if emit:
        wg, wu, wout = _cast_ffn_tiles((wg_ref, wu_ref, wout_ref), (wg_out, wu_out, wout_out), j, last)
    else:
        wg, wu, wout = wg_ref[...], wu_ref[...], wout_ref[...]

    col = lax.broadcasted_iota(jnp.int32, (1, tf), 1)
    fresh = jnp.logical_or(j < last, col >= overlap)
    for r in range(tm // MM_ROWS):
        rows = slice(r * MM_ROWS, (r + 1) * MM_ROWS)
        u = u_scr[rows, :]
        act = jnp.where(fresh, _silu(_dot(u, wg)) * _dot(u, wu), 0.0).astype(BF16)
        for n0 in range(0, D_MODEL, MM_COLS):
            o_ref[rows, n0:n0 + MM_COLS] += _dot(act, wout[:, n0:n0 + MM_COLS])

    @pl.when(j == last)
    def _():
        gate_gain = FFN_RESIDUAL * mod_ref[k0 + 2:k0 + 3, :] * gpost_ref[...]

        def body(r, carry):
            r0 = pl.multiple_of(r * ROW_CHUNK, ROW_CHUNK)
            y = o_ref[pl.ds(r0, ROW_CHUNK), :]
            o_ref[pl.ds(r0, ROW_CHUNK), :] = h_ref[pl.ds(r0, ROW_CHUNK), :] + _rms(y, gate_gain)
            return carry

        lax.fori_loop(0, tm // ROW_CHUNK, body, 0)


def _mod_spec(layer, row_of_tile):
    return pl.BlockSpec((None, N_MOD, D_MODEL), lambda i, *_: (layer * MOD_ROWS + row_of_tile(i), 0, 0))


def _ff_offset(j, tf, base=0):
    return (base // LANES + jnp.minimum(j * (tf // LANES), (D_FF - tf) // LANES)) * LANES


def _ffn_f32_weight_specs(layer, idx, tf):
    el = pl.Element
    in_specs = [
        pl.BlockSpec((None, None, el(D_MODEL), el(tf)), lambda *g: (layer, idx, 0, _ff_offset(g[-1], tf))),
        pl.BlockSpec((None, None, el(D_MODEL), el(tf)), lambda *g: (layer, idx, 0, _ff_offset(g[-1], tf, D_FF))),
        pl.BlockSpec((None, None, el(tf), el(D_MODEL)), lambda *g: (layer, idx, _ff_offset(g[-1], tf), 0)),
    ]
    d_ff_pad = pl.cdiv(D_FF, tf) * tf
    cols_out = pl.BlockSpec((D_MODEL, tf), lambda *g: (0, g[-1]))
    out_specs = [cols_out, cols_out, pl.BlockSpec((tf, D_MODEL), lambda *g: (g[-1], 0))]
    out_shape = [jax.ShapeDtypeStruct((D_MODEL, d_ff_pad), BF16), jax.ShapeDtypeStruct((D_MODEL, d_ff_pad), BF16),
                 jax.ShapeDtypeStruct((d_ff_pad, D_MODEL), BF16)]
    return in_specs, out_specs, out_shape


def _ffn_weight_cast_kernel(wg_ref, wu_ref, wout_ref, wg_out, wu_out, wout_out):
    _cast_ffn_tiles((wg_ref, wu_ref, wout_ref), (wg_out, wu_out, wout_out), pl.program_id(0), pl.num_programs(0) - 1)


def _cast_ffn_weights(w_in, w_out, *, layer, idx):
    tf = FF_TILE
    in_specs, out_specs, out_shape = _ffn_f32_weight_specs(layer, idx, tf)
    return tuple(pl.pallas_call(
        _ffn_weight_cast_kernel,
        grid=(pl.cdiv(D_FF, tf),),
        in_specs=in_specs,
        out_specs=out_specs,
        out_shape=out_shape,
        compiler_params=_cparams(("arbitrary",)),
        name="ffn_weight_cast",
    )(w_in, w_in, w_out))


def _ffn(h, mod, gpre, gpost, weights, *, layer, idx, k0, row_of_tile, tm, emit):
    n = h.shape[0]
    el = pl.Element
    tf = FF_TILE_EMIT if emit else FF_TILE
    col_tile = pl.BlockSpec((el(D_MODEL), el(tf)), lambda i, j: (0, _ff_offset(j, tf)))
    row_tile = pl.BlockSpec((el(tf), el(D_MODEL)), lambda i, j: (_ff_offset(j, tf), 0))
    h_spec = pl.BlockSpec((tm, D_MODEL), lambda i, j: (i, 0))
    out_specs, out_shape = h_spec, jax.ShapeDtypeStruct((n, D_MODEL), F32)
    if emit:
        assert n == tm, "weight tiles are cast once: one row tile only"
        w_in, w_out = weights
        w_args = (w_in, w_in, w_out)
        w_specs, w_out_specs, w_out_shape = _ffn_f32_weight_specs(layer, idx, tf)
        out_specs = [h_spec] + w_out_specs
        out_shape = [out_shape] + w_out_shape
    else:
        w_args = weights
        w_specs = [col_tile, col_tile, row_tile]
    out = pl.pallas_call(
        functools.partial(_ffn_kernel, k0=k0, emit=emit),
        grid=(n // tm, pl.cdiv(D_FF, tf)),
        in_specs=[
            h_spec,
            _mod_spec(layer, row_of_tile),
            pl.BlockSpec((1, D_MODEL), lambda i, j: (0, 0)),
            pl.BlockSpec((1, D_MODEL), lambda i, j: (0, 0)),
        ] + w_specs,
        out_specs=out_specs,
        out_shape=out_shape,
        scratch_shapes=[pltpu.VMEM((tm, D_MODEL), BF16)],
        compiler_params=_cparams(("parallel", "arbitrary")),
        name="ffn_cast" if emit else "ffn",
    )(h, mod, gpre, gpost, *w_args)
    return (out[0], tuple(out[1:])) if emit else out


def _inproj_kernel(h_ref, mod_ref, gpre_ref, w_ref, *rest, emit):
    if emit:
        z_ref, w_out, u_scr = rest
    else:
        z_ref, u_scr = rest
    j = pl.program_id(1)
    tm = h_ref.shape[0]

    @pl.when(j == 0)
    def _():
        _norm_modulate_to(u_scr, h_ref, gpre_ref, mod_ref, 3, 4)

    w = w_ref[...]
    if emit:
        w = w.astype(BF16)
        w_out[...] = w
    for r in range(tm // MM_ROWS):
        rows = slice(r * MM_ROWS, (r + 1) * MM_ROWS)
        z_ref[rows, :] = _dot(u_scr[rows, :], w)


def _inproj(h, mod, gpre, w, *, layer, row_of_tile, tm, emit):
    n = h.shape[0]
    tn = 512 if emit else 1024
    if emit:
        assert n == tm, "weight tiles are cast once: one row tile only"
        w_spec = pl.BlockSpec((None, D_MODEL, tn), lambda i, j: (layer, 0, j))
    else:
        w_spec = pl.BlockSpec((D_MODEL, tn), lambda i, j: (0, j))
    z_spec = pl.BlockSpec((tm, tn), lambda i, j: (i, j))
    z_shape = jax.ShapeDtypeStruct((n, IN_WIDTH), F32)
    out = pl.pallas_call(
        functools.partial(_inproj_kernel, emit=emit),
        grid=(n // tm, IN_WIDTH // tn),
        in_specs=[
            pl.BlockSpec((tm, D_MODEL), lambda i, j: (i, 0)),
            _mod_spec(layer, row_of_tile),
            pl.BlockSpec((1, D_MODEL), lambda i, j: (0, 0)),
            w_spec,
        ],
        out_specs=[z_spec, pl.BlockSpec((D_MODEL, tn), lambda i, j: (0, j))] if emit else z_spec,
        out_shape=[z_shape, jax.ShapeDtypeStruct((D_MODEL, IN_WIDTH), BF16)] if emit else z_shape,
        scratch_shapes=[pltpu.VMEM((tm, D_MODEL), BF16)],
        compiler_params=_cparams(("parallel", "arbitrary")),
        name="inproj_cast" if emit else "inproj",
    )(h, mod, gpre, w)
    return tuple(out) if emit else out


def _outproj_kernel(a_ref, b_ref, c_ref, d_ref, w_ref, h_ref, mod_ref, gpost_ref, o_ref):
    tm = h_ref.shape[0]
    gate_gain = mod_ref[5:6, :] * gpost_ref[...]
    for r in range(tm // OUT_ROWS):
        rows = slice(r * OUT_ROWS, (r + 1) * OUT_ROWS)
        y = _dot(a_ref[rows, :], w_ref[0 * GROUP_WIDTH:1 * GROUP_WIDTH, :])
        y += _dot(b_ref[rows, :], w_ref[1 * GROUP_WIDTH:2 * GROUP_WIDTH, :])
        y += _dot(c_ref[rows, :], w_ref[2 * GROUP_WIDTH:3 * GROUP_WIDTH, :])
        y += _dot(d_ref[rows, :], w_ref[3 * GROUP_WIDTH:4 * GROUP_WIDTH, :])
        o_ref[rows, :] = h_ref[rows, :] + _rms(y, gate_gain)


def _outproj(mix, w, h, mod, gpost, *, layer, row_of_tile, tm):
    n = h.shape[0]
    mix_spec = pl.BlockSpec((tm, GROUP_WIDTH), lambda i: (i, 0))
    return pl.pallas_call(
        _outproj_kernel,
        grid=(n // tm,),
        in_specs=[
            mix_spec, mix_spec, mix_spec, mix_spec,
            pl.BlockSpec((D_MODEL, D_MODEL), lambda i: (0, 0)),
            pl.BlockSpec((tm, D_MODEL), lambda i: (i, 0)),
            _mod_spec(layer, row_of_tile),
            pl.BlockSpec((1, D_MODEL), lambda i: (0, 0)),
        ],
        out_specs=pl.BlockSpec((tm, D_MODEL), lambda i: (i, 0)),
        out_shape=jax.ShapeDtypeStruct((n, D_MODEL), F32),
        compiler_params=_cparams(("parallel",)),
        name="outproj",
    )(*mix, w, h, mod, gpost)


def _gla_constants():
    c = GLA_CHUNK
    idx = np.arange(c)
    i, r = idx[:, None], idx[None, :]
    mats = np.zeros((2, GLA_LEVELS + 2, c, c), np.float32)
    masks = np.zeros((2, GLA_LEVELS + 1, c, c), np.float32)
    mats[0, 0] = r <= i
    mats[0, 1] = r > i
    mats[1, 0] = r >= i
    mats[1, 1] = r < i
    for m in range(GLA_LEVELS):
        hs = c >> (m + 1)
        start = (idx // (2 * hs)) * (2 * hs)
        mid = (start + hs - 1)[:, None]
        right = ((idx // hs) % 2 == 1)[:, None]
        mats[0, 2 + m] = np.where(right, (r > mid) & (r <= i), (r > i) & (r <= mid))
        mats[1, 2 + m] = np.where(right, (r > mid) & (r < i), (r >= i) & (r <= mid))
        same = (start[:, None] == start[None, :])
        q_right = right & ~right.T & same
        masks[0, m] = q_right
        masks[1, m] = q_right.T
    masks[0, GLA_LEVELS] = np.eye(c)
    masks[1, GLA_LEVELS] = np.eye(c)
    mats = mats.reshape(2, (GLA_LEVELS + 2) * c, c)
    mats = np.concatenate([mats, mats], axis=-1)
    masks = np.tile(masks, (1, 1, 1, 2))
    return jnp.asarray(mats, BF16), jnp.asarray(masks, F32)


def _retention_constants():
    c = GLA_CHUNK
    heads = GROUP_WIDTH // HEAD_DIM
    log_decay = np.log1p(-np.exp2(-5.0 - np.arange(heads, dtype=np.float64)))
    ld_lane = np.repeat(log_decay, HEAD_DIM).reshape(SLABS, 1, LANES)
    idx = np.arange(c, dtype=np.float64)
    i, j = idx[:, None], idx[None, :]
    dq = np.zeros((SLABS, 2, c, LANES))
    dk = np.zeros((SLABS, 2, c, LANES))
    dq[:, 0] = np.exp((idx[None, :, None] + 1.0) * ld_lane)
    dk[:, 0] = np.exp((c - 1.0 - idx[None, :, None]) * ld_lane)
    dq[:, 1] = np.exp((c - idx[None, :, None]) * ld_lane)
    dk[:, 1] = np.exp(idx[None, :, None] * ld_lane)
    gt = np.exp(c * ld_lane)
    dm = np.zeros((SLABS, 2, 2, c, c))
    for p in range(SLABS):
        for hh in range(2):
            ld = log_decay[2 * p + hh]
            dm[p, 0, hh] = np.where(i >= j, np.exp((i - j) * ld), 0.0)
            dm[p, 1, hh] = np.where(j >= i, np.exp((j - i) * ld), 0.0)
    dm = dm.transpose(0, 1, 3, 2, 4).reshape(SLABS, 2, c, 2 * c)
    f = lambda t: jnp.asarray(t.astype(np.float32))
    return f(dq), f(dk), f(gt), f(dm)


def _rope_constants():
    quarter = HEAD_DIM // 4
    inv_freq = ROPE_BASE ** (-jnp.arange(quarter, dtype=F32) / quarter)
    rows = SEQ // GRID_W
    row = jnp.repeat(jnp.arange(rows, dtype=F32), GRID_W)
    col = jnp.tile(jnp.arange(GRID_W, dtype=F32), rows)
    ang = jnp.concatenate([row[:, None] * inv_freq, col[:, None] * inv_freq], axis=-1)
    cos, sin = jnp.cos(ang), jnp.sin(ang)
    reps = LANES // HEAD_DIM
    return (jnp.tile(jnp.concatenate([cos, cos], axis=-1), (1, reps)),
            jnp.tile(jnp.concatenate([-sin, sin], axis=-1), (1, reps)))


def _rope(t, cos, sin):
    lane = lax.broadcasted_iota(jnp.int32, t.shape, 1)
    first_half = (lane % HEAD_DIM) < (HEAD_DIM // 2)
    partner = jnp.where(first_half,
                        pltpu.roll(t, LANES - HEAD_DIM // 2, axis=1),
                        pltpu.roll(t, HEAD_DIM // 2, axis=1))
    return t * cos + partner * sin


def _head_lane_masks():
    lane = lax.broadcasted_iota(jnp.int32, (1, LANES), 1)
    return lane < HEAD_DIM, lane >= HEAD_DIM


def _stack_heads(t):
    hm0, hm1 = _head_lane_masks()
    return jnp.concatenate([jnp.where(hm0, t, 0.0).astype(BF16),
                            jnp.where(hm1, t, 0.0).astype(BF16)], axis=0)


def _gla_state_update(ke, v, gtot, st):
    upd = lax.dot_general(v.astype(BF16), ke.astype(BF16), (((0,), (0,)), ((), ())),
                          preferred_element_type=F32)
    row_head = lax.broadcasted_iota(jnp.int32, (LANES, LANES), 0) // HEAD_DIM
    col_head = lax.broadcasted_iota(jnp.int32, (LANES, LANES), 1) // HEAD_DIM
    return st * gtot + jnp.where(row_head == col_head, upd, 0.0)


def _gla_chunk_update(qe, ke, v, gtot, s_cat, st):
    o = _dot_nt(qe.astype(BF16), st.astype(BF16))
    o += _dot(s_cat.astype(BF16), _stack_heads(v))
    return o, _gla_state_update(ke, v, gtot, st)


def _slab(t, s):
    return t[:, s * LANES:(s + 1) * LANES]


def _gla_finish_chunk(acc_ref, zg_ref, gain, o_ref, r0):
    hm0, _ = _head_lane_masks()
    for s in range(acc_ref.shape[1] // LANES):
        cols = slice(s * LANES, (s + 1) * LANES)
        o = acc_ref[pl.ds(r0, GLA_CHUNK), cols]
        sq = o * o
        ms0 = jnp.sum(jnp.where(hm0, sq, 0.0), axis=1, keepdims=True) * (1.0 / HEAD_DIM)
        ms1 = jnp.sum(jnp.where(hm0, 0.0, sq), axis=1, keepdims=True) * (1.0 / HEAD_DIM)
        inv = jnp.where(hm0, lax.rsqrt(ms0 + EPS), lax.rsqrt(ms1 + EPS))
        g = zg_ref[pl.ds(r0, GLA_CHUNK), cols]
        o_ref[pl.ds(r0, GLA_CHUNK), cols] = (o * inv * gain * _silu(g)).astype(o_ref.dtype)


def _gla_scan(chunk_fn, accs, zgs, gain, outs, ctx_out):
    n_slabs = accs[1].shape[1] // LANES
    if ctx_out:
        accs[0][...] = jnp.zeros_like(accs[0])
    else:
        outs[0][...] = jnp.zeros_like(outs[0])
    accs[1][...] = jnp.zeros_like(accs[1])

    def run(seg, states):
        acc_ref = accs[seg]
        n = acc_ref.shape[0] // GLA_CHUNK
        assert n % 2 == 0
        need_out = seg == 1 or ctx_out

        def body(finish):
            def step(t, states):
                new_states, visited = [], []
                for d in (0, 1):
                    r0 = pl.multiple_of((t if d == 0 else n - 1 - t) * GLA_CHUNK, GLA_CHUNK)
                    o, st = chunk_fn(d, seg, r0, states[d], need_out)
                    if need_out:
                        acc_ref[pl.ds(r0, GLA_CHUNK), :] += o
                        visited.append(r0)
                    new_states.append(st)
                if finish:
                    for r0 in visited:
                        _gla_finish_chunk(acc_ref, zgs[seg], gain, outs[seg], r0)
                return tuple(new_states)
            return step

        states = lax.fori_loop(0, n // 2, body(False), states)
        return lax.fori_loop(n // 2, n, body(True), states)

    zero = tuple(jnp.zeros((LANES, LANES), F32) for _ in range(n_slabs))
    run(1, run(0, (zero, zero)))


def _hgrn2_kernel(zq_c, zff_c, zfb_c, zv_c, zg_c, zq_x, zff_x, zfb_x, zv_x, zg_x,
                  lbl_ref, gain_ref, mats_ref, masks_ref, oc_ref, ox_ref, acc_c, acc_x, *, layer, ctx_out):
    c = GLA_CHUNK
    n_slabs = acc_c.shape[1] // LANES
    lbl = lbl_ref[...]
    e = jnp.exp(lbl - jnp.max(lbl, axis=0, keepdims=True))
    sm = e / jnp.sum(e, axis=0, keepdims=True)
    lower = jnp.zeros_like(sm[0])
    for l in range(1, layer + 1):
        lower = lower + sm[l]
    zq = (zq_c, zq_x)
    zf = ((zff_c, zff_x), (zfb_c, zfb_x))
    zv = (zv_c, zv_x)
    row = lax.broadcasted_iota(jnp.int32, (c, 1), 0)

    def chunk_fn(d, seg, r0, states, need_out):
        lb = lower[d:d + 1, :]
        f = lb + (1.0 - lb) * _sigmoid(zf[d][seg][pl.ds(r0, c), :])
        lf = jnp.log(f)
        k = 1.0 - f
        v = zv[seg][pl.ds(r0, c), :]
        lf_hi = lf.astype(BF16)
        lf_lo = (lf - lf_hi.astype(F32)).astype(BF16)
        lf_cat = jnp.concatenate([lf_hi, lf_lo], axis=0)
        gtot = jnp.exp(jnp.sum(lf, axis=0, keepdims=True))
        if not need_out:
            ke = k * jnp.exp(_dot(mats_ref[d, c:2 * c, :], lf_cat))
            return None, tuple(_gla_state_update(_slab(ke, s), _slab(v, s), _slab(gtot, s), states[s])
                               for s in range(n_slabs))
        q = _silu(zq[seg][pl.ds(r0, c), :])
        ex = jnp.exp(_dot(mats_ref[d], lf_cat))
        qe = q * ex[0:c]
        ke = k * ex[c:2 * c]
        s_cats = [jnp.zeros((c, 2 * c), F32) for _ in range(n_slabs)]
        for lvl in range(GLA_LEVELS + 1):
            if lvl < GLA_LEVELS:
                hs = c >> (lvl + 1)
                q_side = ((row // hs) % 2) == (1 if d == 0 else 0)
                rhs_src = jnp.where(q_side, q, k) * ex[(2 + lvl) * c:(3 + lvl) * c]
                lhs = rhs_src.astype(BF16)
            else:
                lhs, rhs_src = q.astype(BF16), k
            mask = masks_ref[d, lvl]
            for s in range(n_slabs):
                s_cats[s] = s_cats[s] + mask * _dot_nt(_slab(lhs, s), _stack_heads(_slab(rhs_src, s)))
        outs = [_gla_chunk_update(_slab(qe, s), _slab(ke, s), _slab(v, s), _slab(gtot, s), s_cats[s],
                                  states[s]) for s in range(n_slabs)]
        return jnp.concatenate([o for o, _ in outs], axis=1), tuple(st for _, st in outs)

    _gla_scan(chunk_fn, (acc_c, acc_x), (zg_c, zg_x), gain_ref[...], (oc_ref, ox_ref), ctx_out)


def _z_specs(groups, width, single_buffered=()):
    per_group = GROUP_WIDTH // width
    mode = lambda g: dict(pipeline_mode=pl.Buffered(1)) if g in single_buffered else {}
    c_specs = [pl.BlockSpec((CTX_LEN, width), lambda b, p, g=g: (b, g * per_group + p)) for g in groups]
    x_specs = [pl.BlockSpec((SEQ, width), lambda b, p, g=g: (b, g * per_group + p), **mode(g)) for g in groups]
    return c_specs, x_specs


def _mix_out(batch, width):
    specs = [pl.BlockSpec((CTX_LEN, width), lambda b, p: (b, p)),
             pl.BlockSpec((SEQ, width), lambda b, p: (b, p))]
    shapes = [jax.ShapeDtypeStruct((batch * CTX_LEN, GROUP_WIDTH), BF16),
              jax.ShapeDtypeStruct((batch * SEQ, GROUP_WIDTH), BF16)]
    return specs, shapes


def _gla_scratch(width):
    return [pltpu.VMEM((CTX_LEN, width), F32), pltpu.VMEM((SEQ, width), F32)]


def _hgrn2(z_c, z_x, lb_logits, gain, consts, *, layer, batch, ctx_out):
    mats, masks = consts
    groups = (G_AQ, G_AFF, G_AFB, G_AV, G_AG)
    width = HGRN2_WIDTH
    c_specs, x_specs = _z_specs(groups, width, single_buffered=(G_AFB, G_AV, G_AG))
    out_specs, out_shapes = _mix_out(batch, width)
    n_c = len(groups)
    return pl.pallas_call(
        functools.partial(_hgrn2_kernel, layer=layer, ctx_out=ctx_out),
        grid=(batch, GROUP_WIDTH // width),
        in_specs=c_specs + x_specs + [
            pl.BlockSpec((DEPTH, 2, width), lambda b, p: (0, 0, p)),
            pl.BlockSpec((1, LANES), lambda b, p: (0, 0)),
            pl.BlockSpec(mats.shape, lambda b, p: (0, 0, 0)),
            pl.BlockSpec(masks.shape, lambda b, p: (0, 0, 0, 0)),
        ],
        out_specs=out_specs,
        out_shape=out_shapes,
        scratch_shapes=_gla_scratch(width),
        compiler_params=_cparams(("parallel", "parallel")),
        name="hgrn2",
    )(*([z_c] * n_c), *([z_x] * n_c), lb_logits, gain, mats, masks)


def _retention_kernel(zq_c, zk_c, zv_c, zg_c, zq_x, zk_x, zv_x, zg_x, cos_ref, sin_ref,
                      dq_ref, dk_ref, gt_ref, dm_ref, gain_ref, oc_ref, ox_ref, acc_c, acc_x, *, ctx_out):
    c = GLA_CHUNK
    n_slabs = acc_c.shape[1] // LANES
    zq, zk, zv = (zq_c, zq_x), (zk_c, zk_x), (zv_c, zv_x)

    def chunk_fn(d, seg, r0, states, need_out):
        def rotated(t):
            return _rope(t, cos_ref[pl.ds(r0, c), :], sin_ref[pl.ds(r0, c), :]) if seg == 1 else t

        outs = []
        for s in range(n_slabs):
            cols = slice(s * LANES, (s + 1) * LANES)
            k = rotated(zk[seg][pl.ds(r0, c), cols] * (HEAD_DIM ** -0.5))
            v = zv[seg][pl.ds(r0, c), cols]
            if not need_out:
                outs.append((None, _gla_state_update(k * dk_ref[s, d], v, gt_ref[s], states[s])))
                continue
            q = rotated(zq[seg][pl.ds(r0, c), cols])
            s_cat = dm_ref[s, d] * _dot_nt(q.astype(BF16), _stack_heads(k))
            outs.append(_gla_chunk_update(q * dq_ref[s, d], k * dk_ref[s, d], v, gt_ref[s], s_cat, states[s]))
        o = jnp.concatenate([o for o, _ in outs], axis=1) if need_out else None
        return o, tuple(st for _, st in outs)

    _gla_scan(chunk_fn, (acc_c, acc_x), (zg_c, zg_x), gain_ref[...], (oc_ref, ox_ref), ctx_out)


def _retention(z_c, z_x, gain, rope, consts, *, batch, ctx_out):
    dq, dk, gt, dm = consts
    cos, sin = rope
    groups = (G_BQ, G_BK, G_BV, G_BG)
    width = RET_WIDTH
    c_specs, x_specs = _z_specs(groups, width)
    out_specs, out_shapes = _mix_out(batch, width)
    n_c = len(groups)
    c = GLA_CHUNK
    ns = width // LANES
    return pl.pallas_call(
        functools.partial(_retention_kernel, ctx_out=ctx_out),
        grid=(batch, GROUP_WIDTH // width),
        in_specs=c_specs + x_specs + [
            pl.BlockSpec((SEQ, LANES), lambda b, p: (0, 0)),
            pl.BlockSpec((SEQ, LANES), lambda b, p: (0, 0)),
            pl.BlockSpec((ns, 2, c, LANES), lambda b, p: (p, 0, 0, 0)),
            pl.BlockSpec((ns, 2, c, LANES), lambda b, p: (p, 0, 0, 0)),
            pl.BlockSpec((ns, 1, LANES), lambda b, p: (p, 0, 0)),
            pl.BlockSpec((ns, 2, c, 2 * c), lambda b, p: (p, 0, 0, 0)),
            pl.BlockSpec((1, LANES), lambda b, p: (0, 0)),
        ],
        out_specs=out_specs,
        out_shape=out_shapes,
        scratch_shapes=_gla_scratch(width),
        compiler_params=_cparams(("parallel", "parallel")),
        name="retention",
    )(*([z_c] * n_c), *([z_x] * n_c), cos, sin, dq, dk, gt, dm, gain)


ATTN_FILL_ROWS = 256
ATTN_WIDTH = 512
LOG2_E = 1.4426950408889634


def _diff_attn_kernel(*refs, lam_init, with_latent):
    if with_latent:
        q_ref, kc_ref, vc_ref, kx_ref, vx_ref, cos_ref, sin_ref, lam_ref, gain_ref, o_ref, kk, vv = refs
    else:
        q_ref, kc_ref, vc_ref, lam_ref, gain_ref, o_ref, kk, vv = refs
    tq, width = q_ref.shape
    n_heads = width // LANES

    @pl.when(pl.program_id(2) == 0)
    def _():
        kk[0:CTX_LEN, :] = kc_ref[...].astype(BF16)
        vv[0:CTX_LEN, :] = vc_ref[...].astype(BF16)
        if with_latent:
            for r0 in range(0, SEQ, ATTN_FILL_ROWS):
                rows = slice(r0, r0 + ATTN_FILL_ROWS)
                dst = slice(CTX_LEN + r0, CTX_LEN + r0 + ATTN_FILL_ROWS)
                vv[dst, :] = vx_ref[rows, :].astype(BF16)
                for h in range(n_heads):
                    cols = slice(h * LANES, (h + 1) * LANES)
                    kk[dst, cols] = _rope(kx_ref[rows, cols], cos_ref[rows, :], sin_ref[rows, :]).astype(BF16)

    lv = lam_ref[...]
    lam = (jnp.exp(jnp.sum(lv[0:1] * lv[1:2], axis=1, keepdims=True))
           - jnp.exp(jnp.sum(lv[2:3] * lv[3:4], axis=1, keepdims=True)) + lam_init)
    lane = lax.broadcasted_iota(jnp.int32, (1, LANES), 1)
    gain = gain_ref[...]
    if with_latent:
        q0 = pl.multiple_of(pl.program_id(2) * tq, tq)
        cos, sin = cos_ref[pl.ds(q0, tq), :], sin_ref[pl.ds(q0, tq), :]

    for h in range(n_heads):
        cols = slice(h * LANES, (h + 1) * LANES)
        q = q_ref[:, cols]
        if with_latent:
            q = _rope(q, cos, sin)
        q = (q * (HEAD_DIM ** -0.5 * LOG2_E)).astype(BF16)
        keys = kk[:, cols]

        def weights(qh):
            s = _dot_nt(qh, keys)
            p = jnp.exp2(s - jnp.max(s, axis=-1, keepdims=True))
            return p, jnp.sum(p, axis=-1, keepdims=True)

        zero = jnp.zeros_like(q)
        p1, l1 = weights(jnp.where(lane < HEAD_DIM, q, zero))
        p2, l2 = weights(jnp.where(lane < HEAD_DIM, zero, q))
        a = p1 - (lam * l1 / l2) * p2
        o = _dot(a.astype(BF16), vv[:, cols]) / l1
        o_ref[:, cols] = (_rms(o, gain) * (1.0 - lam_init)).astype(o_ref.dtype)


def _diff_attn(z_q, z_c, z_x, rope, lam_vecs, gain, *, lam_init, batch, tq):
    with_latent = z_x is not None
    q_len = z_q.shape[0] // batch
    nq = q_len // tq
    width = ATTN_WIDTH
    per_group = GROUP_WIDTH // width
    n_keys = CTX_LEN + (SEQ if with_latent else 0)
    block = lambda rows, g: pl.BlockSpec((rows, width), lambda b, h, i, g=g: (b, g * per_group + h))
    qspec = pl.BlockSpec((tq, width), lambda b, h, i: (b * nq + i, G_CQ * per_group + h))
    table = pl.BlockSpec((SEQ, LANES), lambda b, h, i: (0, 0))
    in_specs = [qspec, block(CTX_LEN, G_CK), block(CTX_LEN, G_CV)]
    args = [z_q, z_c, z_c]
    if with_latent:
        in_specs += [block(SEQ, G_CK), block(SEQ, G_CV), table, table]
        args += [z_x, z_x, *rope]
    in_specs += [pl.BlockSpec((4, HEAD_DIM), lambda b, h, i: (0, 0)),
                 pl.BlockSpec((1, LANES), lambda b, h, i: (0, 0))]
    args += [lam_vecs, gain]
    return pl.pallas_call(
        functools.partial(_diff_attn_kernel, lam_init=lam_init, with_latent=with_latent),
        grid=(batch, per_group, nq),
        in_specs=in_specs,
        out_specs=pl.BlockSpec((tq, width), lambda b, h, i: (b * nq + i, h)),
        out_shape=jax.ShapeDtypeStruct((z_q.shape[0], GROUP_WIDTH), BF16),
        scratch_shapes=[pltpu.VMEM((n_keys, width), BF16), pltpu.VMEM((n_keys, width), BF16)],
        compiler_params=_cparams(("parallel", "parallel", "arbitrary")),
        name="diff_attn",
    )(*args)


CONV_PAD = 8
F32_TINY = float(np.finfo(np.float32).tiny)


def _gelu_tanh(t):
    return 0.5 * t * (1.0 + jnp.tanh(math.sqrt(2.0 / math.pi) * (t + 0.044715 * (t * t * t))))


def _softplus(t):
    return jnp.maximum(t, 0.0) + jnp.log1p(jnp.exp(-jnp.abs(t)))


def _rglru_kernel(x_c, g_c, x_x, g_x, cw_ref, cb_ref, wr_ref, br_ref, wi_ref, bi_ref, lam_ref,
                  oc_ref, ox_ref, pad, xc, hs):
    n_tok = CTX_LEN + SEQ
    sc = SCAN_CHUNK
    cw = cw_ref[...]
    cb = cb_ref[...]
    zeros = jnp.zeros((CONV_PAD, LANES), F32)

    for x_ref, base, n in ((x_c, 0, CTX_LEN), (x_x, CTX_LEN, SEQ)):
        pad[0:CONV_PAD, :] = zeros
        pad[CONV_PAD:CONV_PAD + n, :] = x_ref[...]
        pad[CONV_PAD + n:2 * CONV_PAD + n, :] = zeros
        for r in range(n // sc):
            t0 = CONV_PAD + r * sc
            acc = cb + cw[0:1] * pad[t0 - 1:t0 - 1 + sc, :]
            acc += cw[1:2] * pad[t0:t0 + sc, :]
            acc += cw[2:3] * pad[t0 + 1:t0 + 1 + sc, :]
            acc += cw[3:4] * pad[t0 + 2:t0 + 2 + sc, :]
            xc[base + r * sc:base + (r + 1) * sc, :] = acc

    row = lax.broadcasted_iota(jnp.int32, (sc, 1), 0)
    n_chunks = n_tok // sc
    neg_c_softplus = -RG_C * _softplus(-lam_ref[...])
    hs[...] = jnp.zeros_like(hs)

    def scan_chunk(d, r0, carry):
        x = xc[pl.ds(r0, sc), :]
        xb = x.astype(BF16)
        r = _sigmoid(_dot(xb, wr_ref[d]) + br_ref[d:d + 1, :])
        i = _sigmoid(_dot(xb, wi_ref[d]) + bi_ref[d:d + 1, :])
        log_a = neg_c_softplus[d:d + 1, :] * r
        a = jnp.exp(log_a)
        var = -jnp.tanh(log_a) * (a * a + 1.0)
        u = var * lax.rsqrt(jnp.maximum(var, F32_TINY)) * (i * x)
        s = 1
        while s < sc:
            ok, shift = (row >= s, s) if d == 0 else (row < sc - s, sc - s)
            u_prev = jnp.where(ok, pltpu.roll(u, shift, axis=0), 0.0)
            a_prev = jnp.where(ok, pltpu.roll(a, shift, axis=0), 1.0)
            u = u + a * u_prev
            a = a * a_prev
            s *= 2
        h = u + a * carry
        hs[pl.ds(r0, sc), :] += h
        return h[sc - 1:sc, :] if d == 0 else h[0:1, :]

    def body(t, carries):
        fwd = scan_chunk(0, pl.multiple_of(t * sc, sc), carries[0])
        back = jnp.where(t == 0, 0, n_chunks - t)
        bwd = scan_chunk(1, pl.multiple_of(back * sc, sc), carries[1])
        return fwd, bwd

    zero = jnp.zeros((1, LANES), F32)
    lax.fori_loop(0, n_chunks, body, (zero, zero))

    oc_ref[...] = (_gelu_tanh(g_c[...]) * hs[0:CTX_LEN, :]).astype(oc_ref.dtype)
    for r in range(SEQ // sc):
        rows = slice(r * sc, (r + 1) * sc)
        ox_ref[rows, :] = (_gelu_tanh(g_x[rows, :]) * hs[CTX_LEN + r * sc:CTX_LEN + (r + 1) * sc, :]
                           ).astype(ox_ref.dtype)


def _rglru(z_c, z_x, conv_w, conv_b, w_r, b_r, w_i, b_i, lam, *, batch):
    groups = (G_DX, G_DG)
    c_specs, x_specs = _z_specs(groups, LANES)
    out_specs, out_shapes = _mix_out(batch, LANES)
    vec2 = pl.BlockSpec((2, LANES), lambda b, p: (0, p))
    wspec = pl.BlockSpec((2, LANES, LANES), lambda b, p: (0, p, p))
    return pl.pallas_call(
        _rglru_kernel,
        grid=(batch, SLABS),
        in_specs=c_specs + x_specs + [
            pl.BlockSpec((4, LANES), lambda b, p: (0, p)),
            pl.BlockSpec((1, LANES), lambda b, p: (0, p)),
            wspec, vec2, wspec, vec2, vec2,
        ],
        out_specs=out_specs,
        out_shape=out_shapes,
        scratch_shapes=[pltpu.VMEM((SEQ + 2 * CONV_PAD, LANES), F32),
                        pltpu.VMEM((CTX_LEN + SEQ, LANES), F32),
                        pltpu.VMEM((CTX_LEN + SEQ, LANES), F32)],
        compiler_params=_cparams(("parallel", "parallel")),
        name="rglru",
    )(z_c, z_c, z_x, z_x, conv_w, conv_b, w_r, b_r, w_i, b_i, lam)


def _block_diag_dense(w):
    two, nb, bs, _ = w.shape
    eye = jnp.eye(nb, dtype=w.dtype)
    return jnp.einsum("dgio,gh->dgiho", w, eye).reshape(two, nb * bs, nb * bs)


def kernel(x, c, ctx, c_ctx, w_ada, b_ada, norm_pre, norm_post, ffn_w_in, ffn_w_out, w_in, w_out,
           lb_logits, a_norm, b_norm, c_lambda, c_norm, d_conv_w, d_conv_b, d_w_r, d_b_r, d_w_i,
           d_b_i, d_lambda):
    batch = x.shape[0]
    assert x.shape == (batch, SEQ, D_MODEL) and ctx.shape == (batch, CTX_LEN, D_MODEL)
    assert batch < MOD_ROWS

    gla_consts = _gla_constants()
    ret_consts = _retention_constants()
    rope = _rope_constants()

    c_rows = jnp.zeros((MOD_ROWS, D_MODEL), F32).at[:batch].set(c).at[batch].set(c_ctx)
    mod = _ada_table(c_rows, w_ada, b_ada).reshape(DEPTH * MOD_ROWS, N_MOD, D_MODEL)

    n_c = batch * CTX_LEN
    tm_ffn, tm_proj, tm_out = 1024, 1024, 512
    tm_c = min(n_c, 1024)
    assert n_c % tm_c == 0
    row_x = lambda tm: (lambda i: i // (SEQ // tm))
    row_c = lambda i: batch

    hx = x.reshape(batch * SEQ, D_MODEL)
    hc = ctx.reshape(n_c, D_MODEL)
    tile2 = lambda g: jnp.tile(g.reshape(1, -1), (1, LANES // g.shape[-1]))
    w_out_b = w_out.astype(BF16)

    def ffn(h, l, idx, is_ctx, weights):
        return _ffn(h, mod, norm_pre[l, 2 * idx].reshape(1, D_MODEL), norm_post[l, 2 * idx].reshape(1, D_MODEL),
                    weights, layer=l, idx=idx, k0=6 * idx, emit=is_ctx,
                    row_of_tile=row_c if is_ctx else row_x(tm_ffn), tm=tm_c if is_ctx else tm_ffn)

    for l in range(DEPTH):
        ctx_out = l < DEPTH - 1
        gpre1 = norm_pre[l, 1].reshape(1, D_MODEL)
        gpost1 = norm_post[l, 1].reshape(1, D_MODEL)

        hc, ffn_w = ffn(hc, l, 0, True, (ffn_w_in, ffn_w_out))
        hx = ffn(hx, l, 0, False, ffn_w)

        z_c, w_in_l = _inproj(hc, mod, gpre1, w_in, layer=l, row_of_tile=row_c, tm=tm_c, emit=True)
        z_x = _inproj(hx, mod, gpre1, w_in_l, layer=l, row_of_tile=row_x(tm_proj), tm=tm_proj, emit=False)

        a_c, a_x = _hgrn2(z_c, z_x, lb_logits, tile2(a_norm[l]), gla_consts, layer=l, batch=batch, ctx_out=ctx_out)
        b_c, b_x = _retention(z_c, z_x, tile2(b_norm[l]), rope, ret_consts, batch=batch, ctx_out=ctx_out)

        lam_init = 0.8 - 0.6 * math.exp(-0.3 * l)
        cgain = c_norm[l].reshape(1, LANES)
        c_x = _diff_attn(z_x, z_c, z_x, rope, c_lambda[l], cgain, lam_init=lam_init, batch=batch, tq=256)

        d_c, d_x = _rglru(z_c, z_x, d_conv_w[l], d_conv_b[l].reshape(1, -1),
                          _block_diag_dense(d_w_r[l]).astype(BF16), d_b_r[l],
                          _block_diag_dense(d_w_i[l]).astype(BF16), d_b_i[l], d_lambda[l], batch=batch)

        hx = _outproj((a_x, b_x, c_x, d_x), w_out_b[l], hx, mod, gpost1, layer=l,
                      row_of_tile=row_x(tm_out), tm=tm_out)

        if ctx_out:
            c_c = _diff_attn(z_c, z_c, None, rope, c_lambda[l], cgain, lam_init=lam_init, batch=batch, tq=CTX_LEN)
            hc = _outproj((a_c, b_c, c_c, d_c), w_out_b[l], hc, mod, gpost1, layer=l,
                          row_of_tile=row_c, tm=min(tm_c, tm_out))
            hc, ffn_w = ffn(hc, l, 1, True, (ffn_w_in, ffn_w_out))
        else:
            ffn_w = _cast_ffn_weights(ffn_w_in, ffn_w_out, layer=l, idx=1)
        hx = ffn(hx, l, 1, False, ffn_w)

    return hx.reshape(batch, SEQ, D_MODEL)
```

```python
import functools
import math

import numpy as np
import jax
import jax.numpy as jnp
from jax import lax
from jax.experimental import pallas as pl
from jax.experimental.pallas import tpu as pltpu

F32 = jnp.float32
BF16 = jnp.bfloat16

D_MODEL = 2048
SEQ = 2048
CTX_LEN = 256
DEPTH = 2
GRID_W = 64
HEAD_DIM = 64
GROUP_WIDTH = 512
IN_WIDTH = 7168
D_FF = 5504
N_MOD = 9
ROPE_BASE = 10000.0
EPS = 1e-6
RG_C = 8.0
FFN_RESIDUAL = 0.5

LANES = 128
FF_TILE = 512
FF_TILE_EMIT = 256
MOD_ROWS = 8
GLA_CHUNK = 128
HGRN2_WIDTH = 512
RET_WIDTH = 512
GLA_LEVELS = 7
GLA_COARSE_LEVELS = 4
SCAN_CHUNK = 256
VMEM_LIMIT = 56 * 1024 * 1024

G_AQ, G_AFF, G_AFB, G_AV, G_AG = 0, 1, 2, 3, 4
G_BQ, G_BK, G_BV, G_BG = 5, 6, 7, 8
G_CQ, G_CK, G_CV = 9, 10, 11
G_DX, G_DG = 12, 13
SLABS = GROUP_WIDTH // LANES


def _cparams(sem):
    return pltpu.CompilerParams(dimension_semantics=sem, vmem_limit_bytes=VMEM_LIMIT)


def _sigmoid(t):
    return jax.nn.sigmoid(t)


def _silu(t):
    return t * _sigmoid(t)


def _rms(t, gain):
    return t * lax.rsqrt(jnp.mean(t * t, axis=-1, keepdims=True) + EPS) * gain


def _dot(a, b):
    return jnp.dot(a, b, preferred_element_type=F32)


def _dot_nt(a, b):
    return lax.dot_general(a, b, (((1,), (1,)), ((), ())), preferred_element_type=F32)


def _ada_kernel(c_ref, w_ref, b_ref, o_ref):
    s = _silu(c_ref[...])
    s_hi = s.astype(BF16).astype(F32)
    s_mid = (s - s_hi).astype(BF16).astype(F32)
    s_lo = s - s_hi - s_mid
    w = w_ref[...]
    w_hi = w.astype(BF16)
    w_lo = (w - w_hi.astype(F32)).astype(BF16)
    r_hi = _dot(jnp.concatenate([s_hi, s_mid, s_lo], axis=0).astype(BF16), w_hi)
    r_lo = _dot(jnp.concatenate([s_hi, s_mid], axis=0).astype(BF16), w_lo)
    m = MOD_ROWS
    o_ref[...] = (r_hi[0:m] + r_hi[m:2 * m] + r_hi[2 * m:3 * m]) + (r_lo[0:m] + r_lo[m:2 * m]) + b_ref[...]


def _ada_table(c_rows, w_ada, b_ada):
    tn = 1024
    n = N_MOD * D_MODEL
    return pl.pallas_call(
        _ada_kernel,
        grid=(DEPTH, n // tn),
        in_specs=[
            pl.BlockSpec((MOD_ROWS, D_MODEL), lambda l, j: (0, 0)),
            pl.BlockSpec((None, D_MODEL, tn), lambda l, j: (l, 0, j)),
            pl.BlockSpec((None, 1, tn), lambda l, j: (l, 0, j)),
        ],
        out_specs=pl.BlockSpec((None, MOD_ROWS, tn), lambda l, j: (l, 0, j)),
        out_shape=jax.ShapeDtypeStruct((DEPTH, MOD_ROWS, n), F32),
        compiler_params=_cparams(("arbitrary", "arbitrary")),
        name="ada_table",
    )(c_rows, w_ada, b_ada.reshape(DEPTH, 1, n))


ROW_CHUNK = 256
MM_ROWS = 512
MM_COLS = 512
OUT_ROWS = 256


def _norm_modulate_to(u_scr, h_ref, gain_ref, mod_ref, k_shift, k_scale):
    tm = h_ref.shape[0]
    shift = mod_ref[k_shift:k_shift + 1, :]
    gain_scale = gain_ref[...] * (1.0 + mod_ref[k_scale:k_scale + 1, :])

    def body(r, carry):
        r0 = pl.multiple_of(r * ROW_CHUNK, ROW_CHUNK)
        h = h_ref[pl.ds(r0, ROW_CHUNK), :]
        u_scr[pl.ds(r0, ROW_CHUNK), :] = (_rms(h, gain_scale) + shift).astype(BF16)
        return carry

    lax.fori_loop(0, tm // ROW_CHUNK, body, 0)


def _cast_ffn_tiles(w_refs, out_refs, j, last):
    tf = w_refs[0].shape[1]
    overlap = pl.cdiv(D_FF, tf) * tf - D_FF
    tiles = [r[...].astype(BF16) for r in w_refs]
    for t, out, axis in zip(tiles, out_refs, (1, 1, 0)):
        if overlap:
            keep = lax.slice_in_dim(t, overlap, tf, axis=axis)
            fill = jnp.zeros_like(lax.slice_in_dim(t, 0, overlap, axis=axis))
            t = jnp.where(j == last, jnp.concatenate([keep, fill], axis=axis), t)
        out[...] = t
    return tiles


def _ffn_kernel(h_ref, mod_ref, gpre_ref, gpost_ref, wg_ref, wu_ref, wout_ref, *rest, k0, emit):
    if emit:
        o_ref, wg_out, wu_out, wout_out, u_scr = rest
    else:
        o_ref, u_scr = rest
    j = pl.program_id(1)
    last = pl.num_programs(1) - 1
    tm = h_ref.shape[0]
    tf = wg_ref.shape[1]
    overlap = pl.cdiv(D_FF, tf) * tf - D_FF

    @pl.when(j == 0)
    def _():
        _norm_modulate_to(u_scr, h_ref, gpre_ref, mod_ref, k0, k0 + 1)
        o_ref[...] = jnp.zeros_like(o_ref)

    if emit:
        wg, wu, wout = _cast_ffn_tiles((wg_ref, wu_ref, wout_ref), (wg_out, wu_out, wout_out), j, last)
    else:
        wg, wu, wout = wg_ref[...], wu_ref[...], wout_ref[...]

    col = lax.broadcasted_iota(jnp.int32, (1, tf), 1)
    fresh = jnp.logical_or(j < last, col >= overlap)
    for r in range(tm // MM_ROWS):
        rows = slice(r * MM_ROWS, (r + 1) * MM_ROWS)
        u = u_scr[rows, :]
        act = jnp.where(fresh, _silu(_dot(u, wg)) * _dot(u, wu), 0.0).astype(BF16)
        for n0 in range(0, D_MODEL, MM_COLS):
            o_ref[rows, n0:n0 + MM_COLS] += _dot(act, wout[:, n0:n0 + MM_COLS])

    @pl.when(j == last)
    def _():
        gate_gain = FFN_RESIDUAL * mod_ref[k0 + 2:k0 + 3, :] * gpost_ref[...]

        def body(r, carry):
            r0 = pl.multiple_of(r * ROW_CHUNK, ROW_CHUNK)
            y = o_ref[pl.ds(r0, ROW_CHUNK), :]
            o_ref[pl.ds(r0, ROW_CHUNK), :] = h_ref[pl.ds(r0, ROW_CHUNK), :] + _rms(y, gate_gain)
            return carry

        lax.fori_loop(0, tm // ROW_CHUNK, body, 0)


def _mod_spec(layer, row_of_tile):
    return pl.BlockSpec((None, N_MOD, D_MODEL), lambda i, *_: (layer * MOD_ROWS + row_of_tile(i), 0, 0))


def _ff_offset(j, tf, base=0):
    return (base // LANES + jnp.minimum(j * (tf // LANES), (D_FF - tf) // LANES)) * LANES


def _ffn_f32_weight_specs(layer, idx, tf):
    el = pl.Element
    in_specs = [
        pl.BlockSpec((None, None, el(D_MODEL), el(tf)), lambda *g: (layer, idx, 0, _ff_offset(g[-1], tf))),
        pl.BlockSpec((None, None, el(D_MODEL), el(tf)), lambda *g: (layer, idx, 0, _ff_offset(g[-1], tf, D_FF))),
        pl.BlockSpec((None, None, el(tf), el(D_MODEL)), lambda *g: (layer, idx, _ff_offset(g[-1], tf), 0)),
    ]
    d_ff_pad = pl.cdiv(D_FF, tf) * tf
    cols_out = pl.BlockSpec((D_MODEL, tf), lambda *g: (0, g[-1]))
    out_specs = [cols_out, cols_out, pl.BlockSpec((tf, D_MODEL), lambda *g: (g[-1], 0))]
    out_shape = [jax.ShapeDtypeStruct((D_MODEL, d_ff_pad), BF16), jax.ShapeDtypeStruct((D_MODEL, d_ff_pad), BF16),
                 jax.ShapeDtypeStruct((d_ff_pad, D_MODEL), BF16)]
    return in_specs, out_specs, out_shape


def _ffn_weight_cast_kernel(wg_ref, wu_ref, wout_ref, wg_out, wu_out, wout_out):
    _cast_ffn_tiles((wg_ref, wu_ref, wout_ref), (wg_out, wu_out, wout_out), pl.program_id(0), pl.num_programs(0) - 1)


def _cast_ffn_weights(w_in, w_out, *, layer, idx):
    tf = FF_TILE
    in_specs, out_specs, out_shape = _ffn_f32_weight_specs(layer, idx, tf)
    return tuple(pl.pallas_call(
        _ffn_weight_cast_kernel,
        grid=(pl.cdiv(D_FF, tf),),
        in_specs=in_specs,
        out_specs=out_specs,
        out_shape=out_shape,
        compiler_params=_cparams(("arbitrary",)),
        name="ffn_weight_cast",
    )(w_in, w_in, w_out))


def _ffn(h, mod, gpre, gpost, weights, *, layer, idx, k0, row_of_tile, tm, emit):
    n = h.shape[0]
    el = pl.Element
    tf = FF_TILE_EMIT if emit else FF_TILE
    col_tile = pl.BlockSpec((el(D_MODEL), el(tf)), lambda i, j: (0, _ff_offset(j, tf)))
    row_tile = pl.BlockSpec((el(tf), el(D_MODEL)), lambda i, j: (_ff_offset(j, tf), 0))
    h_spec = pl.BlockSpec((tm, D_MODEL), lambda i, j: (i, 0))
    out_specs, out_shape = h_spec, jax.ShapeDtypeStruct((n, D_MODEL), F32)
    if emit:
        assert n == tm, "weight tiles are cast once: one row tile only"
        w_in, w_out = weights
        w_args = (w_in, w_in, w_out)
        w_specs, w_out_specs, w_out_shape = _ffn_f32_weight_specs(layer, idx, tf)
        out_specs = [h_spec] + w_out_specs
        out_shape = [out_shape] + w_out_shape
    else:
        w_args = weights
        w_specs = [col_tile, col_tile, row_tile]
    out = pl.pallas_call(
        functools.partial(_ffn_kernel, k0=k0, emit=emit),
        grid=(n // tm, pl.cdiv(D_FF, tf)),
        in_specs=[
            h_spec,
            _mod_spec(layer, row_of_tile),
            pl.BlockSpec((1, D_MODEL), lambda i, j: (0, 0)),
            pl.BlockSpec((1, D_MODEL), lambda i, j: (0, 0)),
        ] + w_specs,
        out_specs=out_specs,
        out_shape=out_shape,
        scratch_shapes=[pltpu.VMEM((tm, D_MODEL), BF16)],
        compiler_params=_cparams(("parallel", "arbitrary")),
        name="ffn_cast" if emit else "ffn",
    )(h, mod, gpre, gpost, *w_args)
    return (out[0], tuple(out[1:])) if emit else out


def _inproj_kernel(h_ref, mod_ref, gpre_ref, w_ref, *rest, emit):
    if emit:
        z_ref, w_out, u_scr = rest
    else:
        z_ref, u_scr = rest
    j = pl.program_id(1)
    tm = h_ref.shape[0]

    @pl.when(j == 0)
    def _():
        _norm_modulate_to(u_scr, h_ref, gpre_ref, mod_ref, 3, 4)

    w = w_ref[...]
    if emit:
        w = w.astype(BF16)
        w_out[...] = w
    for r in range(tm // MM_ROWS):
        rows = slice(r * MM_ROWS, (r + 1) * MM_ROWS)
        z_ref[rows, :] = _dot(u_scr[rows, :], w)


def _inproj(h, mod, gpre, w, *, layer, row_of_tile, tm, emit):
    n = h.shape[0]
    tn = 512 if emit else 1024
    if emit:
        assert n == tm, "weight tiles are cast once: one row tile only"
        w_spec = pl.BlockSpec((None, D_MODEL, tn), lambda i, j: (layer, 0, j))
    else:
        w_spec = pl.BlockSpec((D_MODEL, tn), lambda i, j: (0, j))
    z_spec = pl.BlockSpec((tm, tn), lambda i, j: (i, j))
    z_shape = jax.ShapeDtypeStruct((n, IN_WIDTH), F32)
    out = pl.pallas_call(
        functools.partial(_inproj_kernel, emit=emit),
        grid=(n // tm, IN_WIDTH // tn),
        in_specs=[
            pl.BlockSpec((tm, D_MODEL), lambda i, j: (i, 0)),
            _mod_spec(layer, row_of_tile),
            pl.BlockSpec((1, D_MODEL), lambda i, j: (0, 0)),
            w_spec,
        ],
        out_specs=[z_spec, pl.BlockSpec((D_MODEL, tn), lambda i, j: (0, j))] if emit else z_spec,
        out_shape=[z_shape, jax.ShapeDtypeStruct((D_MODEL, IN_WIDTH), BF16)] if emit else z_shape,
        scratch_shapes=[pltpu.VMEM((tm, D_MODEL), BF16)],
        compiler_params=_cparams(("parallel", "arbitrary")),
        name="inproj_cast" if emit else "inproj",
    )(h, mod, gpre, w)
    return tuple(out) if emit else out


def _outproj_kernel(a_ref, b_ref, c_ref, d_ref, w_ref, h_ref, mod_ref, gpost_ref, o_ref):
    tm = h_ref.shape[0]
    gate_gain = mod_ref[5:6, :] * gpost_ref[...]
    for r in range(tm // OUT_ROWS):
        rows = slice(r * OUT_ROWS, (r + 1) * OUT_ROWS)
        y = _dot(a_ref[rows, :], w_ref[0 * GROUP_WIDTH:1 * GROUP_WIDTH, :])
        y += _dot(b_ref[rows, :], w_ref[1 * GROUP_WIDTH:2 * GROUP_WIDTH, :])
        y += _dot(c_ref[rows, :], w_ref[2 * GROUP_WIDTH:3 * GROUP_WIDTH, :])
        y += _dot(d_ref[rows, :], w_ref[3 * GROUP_WIDTH:4 * GROUP_WIDTH, :])
        o_ref[rows, :] = h_ref[rows, :] + _rms(y, gate_gain)


def _outproj(mix, w, h, mod, gpost, *, layer, row_of_tile, tm):
    n = h.shape[0]
    mix_spec = pl.BlockSpec((tm, GROUP_WIDTH), lambda i: (i, 0))
    return pl.pallas_call(
        _outproj_kernel,
        grid=(n // tm,),
        in_specs=[
            mix_spec, mix_spec, mix_spec, mix_spec,
            pl.BlockSpec((D_MODEL, D_MODEL), lambda i: (0, 0)),
            pl.BlockSpec((tm, D_MODEL), lambda i: (i, 0)),
            _mod_spec(layer, row_of_tile),
            pl.BlockSpec((1, D_MODEL), lambda i: (0, 0)),
        ],
        out_specs=pl.BlockSpec((tm, D_MODEL), lambda i: (i, 0)),
        out_shape=jax.ShapeDtypeStruct((n, D_MODEL), F32),
        compiler_params=_cparams(("parallel",)),
        name="outproj",
    )(*mix, w, h, mod, gpost)


def _gla_constants():
    c = GLA_CHUNK
    idx = np.arange(c)
    i, r = idx[:, None], idx[None, :]
    mats = np.zeros((2, GLA_LEVELS + 2, c, c), np.float32)
    masks = np.zeros((2, GLA_LEVELS + 1, c, c), np.float32)
    mats[0, 0] = r <= i
    mats[0, 1] = r > i
    mats[1, 0] = r >= i
    mats[1, 1] = r < i
    for m in range(GLA_LEVELS):
        hs = c >> (m + 1)
        start = (idx // (2 * hs)) * (2 * hs)
        mid = (start + hs - 1)[:, None]
        right = ((idx // hs) % 2 == 1)[:, None]
        mats[0, 2 + m] = np.where(right, (r > mid) & (r <= i), (r > i) & (r <= mid))
        mats[1, 2 + m] = np.where(right, (r > mid) & (r < i), (r >= i) & (r <= mid))
        same = (start[:, None] == start[None, :])
        q_right = right & ~right.T & same
        masks[0, m] = q_right
        masks[1, m] = q_right.T
    masks[0, GLA_LEVELS] = np.eye(c)
    masks[1, GLA_LEVELS] = np.eye(c)
    mats = mats.reshape(2, (GLA_LEVELS + 2) * c, c)
    mats = np.concatenate([mats, mats], axis=-1)
    masks = np.tile(masks, (1, 1, 1, 2))
    masks_q = np.stack([np.stack([masks[d, m][_gla_query_rows(m, d)] for m in range(GLA_COARSE_LEVELS)])
                        for d in range(2)])
    return jnp.asarray(mats, BF16), jnp.asarray(masks, F32), jnp.asarray(masks_q, F32)


def _gla_query_rows(level, d):
    hs = GLA_CHUNK >> (level + 1)
    idx = np.arange(GLA_CHUNK)
    return idx[(idx // hs) % 2 == (1 if d == 0 else 0)]


def _retention_constants():
    c = GLA_CHUNK
    heads = GROUP_WIDTH // HEAD_DIM
    log_decay = np.log1p(-np.exp2(-5.0 - np.arange(heads, dtype=np.float64)))
    ld_lane = np.repeat(log_decay, HEAD_DIM).reshape(SLABS, 1, LANES)
    idx = np.arange(c, dtype=np.float64)
    i, j = idx[:, None], idx[None, :]
    dq = np.zeros((SLABS, 2, c, LANES))
    dk = np.zeros((SLABS, 2, c, LANES))
    dq[:, 0] = np.exp((idx[None, :, None] + 1.0) * ld_lane)
    dk[:, 0] = np.exp((c - 1.0 - idx[None, :, None]) * ld_lane)
    dq[:, 1] = np.exp((c - idx[None, :, None]) * ld_lane)
    dk[:, 1] = np.exp(idx[None, :, None] * ld_lane)
    gt = np.exp(c * ld_lane)
    dm = np.zeros((SLABS, 2, 2, c, c))
    for p in range(SLABS):
        for hh in range(2):
            ld = log_decay[2 * p + hh]
            dm[p, 0, hh] = np.where(i >= j, np.exp((i - j) * ld), 0.0)
            dm[p, 1, hh] = np.where(j >= i, np.exp((j - i) * ld), 0.0)
    dm = dm.transpose(0, 1, 3, 2, 4).reshape(SLABS, 2, c, 2 * c)
    f = lambda t: jnp.asarray(t.astype(np.float32))
    return f(dq), f(dk), f(gt), f(dm)


def _rope_constants():
    quarter = HEAD_DIM // 4
    inv_freq = ROPE_BASE ** (-jnp.arange(quarter, dtype=F32) / quarter)
    rows = SEQ // GRID_W
    row = jnp.repeat(jnp.arange(rows, dtype=F32), GRID_W)
    col = jnp.tile(jnp.arange(GRID_W, dtype=F32), rows)
    ang = jnp.concatenate([row[:, None] * inv_freq, col[:, None] * inv_freq], axis=-1)
    cos, sin = jnp.cos(ang), jnp.sin(ang)
    reps = LANES // HEAD_DIM
    return (jnp.tile(jnp.concatenate([cos, cos], axis=-1), (1, reps)),
            jnp.tile(jnp.concatenate([-sin, sin], axis=-1), (1, reps)))


def _rope(t, cos, sin):
    lane = lax.broadcasted_iota(jnp.int32, t.shape, 1)
    first_half = (lane % HEAD_DIM) < (HEAD_DIM // 2)
    partner = jnp.where(first_half,
                        pltpu.roll(t, LANES - HEAD_DIM // 2, axis=1),
                        pltpu.roll(t, HEAD_DIM // 2, axis=1))
    return t * cos + partner * sin


def _head_lane_masks():
    lane = lax.broadcasted_iota(jnp.int32, (1, LANES), 1)
    return lane < HEAD_DIM, lane >= HEAD_DIM


def _stack_heads(t):
    hm0, hm1 = _head_lane_masks()
    return jnp.concatenate([jnp.where(hm0, t, 0.0).astype(BF16),
                            jnp.where(hm1, t, 0.0).astype(BF16)], axis=0)


def _gla_state_update(ke, v, gtot, st):
    upd = lax.dot_general(v.astype(BF16), ke.astype(BF16), (((0,), (0,)), ((), ())),
                          preferred_element_type=F32)
    row_head = lax.broadcasted_iota(jnp.int32, (LANES, LANES), 0) // HEAD_DIM
    col_head = lax.broadcasted_iota(jnp.int32, (LANES, LANES), 1) // HEAD_DIM
    return st * gtot + jnp.where(row_head == col_head, upd, 0.0)


def _gla_chunk_update(qe, ke, v, gtot, s_cat, st):
    o = _dot_nt(qe.astype(BF16), st.astype(BF16))
    o += _dot(s_cat.astype(BF16), _stack_heads(v))
    return o, _gla_state_update(ke, v, gtot, st)


def _slab(t, s):
    return t[:, s * LANES:(s + 1) * LANES]


def _gla_finish_chunk(acc_ref, zg_ref, gain, o_ref, r0):
    hm0, _ = _head_lane_masks()
    for s in range(acc_ref.shape[1] // LANES):
        cols = slice(s * LANES, (s + 1) * LANES)
        o = acc_ref[pl.ds(r0, GLA_CHUNK), cols]
        sq = o * o
        ms0 = jnp.sum(jnp.where(hm0, sq, 0.0), axis=1, keepdims=True) * (1.0 / HEAD_DIM)
        ms1 = jnp.sum(jnp.where(hm0, 0.0, sq), axis=1, keepdims=True) * (1.0 / HEAD_DIM)
        inv = jnp.where(hm0, lax.rsqrt(ms0 + EPS), lax.rsqrt(ms1 + EPS))
        g = zg_ref[pl.ds(r0, GLA_CHUNK), cols]
        o_ref[pl.ds(r0, GLA_CHUNK), cols] = (o * inv * gain * _silu(g)).astype(o_ref.dtype)


def _gla_scan(chunk_fn, accs, zgs, gain, outs, ctx_out):
    n_slabs = accs[1].shape[1] // LANES
    if ctx_out:
        accs[0][...] = jnp.zeros_like(accs[0])
    else:
        outs[0][...] = jnp.zeros_like(outs[0])
    accs[1][...] = jnp.zeros_like(accs[1])

    def run(seg, states):
        acc_ref = accs[seg]
        n = acc_ref.shape[0] // GLA_CHUNK
        assert n % 2 == 0
        need_out = seg == 1 or ctx_out

        def body(finish):
            def step(t, states):
                new_states, visited = [], []
                for d in (0, 1):
                    r0 = pl.multiple_of((t if d == 0 else n - 1 - t) * GLA_CHUNK, GLA_CHUNK)
                    o, st = chunk_fn(d, seg, r0, states[d], need_out)
                    if need_out:
                        acc_ref[pl.ds(r0, GLA_CHUNK), :] += o
                        visited.append(r0)
                    new_states.append(st)
                if finish:
                    for r0 in visited:
                        _gla_finish_chunk(acc_ref, zgs[seg], gain, outs[seg], r0)
                return tuple(new_states)
            return step

        states = lax.fori_loop(0, n // 2, body(False), states)
        return lax.fori_loop(n // 2, n, body(True), states)

    zero = tuple(jnp.zeros((LANES, LANES), F32) for _ in range(n_slabs))
    run(1, run(0, (zero, zero)))


def _hgrn2_kernel(zq_c, zff_c, zfb_c, zv_c, zg_c, zq_x, zff_x, zfb_x, zv_x, zg_x,
                  lbl_ref, gain_ref, mats_ref, masks_ref, masks_q_ref, oc_ref, ox_ref, acc_c, acc_x,
                  *, layer, ctx_out):
    c = GLA_CHUNK
    n_slabs = acc_c.shape[1] // LANES
    lbl = lbl_ref[...]
    e = jnp.exp(lbl - jnp.max(lbl, axis=0, keepdims=True))
    sm = e / jnp.sum(e, axis=0, keepdims=True)
    lower = jnp.zeros_like(sm[0])
    for l in range(1, layer + 1):
        lower = lower + sm[l]
    zq = (zq_c, zq_x)
    zf = ((zff_c, zff_x), (zfb_c, zfb_x))
    zv = (zv_c, zv_x)
    row = lax.broadcasted_iota(jnp.int32, (c, 1), 0)

    def chunk_fn(d, seg, r0, states, need_out):
        lb = lower[d:d + 1, :]
        f = lb + (1.0 - lb) * _sigmoid(zf[d][seg][pl.ds(r0, c), :])
        lf = jnp.log(f)
        k = 1.0 - f
        v = zv[seg][pl.ds(r0, c), :]
        lf_hi = lf.astype(BF16)
        lf_lo = (lf - lf_hi.astype(F32)).astype(BF16)
        lf_cat = jnp.concatenate([lf_hi, lf_lo], axis=0)
        gtot = jnp.exp(jnp.sum(lf, axis=0, keepdims=True))
        if not need_out:
            ke = k * jnp.exp(_dot(mats_ref[d, c:2 * c, :], lf_cat))
            return None, tuple(_gla_state_update(_slab(ke, s), _slab(v, s), _slab(gtot, s), states[s])
                               for s in range(n_slabs))
        q = _silu(zq[seg][pl.ds(r0, c), :])
        ex = jnp.exp(_dot(mats_ref[d], lf_cat))
        qe = q * ex[0:c]
        ke = k * ex[c:2 * c]
        s_cats = [jnp.zeros((c, 2 * c), F32) for _ in range(n_slabs)]
        for lvl in range(GLA_LEVELS + 1):
            if lvl < GLA_LEVELS:
                hs = c >> (lvl + 1)
                q_side = ((row // hs) % 2) == (1 if d == 0 else 0)
                rhs_src = jnp.where(q_side, q, k) * ex[(2 + lvl) * c:(3 + lvl) * c]
                lhs = rhs_src.astype(BF16)
            else:
                lhs, rhs_src = q.astype(BF16), k
            if lvl < GLA_COARSE_LEVELS:
                rows_q = _gla_query_rows(lvl, d)
                starts = rows_q[::hs]
                lhs_q = jnp.concatenate([rhs_src[int(r):int(r) + hs] for r in starts], axis=0).astype(BF16)
                gap = jnp.zeros((hs, 2 * c), F32)
                for s in range(n_slabs):
                    p = masks_q_ref[d, lvl] * _dot_nt(_slab(lhs_q, s), _stack_heads(_slab(rhs_src, s)))
                    pieces = []
                    for b in range(len(starts)):
                        blk = p[b * hs:(b + 1) * hs]
                        pieces += [gap, blk] if d == 0 else [blk, gap]
                    s_cats[s] = s_cats[s] + jnp.concatenate(pieces, axis=0)
                continue
            mask = masks_ref[d, lvl]
            for s in range(n_slabs):
                s_cats[s] = s_cats[s] + mask * _dot_nt(_slab(lhs, s), _stack_heads(_slab(rhs_src, s)))
        outs = [_gla_chunk_update(_slab(qe, s), _slab(ke, s), _slab(v, s), _slab(gtot, s), s_cats[s],
                                  states[s]) for s in range(n_slabs)]
        return jnp.concatenate([o for o, _ in outs], axis=1), tuple(st for _, st in outs)

    _gla_scan(chunk_fn, (acc_c, acc_x), (zg_c, zg_x), gain_ref[...], (oc_ref, ox_ref), ctx_out)


def _z_specs(groups, width, single_buffered=()):
    per_group = GROUP_WIDTH // width
    mode = lambda g: dict(pipeline_mode=pl.Buffered(1)) if g in single_buffered else {}
    c_specs = [pl.BlockSpec((CTX_LEN, width), lambda b, p, g=g: (b, g * per_group + p)) for g in groups]
    x_specs = [pl.BlockSpec((SEQ, width), lambda b, p, g=g: (b, g * per_group + p), **mode(g)) for g in groups]
    return c_specs, x_specs


def _mix_out(batch, width):
    specs = [pl.BlockSpec((CTX_LEN, width), lambda b, p: (b, p)),
             pl.BlockSpec((SEQ, width), lambda b, p: (b, p))]
    shapes = [jax.ShapeDtypeStruct((batch * CTX_LEN, GROUP_WIDTH), BF16),
              jax.ShapeDtypeStruct((batch * SEQ, GROUP_WIDTH), BF16)]
    return specs, shapes


def _gla_scratch(width):
    return [pltpu.VMEM((CTX_LEN, width), F32), pltpu.VMEM((SEQ, width), F32)]


def _hgrn2(z_c, z_x, lb_logits, gain, consts, *, layer, batch, ctx_out):
    mats, masks, masks_q = consts
    groups = (G_AQ, G_AFF, G_AFB, G_AV, G_AG)
    width = HGRN2_WIDTH
    c_specs, x_specs = _z_specs(groups, width, single_buffered=(G_AFB, G_AV, G_AG))
    out_specs, out_shapes = _mix_out(batch, width)
    n_c = len(groups)
    return pl.pallas_call(
        functools.partial(_hgrn2_kernel, layer=layer, ctx_out=ctx_out),
        grid=(batch, GROUP_WIDTH // width),
        in_specs=c_specs + x_specs + [
            pl.BlockSpec((DEPTH, 2, width), lambda b, p: (0, 0, p)),
            pl.BlockSpec((1, LANES), lambda b, p: (0, 0)),
            pl.BlockSpec(mats.shape, lambda b, p: (0, 0, 0)),
            pl.BlockSpec(masks.shape, lambda b, p: (0, 0, 0, 0)),
            pl.BlockSpec(masks_q.shape, lambda b, p: (0, 0, 0, 0)),
        ],
        out_specs=out_specs,
        out_shape=out_shapes,
        scratch_shapes=_gla_scratch(width),
        compiler_params=_cparams(("parallel", "parallel")),
        name="hgrn2",
    )(*([z_c] * n_c), *([z_x] * n_c), lb_logits, gain, mats, masks, masks_q)


def _retention_kernel(zq_c, zk_c, zv_c, zg_c, zq_x, zk_x, zv_x, zg_x, cos_ref, sin_ref,
                      dq_ref, dk_ref, gt_ref, dm_ref, gain_ref, oc_ref, ox_ref, acc_c, acc_x, *, ctx_out):
    c = GLA_CHUNK
    n_slabs = acc_c.shape[1] // LANES
    zq, zk, zv = (zq_c, zq_x), (zk_c, zk_x), (zv_c, zv_x)

    def chunk_fn(d, seg, r0, states, need_out):
        def rotated(t):
            return _rope(t, cos_ref[pl.ds(r0, c), :], sin_ref[pl.ds(r0, c), :]) if seg == 1 else t

        outs = []
        for s in range(n_slabs):
            cols = slice(s * LANES, (s + 1) * LANES)
            k = rotated(zk[seg][pl.ds(r0, c), cols] * (HEAD_DIM ** -0.5))
            v = zv[seg][pl.ds(r0, c), cols]
            if not need_out:
                outs.append((None, _gla_state_update(k * dk_ref[s, d], v, gt_ref[s], states[s])))
                continue
            q = rotated(zq[seg][pl.ds(r0, c), cols])
            s_cat = dm_ref[s, d] * _dot_nt(q.astype(BF16), _stack_heads(k))
            outs.append(_gla_chunk_update(q * dq_ref[s, d], k * dk_ref[s, d], v, gt_ref[s], s_cat, states[s]))
        o = jnp.concatenate([o for o, _ in outs], axis=1) if need_out else None
        return o, tuple(st for _, st in outs)

    _gla_scan(chunk_fn, (acc_c, acc_x), (zg_c, zg_x), gain_ref[...], (oc_ref, ox_ref), ctx_out)


def _retention(z_c, z_x, gain, rope, consts, *, batch, ctx_out):
    dq, dk, gt, dm = consts
    cos, sin = rope
    groups = (G_BQ, G_BK, G_BV, G_BG)
    width = RET_WIDTH
    c_specs, x_specs = _z_specs(groups, width)
    out_specs, out_shapes = _mix_out(batch, width)
    n_c = len(groups)
    c = GLA_CHUNK
    ns = width // LANES
    return pl.pallas_call(
        functools.partial(_retention_kernel, ctx_out=ctx_out),
        grid=(batch, GROUP_WIDTH // width),
        in_specs=c_specs + x_specs + [
            pl.BlockSpec((SEQ, LANES), lambda b, p: (0, 0)),
            pl.BlockSpec((SEQ, LANES), lambda b, p: (0, 0)),
            pl.BlockSpec((ns, 2, c, LANES), lambda b, p: (p, 0, 0, 0)),
            pl.BlockSpec((ns, 2, c, LANES), lambda b, p: (p, 0, 0, 0)),
            pl.BlockSpec((ns, 1, LANES), lambda b, p: (p, 0, 0)),
            pl.BlockSpec((ns, 2, c, 2 * c), lambda b, p: (p, 0, 0, 0)),
            pl.BlockSpec((1, LANES), lambda b, p: (0, 0)),
        ],
        out_specs=out_specs,
        out_shape=out_shapes,
        scratch_shapes=_gla_scratch(width),
        compiler_params=_cparams(("parallel", "parallel")),
        name="retention",
    )(*([z_c] * n_c), *([z_x] * n_c), cos, sin, dq, dk, gt, dm, gain)


ATTN_FILL_ROWS = 256
ATTN_WIDTH = 512
LOG2_E = 1.4426950408889634


def _diff_attn_kernel(*refs, lam_init, with_latent):
    if with_latent:
        q_ref, kc_ref, vc_ref, kx_ref, vx_ref, cos_ref, sin_ref, lam_ref, gain_ref, o_ref, kk, vv = refs
    else:
        q_ref, kc_ref, vc_ref, lam_ref, gain_ref, o_ref, kk, vv = refs
    tq, width = q_ref.shape
    n_heads = width // LANES

    @pl.when(pl.program_id(2) == 0)
    def _():
        kk[0:CTX_LEN, :] = kc_ref[...].astype(BF16)
        vv[0:CTX_LEN, :] = vc_ref[...].astype(BF16)
        if with_latent:
            for r0 in range(0, SEQ, ATTN_FILL_ROWS):
                rows = slice(r0, r0 + ATTN_FILL_ROWS)
                dst = slice(CTX_LEN + r0, CTX_LEN + r0 + ATTN_FILL_ROWS)
                vv[dst, :] = vx_ref[rows, :].astype(BF16)
                for h in range(n_heads):
                    cols = slice(h * LANES, (h + 1) * LANES)
                    kk[dst, cols] = _rope(kx_ref[rows, cols], cos_ref[rows, :], sin_ref[rows, :]).astype(BF16)

    lv = lam_ref[...]
    lam = (jnp.exp(jnp.sum(lv[0:1] * lv[1:2], axis=1, keepdims=True))
           - jnp.exp(jnp.sum(lv[2:3] * lv[3:4], axis=1, keepdims=True)) + lam_init)
    lane = lax.broadcasted_iota(jnp.int32, (1, LANES), 1)
    gain = gain_ref[...]
    if with_latent:
        q0 = pl.multiple_of(pl.program_id(2) * tq, tq)
        cos, sin = cos_ref[pl.ds(q0, tq), :], sin_ref[pl.ds(q0, tq), :]

    for h in range(n_heads):
        cols = slice(h * LANES, (h + 1) * LANES)
        q = q_ref[:, cols]
        if with_latent:
            q = _rope(q, cos, sin)
        q = (q * (HEAD_DIM ** -0.5 * LOG2_E)).astype(BF16)
        keys = kk[:, cols]

        def weights(qh):
            s = _dot_nt(qh, keys)
            p = jnp.exp2(s - jnp.max(s, axis=-1, keepdims=True))
            return p, jnp.sum(p, axis=-1, keepdims=True)

        zero = jnp.zeros_like(q)
        p1, l1 = weights(jnp.where(lane < HEAD_DIM, q, zero))
        p2, l2 = weights(jnp.where(lane < HEAD_DIM, zero, q))
        a = p1 - (lam * l1 / l2) * p2
        o = _dot(a.astype(BF16), vv[:, cols]) / l1
        o_ref[:, cols] = (_rms(o, gain) * (1.0 - lam_init)).astype(o_ref.dtype)


def _diff_attn(z_q, z_c, z_x, rope, lam_vecs, gain, *, lam_init, batch, tq):
    with_latent = z_x is not None
    q_len = z_q.shape[0] // batch
    nq = q_len // tq
    width = ATTN_WIDTH
    per_group = GROUP_WIDTH // width
    n_keys = CTX_LEN + (SEQ if with_latent else 0)
    block = lambda rows, g: pl.BlockSpec((rows, width), lambda b, h, i, g=g: (b, g * per_group + h))
    qspec = pl.BlockSpec((tq, width), lambda b, h, i: (b * nq + i, G_CQ * per_group + h))
    table = pl.BlockSpec((SEQ, LANES), lambda b, h, i: (0, 0))
    in_specs = [qspec, block(CTX_LEN, G_CK), block(CTX_LEN, G_CV)]
    args = [z_q, z_c, z_c]
    if with_latent:
        in_specs += [block(SEQ, G_CK), block(SEQ, G_CV), table, table]
        args += [z_x, z_x, *rope]
    in_specs += [pl.BlockSpec((4, HEAD_DIM), lambda b, h, i: (0, 0)),
                 pl.BlockSpec((1, LANES), lambda b, h, i: (0, 0))]
    args += [lam_vecs, gain]
    return pl.pallas_call(
        functools.partial(_diff_attn_kernel, lam_init=lam_init, with_latent=with_latent),
        grid=(batch, per_group, nq),
        in_specs=in_specs,
        out_specs=pl.BlockSpec((tq, width), lambda b, h, i: (b * nq + i, h)),
        out_shape=jax.ShapeDtypeStruct((z_q.shape[0], GROUP_WIDTH), BF16),
        scratch_shapes=[pltpu.VMEM((n_keys, width), BF16), pltpu.VMEM((n_keys, width), BF16)],
        compiler_params=_cparams(("parallel", "parallel", "arbitrary")),
        name="diff_attn",
    )(*args)


CONV_PAD = 8
F32_TINY = float(np.finfo(np.float32).tiny)


def _gelu_tanh(t):
    return 0.5 * t * (1.0 + jnp.tanh(math.sqrt(2.0 / math.pi) * (t + 0.044715 * (t * t * t))))


def _softplus(t):
    return jnp.maximum(t, 0.0) + jnp.log1p(jnp.exp(-jnp.abs(t)))


def _rglru_kernel(x_c, g_c, x_x, g_x, cw_ref, cb_ref, wr_ref, br_ref, wi_ref, bi_ref, lam_ref,
                  oc_ref, ox_ref, pad, xc, hs):
    n_tok = CTX_LEN + SEQ
    sc = SCAN_CHUNK
    cw = cw_ref[...]
    cb = cb_ref[...]
    zeros = jnp.zeros((CONV_PAD, LANES), F32)

    for x_ref, base, n in ((x_c, 0, CTX_LEN), (x_x, CTX_LEN, SEQ)):
        pad[0:CONV_PAD, :] = zeros
        pad[CONV_PAD:CONV_PAD + n, :] = x_ref[...]
        pad[CONV_PAD + n:2 * CONV_PAD + n, :] = zeros
        for r in range(n // sc):
            t0 = CONV_PAD + r * sc
            acc = cb + cw[0:1] * pad[t0 - 1:t0 - 1 + sc, :]
            acc += cw[1:2] * pad[t0:t0 + sc, :]
            acc += cw[2:3] * pad[t0 + 1:t0 + 1 + sc, :]
            acc += cw[3:4] * pad[t0 + 2:t0 + 2 + sc, :]
            xc[base + r * sc:base + (r + 1) * sc, :] = acc

    row = lax.broadcasted_iota(jnp.int32, (sc, 1), 0)
    n_chunks = n_tok // sc
    neg_c_softplus = -RG_C * _softplus(-lam_ref[...])
    hs[...] = jnp.zeros_like(hs)

    def scan_chunk(d, r0, carry):
        x = xc[pl.ds(r0, sc), :]
        xb = x.astype(BF16)
        r = _sigmoid(_dot(xb, wr_ref[d]) + br_ref[d:d + 1, :])
        i = _sigmoid(_dot(xb, wi_ref[d]) + bi_ref[d:d + 1, :])
        log_a = neg_c_softplus[d:d + 1, :] * r
        a = jnp.exp(log_a)
        var = -jnp.tanh(log_a) * (a * a + 1.0)
        u = var * lax.rsqrt(jnp.maximum(var, F32_TINY)) * (i * x)
        s = 1
        while s < sc:
            ok, shift = (row >= s, s) if d == 0 else (row < sc - s, sc - s)
            u_prev = jnp.where(ok, pltpu.roll(u, shift, axis=0), 0.0)
            a_prev = jnp.where(ok, pltpu.roll(a, shift, axis=0), 1.0)
            u = u + a * u_prev
            a = a * a_prev
            s *= 2
        h = u + a * carry
        hs[pl.ds(r0, sc), :] += h
        return h[sc - 1:sc, :] if d == 0 else h[0:1, :]

    def body(t, carries):
        fwd = scan_chunk(0, pl.multiple_of(t * sc, sc), carries[0])
        back = jnp.where(t == 0, 0, n_chunks - t)
        bwd = scan_chunk(1, pl.multiple_of(back * sc, sc), carries[1])
        return fwd, bwd

    zero = jnp.zeros((1, LANES), F32)
    lax.fori_loop(0, n_chunks, body, (zero, zero))

    oc_ref[...] = (_gelu_tanh(g_c[...]) * hs[0:CTX_LEN, :]).astype(oc_ref.dtype)
    for r in range(SEQ // sc):
        rows = slice(r * sc, (r + 1) * sc)
        ox_ref[rows, :] = (_gelu_tanh(g_x[rows, :]) * hs[CTX_LEN + r * sc:CTX_LEN + (r + 1) * sc, :]
                           ).astype(ox_ref.dtype)


def _rglru(z_c, z_x, conv_w, conv_b, w_r, b_r, w_i, b_i, lam, *, batch):
    groups = (G_DX, G_DG)
    c_specs, x_specs = _z_specs(groups, LANES)
    out_specs, out_shapes = _mix_out(batch, LANES)
    vec2 = pl.BlockSpec((2, LANES), lambda b, p: (0, p))
    wspec = pl.BlockSpec((2, LANES, LANES), lambda b, p: (0, p, p))
    return pl.pallas_call(
        _rglru_kernel,
        grid=(batch, SLABS),
        in_specs=c_specs + x_specs + [
            pl.BlockSpec((4, LANES), lambda b, p: (0, p)),
            pl.BlockSpec((1, LANES), lambda b, p: (0, p)),
            wspec, vec2, wspec, vec2, vec2,
        ],
        out_specs=out_specs,
        out_shape=out_shapes,
        scratch_shapes=[pltpu.VMEM((SEQ + 2 * CONV_PAD, LANES), F32),
                        pltpu.VMEM((CTX_LEN + SEQ, LANES), F32),
                        pltpu.VMEM((CTX_LEN + SEQ, LANES), F32)],
        compiler_params=_cparams(("parallel", "parallel")),
        name="rglru",
    )(z_c, z_c, z_x, z_x, conv_w, conv_b, w_r, b_r, w_i, b_i, lam)


def _block_diag_dense(w):
    two, nb, bs, _ = w.shape
    eye = jnp.eye(nb, dtype=w.dtype)
    return jnp.einsum("dgio,gh->dgiho", w, eye).reshape(two, nb * bs, nb * bs)


def kernel(x, c, ctx, c_ctx, w_ada, b_ada, norm_pre, norm_post, ffn_w_in, ffn_w_out, w_in, w_out,
           lb_logits, a_norm, b_norm, c_lambda, c_norm, d_conv_w, d_conv_b, d_w_r, d_b_r, d_w_i,
           d_b_i, d_lambda):
    batch = x.shape[0]
    assert x.shape == (batch, SEQ, D_MODEL) and ctx.shape == (batch, CTX_LEN, D_MODEL)
    assert batch < MOD_ROWS

    gla_consts = _gla_constants()
    ret_consts = _retention_constants()
    rope = _rope_constants()

    c_rows = jnp.zeros((MOD_ROWS, D_MODEL), F32).at[:batch].set(c).at[batch].set(c_ctx)
    mod = _ada_table(c_rows, w_ada, b_ada).reshape(DEPTH * MOD_ROWS, N_MOD, D_MODEL)

    n_c = batch * CTX_LEN
    tm_ffn, tm_proj, tm_out = 1024, 1024, 512
    tm_c = min(n_c, 1024)
    assert n_c % tm_c == 0
    row_x = lambda tm: (lambda i: i // (SEQ // tm))
    row_c = lambda i: batch

    hx = x.reshape(batch * SEQ, D_MODEL)
    hc = ctx.reshape(n_c, D_MODEL)
    tile2 = lambda g: jnp.tile(g.reshape(1, -1), (1, LANES // g.shape[-1]))
    w_out_b = w_out.astype(BF16)

    def ffn(h, l, idx, is_ctx, weights):
        return _ffn(h, mod, norm_pre[l, 2 * idx].reshape(1, D_MODEL), norm_post[l, 2 * idx].reshape(1, D_MODEL),
                    weights, layer=l, idx=idx, k0=6 * idx, emit=is_ctx,
                    row_of_tile=row_c if is_ctx else row_x(tm_ffn), tm=tm_c if is_ctx else tm_ffn)

    for l in range(DEPTH):
        ctx_out = l < DEPTH - 1
        gpre1 = norm_pre[l, 1].reshape(1, D_MODEL)
        gpost1 = norm_post[l, 1].reshape(1, D_MODEL)

        hc, ffn_w = ffn(hc, l, 0, True, (ffn_w_in, ffn_w_out))
        hx = ffn(hx, l, 0, False, ffn_w)

        z_c, w_in_l = _inproj(hc, mod, gpre1, w_in, layer=l, row_of_tile=row_c, tm=tm_c, emit=True)
        z_x = _inproj(hx, mod, gpre1, w_in_l, layer=l, row_of_tile=row_x(tm_proj), tm=tm_proj, emit=False)

        a_c, a_x = _hgrn2(z_c, z_x, lb_logits, tile2(a_norm[l]), gla_consts, layer=l, batch=batch, ctx_out=ctx_out)
        b_c, b_x = _retention(z_c, z_x, tile2(b_norm[l]), rope, ret_consts, batch=batch, ctx_out=ctx_out)

        lam_init = 0.8 - 0.6 * math.exp(-0.3 * l)
        cgain = c_norm[l].reshape(1, LANES)
        c_x = _diff_attn(z_x, z_c, z_x, rope, c_lambda[l], cgain, lam_init=lam_init, batch=batch, tq=256)

        d_c, d_x = _rglru(z_c, z_x, d_conv_w[l], d_conv_b[l].reshape(1, -1),
                          _block_diag_dense(d_w_r[l]).astype(BF16), d_b_r[l],
                          _block_diag_dense(d_w_i[l]).astype(BF16), d_b_i[l], d_lambda[l], batch=batch)

        hx = _outproj((a_x, b_x, c_x, d_x), w_out_b[l], hx, mod, gpost1, layer=l,
                      row_of_tile=row_x(tm_out), tm=tm_out)

        if ctx_out:
            c_c = _diff_attn(z_c, z_c, None, rope, c_lambda[l], cgain, lam_init=lam_init, batch=batch, tq=CTX_LEN)
            hc = _outproj((a_c, b_c, c_c, d_c), w_out_b[l], hc, mod, gpost1, layer=l,
                          row_of_tile=row_c, tm=min(tm_c, tm_out))
            hc, ffn_w = ffn(hc, l, 1, True, (ffn_w_in, ffn_w_out))
        else:
            ffn_w = _cast_ffn_weights(ffn_w_in, ffn_w_out, layer=l, idx=1)
        hx = ffn(hx, l, 1, False, ffn_w)

    return hx.reshape(batch, SEQ, D_MODEL)
```

```python
import functools
import math

import numpy as np
import jax
import jax.numpy as jnp
from jax import lax
from jax.experimental import pallas as pl
from jax.experimental.pallas import tpu as pltpu

F32 = jnp.float32
BF16 = jnp.bfloat16

D_MODEL = 2048
SEQ = 2048
CTX_LEN = 256
DEPTH = 2
GRID_W = 64
HEAD_DIM = 64
GROUP_WIDTH = 512
IN_WIDTH = 7168
D_FF = 5504
N_MOD = 9
ROPE_BASE = 10000.0
EPS = 1e-6
RG_C = 8.0
FFN_RESIDUAL = 0.5

LANES = 128
FF_TILE = 512
FF_TILE_EMIT = 256
MOD_ROWS = 8
GLA_CHUNK = 128
HGRN2_WIDTH = 512
RET_WIDTH = 512
GLA_LEVELS = 7
GLA_COARSE_LEVELS = 4
SCAN_CHUNK = 256
VMEM_LIMIT = 56 * 1024 * 1024

G_AQ, G_AFF, G_AFB, G_AV, G_AG = 0, 1, 2, 3, 4
G_BQ, G_BK, G_BV, G_BG = 5, 6, 7, 8
G_CQ, G_CK, G_CV = 9, 10, 11
G_DX, G_DG = 12, 13
SLABS = GROUP_WIDTH // LANES


def _cparams(sem):
    return pltpu.CompilerParams(dimension_semantics=sem, vmem_limit_bytes=VMEM_LIMIT)


def _sigmoid(t):
    return jax.nn.sigmoid(t)


def _silu(t):
    return t * _sigmoid(t)


def _rms(t, gain):
    return t * lax.rsqrt(jnp.mean(t * t, axis=-1, keepdims=True) + EPS) * gain


def _dot(a, b):
    return jnp.dot(a, b, preferred_element_type=F32)


def _dot_nt(a, b):
    return lax.dot_general(a, b, (((1,), (1,)), ((), ())), preferred_element_type=F32)


def _ada_kernel(c_ref, w_ref, b_ref, o_ref):
    s = _silu(c_ref[...])
    s_hi = s.astype(BF16).astype(F32)
    s_mid = (s - s_hi).astype(BF16).astype(F32)
    s_lo = s - s_hi - s_mid
    w = w_ref[...]
    w_hi = w.astype(BF16)
    w_lo = (w - w_hi.astype(F32)).astype(BF16)
    r_hi = _dot(jnp.concatenate([s_hi, s_mid, s_lo], axis=0).astype(BF16), w_hi)
    r_lo = _dot(jnp.concatenate([s_hi, s_mid], axis=0).astype(BF16), w_lo)
    m = MOD_ROWS
    o_ref[...] = (r_hi[0:m] + r_hi[m:2 * m] + r_hi[2 * m:3 * m]) + (r_lo[0:m] + r_lo[m:2 * m]) + b_ref[...]


def _ada_table(c_rows, w_ada, b_ada):
    tn = 1024
    n = N_MOD * D_MODEL
    return pl.pallas_call(
        _ada_kernel,
        grid=(DEPTH, n // tn),
        in_specs=[
            pl.BlockSpec((MOD_ROWS, D_MODEL), lambda l, j: (0, 0)),
            pl.BlockSpec((None, D_MODEL, tn), lambda l, j: (l, 0, j)),
            pl.BlockSpec((None, 1, tn), lambda l, j: (l, 0, j)),
        ],
        out_specs=pl.BlockSpec((None, MOD_ROWS, tn), lambda l, j: (l, 0, j)),
        out_shape=jax.ShapeDtypeStruct((DEPTH, MOD_ROWS, n), F32),
        compiler_params=_cparams(("arbitrary", "arbitrary")),
        name="ada_table",
    )(c_rows, w_ada, b_ada.reshape(DEPTH, 1, n))


ROW_CHUNK = 256
MM_ROWS = 512
MM_COLS = 512
OUT_ROWS = 256


def _norm_modulate_to(u_scr, h_ref, gain_ref, mod_ref, k_shift, k_scale):
    tm = h_ref.shape[0]
    shift = mod_ref[k_shift:k_shift + 1, :]
    gain_scale = gain_ref[...] * (1.0 + mod_ref[k_scale:k_scale + 1, :])

    def body(r, carry):
        r0 = pl.multiple_of(r * ROW_CHUNK, ROW_CHUNK)
        h = h_ref[pl.ds(r0, ROW_CHUNK), :]
        u_scr[pl.ds(r0, ROW_CHUNK), :] = (_rms(h, gain_scale) + shift).astype(BF16)
        return carry

    lax.fori_loop(0, tm // ROW_CHUNK, body, 0)


def _cast_ffn_tiles(w_refs, out_refs, j, last):
    tf = w_refs[0].shape[1]
    overlap = pl.cdiv(D_FF, tf) * tf - D_FF
    tiles = [r[...].astype(BF16) for r in w_refs]
    for t, out, axis in zip(tiles, out_refs, (1, 1, 0)):
        if overlap:
            keep = lax.slice_in_dim(t, overlap, tf, axis=axis)
            fill = jnp.zeros_like(lax.slice_in_dim(t, 0, overlap, axis=axis))
            t = jnp.where(j == last, jnp.concatenate([keep, fill], axis=axis), t)
        out[...] = t
    return tiles


def _ffn_kernel(h_ref, mod_ref, gpre_ref, gpost_ref, wg_ref, wu_ref, wout_ref, *rest, k0, emit):
    if emit:
        o_ref, wg_out, wu_out, wout_out, u_scr = rest
    else:
        o_ref, u_scr = rest
    j = pl.program_id(1)
    last = pl.num_programs(1) - 1
    tm = h_ref.shape[0]
    tf = wg_ref.shape[1]
    overlap = pl.cdiv(D_FF, tf) * tf - D_FF

    @pl.when(j == 0)
    def _():
        _norm_modulate_to(u_scr, h_ref, gpre_ref, mod_ref, k0, k0 + 1)
        o_ref[...] = jnp.zeros_like(o_ref)

    if emit:
        wg, wu, wout = _cast_ffn_tiles((wg_ref, wu_ref, wout_ref), (wg_out, wu_out, wout_out), j, last)
    else:
        wg, wu, wout = wg_ref[...], wu_ref[...], wout_ref[...]

    col = lax.broadcasted_iota(jnp.int32, (1, tf), 1)
    fresh = jnp.logical_or(j < last, col >= overlap)
    for r in range(tm // MM_ROWS):
        rows = slice(r * MM_ROWS, (r + 1) * MM_ROWS)
        u = u_scr[rows, :]
        act = jnp.where(fresh, _silu(_dot(u, wg)) * _dot(u, wu), 0.0).astype(BF16)
        for n0 in range(0, D_MODEL, MM_COLS):
            o_ref[rows, n0:n0 + MM_COLS] += _dot(act, wout[:, n0:n0 + MM_COLS])

    @pl.when(j == last)
    def _():
        gate_gain = FFN_RESIDUAL * mod_ref[k0 + 2:k0 + 3, :] * gpost_ref[...]

        def body(r, carry):
            r0 = pl.multiple_of(r * ROW_CHUNK, ROW_CHUNK)
            y = o_ref[pl.ds(r0, ROW_CHUNK), :]
            o_ref[pl.ds(r0, ROW_CHUNK), :] = h_ref[pl.ds(r0, ROW_CHUNK), :] + _rms(y, gate_gain)
            return carry

        lax.fori_loop(0, tm // ROW_CHUNK, body, 0)


def _mod_spec(layer, row_of_tile):
    return pl.BlockSpec((None, N_MOD, D_MODEL), lambda i, *_: (layer * MOD_ROWS + row_of_tile(i), 0, 0))


def _ff_offset(j, tf, base=0):
    return (base // LANES + jnp.minimum(j * (tf // LANES), (D_FF - tf) // LANES)) * LANES


def _ffn_f32_weight_specs(layer, idx, tf):
    el = pl.Element
    in_specs = [
        pl.BlockSpec((None, None, el(D_MODEL), el(tf)), lambda *g: (layer, idx, 0, _ff_offset(g[-1], tf))),
        pl.BlockSpec((None, None, el(D_MODEL), el(tf)), lambda *g: (layer, idx, 0, _ff_offset(g[-1], tf, D_FF))),
        pl.BlockSpec((None, None, el(tf), el(D_MODEL)), lambda *g: (layer, idx, _ff_offset(g[-1], tf), 0)),
    ]
    d_ff_pad = pl.cdiv(D_FF, tf) * tf
    cols_out = pl.BlockSpec((D_MODEL, tf), lambda *g: (0, g[-1]))
    out_specs = [cols_out, cols_out, pl.BlockSpec((tf, D_MODEL), lambda *g: (g[-1], 0))]
    out_shape = [jax.ShapeDtypeStruct((D_MODEL, d_ff_pad), BF16), jax.ShapeDtypeStruct((D_MODEL, d_ff_pad), BF16),
                 jax.ShapeDtypeStruct((d_ff_pad, D_MODEL), BF16)]
    return in_specs, out_specs, out_shape


def _ffn_weight_cast_kernel(wg_ref, wu_ref, wout_ref, wg_out, wu_out, wout_out):
    _cast_ffn_tiles((wg_ref, wu_ref, wout_ref), (wg_out, wu_out, wout_out), pl.program_id(0), pl.num_programs(0) - 1)


def _cast_ffn_weights(w_in, w_out, *, layer, idx):
    tf = FF_TILE
    in_specs, out_specs, out_shape = _ffn_f32_weight_specs(layer, idx, tf)
    return tuple(pl.pallas_call(
        _ffn_weight_cast_kernel,
        grid=(pl.cdiv(D_FF, tf),),
        in_specs=in_specs,
        out_specs=out_specs,
        out_shape=out_shape,
        compiler_params=_cparams(("arbitrary",)),
        name="ffn_weight_cast",
    )(w_in, w_in, w_out))


def _ffn(h, mod, gpre, gpost, weights, *, layer, idx, k0, row_of_tile, tm, emit):
    n = h.shape[0]
    el = pl.Element
    tf = FF_TILE_EMIT if emit else FF_TILE
    col_tile = pl.BlockSpec((el(D_MODEL), el(tf)), lambda i, j: (0, _ff_offset(j, tf)))
    row_tile = pl.BlockSpec((el(tf), el(D_MODEL)), lambda i, j: (_ff_offset(j, tf), 0))
    h_spec = pl.BlockSpec((tm, D_MODEL), lambda i, j: (i, 0))
    out_specs, out_shape = h_spec, jax.ShapeDtypeStruct((n, D_MODEL), F32)
    if emit:
        assert n == tm, "weight tiles are cast once: one row tile only"
        w_in, w_out = weights
        w_args = (w_in, w_in, w_out)
        w_specs, w_out_specs, w_out_shape = _ffn_f32_weight_specs(layer, idx, tf)
        out_specs = [h_spec] + w_out_specs
        out_shape = [out_shape] + w_out_shape
    else:
        w_args = weights
        w_specs = [col_tile, col_tile, row_tile]
    out = pl.pallas_call(
        functools.partial(_ffn_kernel, k0=k0, emit=emit),
        grid=(n // tm, pl.cdiv(D_FF, tf)),
        in_specs=[
            h_spec,
            _mod_spec(layer, row_of_tile),
            pl.BlockSpec((1, D_MODEL), lambda i, j: (0, 0)),
            pl.BlockSpec((1, D_MODEL), lambda i, j: (0, 0)),
        ] + w_specs,
        out_specs=out_specs,
        out_shape=out_shape,
        scratch_shapes=[pltpu.VMEM((tm, D_MODEL), BF16)],
        compiler_params=_cparams(("parallel", "arbitrary")),
        name="ffn_cast" if emit else "ffn",
    )(h, mod, gpre, gpost, *w_args)
    return (out[0], tuple(out[1:])) if emit else out


def _inproj_kernel(h_ref, mod_ref, gpre_ref, w_ref, *rest, emit):
    if emit:
        z_ref, w_out, u_scr = rest
    else:
        z_ref, u_scr = rest
    j = pl.program_id(1)
    tm = h_ref.shape[0]

    @pl.when(j == 0)
    def _():
        _norm_modulate_to(u_scr, h_ref, gpre_ref, mod_ref, 3, 4)

    w = w_ref[...]
    if emit:
        w = w.astype(BF16)
        w_out[...] = w
    for r in range(tm // MM_ROWS):
        rows = slice(r * MM_ROWS, (r + 1) * MM_ROWS)
        z_ref[rows, :] = _dot(u_scr[rows, :], w)


def _inproj(h, mod, gpre, w, *, layer, row_of_tile, tm, emit):
    n = h.shape[0]
    tn = 512 if emit else 1024
    if emit:
        assert n == tm, "weight tiles are cast once: one row tile only"
        w_spec = pl.BlockSpec((None, D_MODEL, tn), lambda i, j: (layer, 0, j))
    else:
        w_spec = pl.BlockSpec((D_MODEL, tn), lambda i, j: (0, j))
    z_spec = pl.BlockSpec((tm, tn), lambda i, j: (i, j))
    z_shape = jax.ShapeDtypeStruct((n, IN_WIDTH), F32)
    out = pl.pallas_call(
        functools.partial(_inproj_kernel, emit=emit),
        grid=(n // tm, IN_WIDTH // tn),
        in_specs=[
            pl.BlockSpec((tm, D_MODEL), lambda i, j: (i, 0)),
            _mod_spec(layer, row_of_tile),
            pl.BlockSpec((1, D_MODEL), lambda i, j: (0, 0)),
            w_spec,
        ],
        out_specs=[z_spec, pl.BlockSpec((D_MODEL, tn), lambda i, j: (0, j))] if emit else z_spec,
        out_shape=[z_shape, jax.ShapeDtypeStruct((D_MODEL, IN_WIDTH), BF16)] if emit else z_shape,
        scratch_shapes=[pltpu.VMEM((tm, D_MODEL), BF16)],
        compiler_params=_cparams(("parallel", "arbitrary")),
        name="inproj_cast" if emit else "inproj",
    )(h, mod, gpre, w)
    return tuple(out) if emit else out


def _outproj_kernel(a_ref, b_ref, c_ref, d_ref, w_ref, h_ref, mod_ref, gpost_ref, o_ref):
    tm = h_ref.shape[0]
    gate_gain = mod_ref[5:6, :] * gpost_ref[...]
    for r in range(tm // OUT_ROWS):
        rows = slice(r * OUT_ROWS, (r + 1) * OUT_ROWS)
        mix = jnp.concatenate([a_ref[rows, :], b_ref[rows, :], c_ref[rows, :], d_ref[rows, :]], axis=1)
        o_ref[rows, :] = h_ref[rows, :] + _rms(_dot(mix, w_ref[...]), gate_gain)


def _outproj(mix, w, h, mod, gpost, *, layer, row_of_tile, tm):
    n = h.shape[0]
    mix_spec = pl.BlockSpec((tm, GROUP_WIDTH), lambda i: (i, 0))
    return pl.pallas_call(
        _outproj_kernel,
        grid=(n // tm,),
        in_specs=[
            mix_spec, mix_spec, mix_spec, mix_spec,
            pl.BlockSpec((D_MODEL, D_MODEL), lambda i: (0, 0)),
            pl.BlockSpec((tm, D_MODEL), lambda i: (i, 0)),
            _mod_spec(layer, row_of_tile),
            pl.BlockSpec((1, D_MODEL), lambda i: (0, 0)),
        ],
        out_specs=pl.BlockSpec((tm, D_MODEL), lambda i: (i, 0)),
        out_shape=jax.ShapeDtypeStruct((n, D_MODEL), F32),
        compiler_params=_cparams(("parallel",)),
        name="outproj",
    )(*mix, w, h, mod, gpost)


def _gla_constants():
    c = GLA_CHUNK
    idx = np.arange(c)
    i, r = idx[:, None], idx[None, :]
    mats = np.zeros((2, GLA_LEVELS + 2, c, c), np.float32)
    masks = np.zeros((2, GLA_LEVELS + 1, c, c), np.float32)
    mats[0, 0] = r <= i
    mats[0, 1] = r > i
    mats[1, 0] = r >= i
    mats[1, 1] = r < i
    for m in range(GLA_LEVELS):
        hs = c >> (m + 1)
        start = (idx // (2 * hs)) * (2 * hs)
        mid = (start + hs - 1)[:, None]
        right = ((idx // hs) % 2 == 1)[:, None]
        mats[0, 2 + m] = np.where(right, (r > mid) & (r <= i), (r > i) & (r <= mid))
        mats[1, 2 + m] = np.where(right, (r > mid) & (r < i), (r >= i) & (r <= mid))
        same = (start[:, None] == start[None, :])
        q_right = right & ~right.T & same
        masks[0, m] = q_right
        masks[1, m] = q_right.T
    masks[0, GLA_LEVELS] = np.eye(c)
    masks[1, GLA_LEVELS] = np.eye(c)
    mats = mats.reshape(2, (GLA_LEVELS + 2) * c, c)
    mats = np.concatenate([mats, mats], axis=-1)
    masks = np.tile(masks, (1, 1, 1, 2))
    masks_q = np.stack([np.stack([masks[d, m][_gla_query_rows(m, d)] for m in range(GLA_COARSE_LEVELS)])
                        for d in range(2)])
    return jnp.asarray(mats, BF16), jnp.asarray(masks, F32), jnp.asarray(masks_q, F32)


def _gla_query_rows(level, d):
    hs = GLA_CHUNK >> (level + 1)
    idx = np.arange(GLA_CHUNK)
    return idx[(idx // hs) % 2 == (1 if d == 0 else 0)]


def _retention_constants():
    c = GLA_CHUNK
    heads = GROUP_WIDTH // HEAD_DIM
    log_decay = np.log1p(-np.exp2(-5.0 - np.arange(heads, dtype=np.float64)))
    ld_lane = np.repeat(log_decay, HEAD_DIM).reshape(SLABS, 1, LANES)
    idx = np.arange(c, dtype=np.float64)
    i, j = idx[:, None], idx[None, :]
    dq = np.zeros((SLABS, 2, c, LANES))
    dk = np.zeros((SLABS, 2, c, LANES))
    dq[:, 0] = np.exp((idx[None, :, None] + 1.0) * ld_lane)
    dk[:, 0] = np.exp((c - 1.0 - idx[None, :, None]) * ld_lane)
    dq[:, 1] = np.exp((c - idx[None, :, None]) * ld_lane)
    dk[:, 1] = np.exp(idx[None, :, None] * ld_lane)
    gt = np.exp(c * ld_lane)
    dm = np.zeros((SLABS, 2, 2, c, c))
    for p in range(SLABS):
        for hh in range(2):
            ld = log_decay[2 * p + hh]
            dm[p, 0, hh] = np.where(i >= j, np.exp((i - j) * ld), 0.0)
            dm[p, 1, hh] = np.where(j >= i, np.exp((j - i) * ld), 0.0)
    dm = dm.transpose(0, 1, 3, 2, 4).reshape(SLABS, 2, c, 2 * c)
    f = lambda t: jnp.asarray(t.astype(np.float32))
    return f(dq), f(dk), f(gt), f(dm)


def _rope_constants():
    quarter = HEAD_DIM // 4
    inv_freq = ROPE_BASE ** (-jnp.arange(quarter, dtype=F32) / quarter)
    rows = SEQ // GRID_W
    row = jnp.repeat(jnp.arange(rows, dtype=F32), GRID_W)
    col = jnp.tile(jnp.arange(GRID_W, dtype=F32), rows)
    ang = jnp.concatenate([row[:, None] * inv_freq, col[:, None] * inv_freq], axis=-1)
    cos, sin = jnp.cos(ang), jnp.sin(ang)
    reps = LANES // HEAD_DIM
    return (jnp.tile(jnp.concatenate([cos, cos], axis=-1), (1, reps)),
            jnp.tile(jnp.concatenate([-sin, sin], axis=-1), (1, reps)))


def _rope(t, cos, sin):
    lane = lax.broadcasted_iota(jnp.int32, t.shape, 1)
    first_half = (lane % HEAD_DIM) < (HEAD_DIM // 2)
    partner = jnp.where(first_half,
                        pltpu.roll(t, LANES - HEAD_DIM // 2, axis=1),
                        pltpu.roll(t, HEAD_DIM // 2, axis=1))
    return t * cos + partner * sin


def _head_lane_masks():
    lane = lax.broadcasted_iota(jnp.int32, (1, LANES), 1)
    return lane < HEAD_DIM, lane >= HEAD_DIM


def _stack_heads(t):
    hm0, hm1 = _head_lane_masks()
    return jnp.concatenate([jnp.where(hm0, t, 0.0).astype(BF16),
                            jnp.where(hm1, t, 0.0).astype(BF16)], axis=0)


def _gla_state_update(ke, v, gtot, st):
    upd = lax.dot_general(v.astype(BF16), ke.astype(BF16), (((0,), (0,)), ((), ())),
                          preferred_element_type=F32)
    row_head = lax.broadcasted_iota(jnp.int32, (LANES, LANES), 0) // HEAD_DIM
    col_head = lax.broadcasted_iota(jnp.int32, (LANES, LANES), 1) // HEAD_DIM
    return st * gtot + jnp.where(row_head == col_head, upd, 0.0)


def _gla_chunk_update(qe, ke, v, gtot, s_cat, st):
    o = _dot_nt(qe.astype(BF16), st.astype(BF16))
    o += _dot(s_cat.astype(BF16), _stack_heads(v))
    return o, _gla_state_update(ke, v, gtot, st)


def _slab(t, s):
    return t[:, s * LANES:(s + 1) * LANES]


def _gla_finish_chunk(acc_ref, zg_ref, gain, o_ref, r0):
    hm0, _ = _head_lane_masks()
    for s in range(acc_ref.shape[1] // LANES):
        cols = slice(s * LANES, (s + 1) * LANES)
        o = acc_ref[pl.ds(r0, GLA_CHUNK), cols]
        sq = o * o
        ms0 = jnp.sum(jnp.where(hm0, sq, 0.0), axis=1, keepdims=True) * (1.0 / HEAD_DIM)
        ms1 = jnp.sum(jnp.where(hm0, 0.0, sq), axis=1, keepdims=True) * (1.0 / HEAD_DIM)
        inv = jnp.where(hm0, lax.rsqrt(ms0 + EPS), lax.rsqrt(ms1 + EPS))
        g = zg_ref[pl.ds(r0, GLA_CHUNK), cols]
        o_ref[pl.ds(r0, GLA_CHUNK), cols] = (o * inv * gain * _silu(g)).astype(o_ref.dtype)


def _gla_scan(chunk_fn, accs, zgs, gain, outs, ctx_out):
    n_slabs = accs[1].shape[1] // LANES
    if ctx_out:
        accs[0][...] = jnp.zeros_like(accs[0])
    else:
        outs[0][...] = jnp.zeros_like(outs[0])
    accs[1][...] = jnp.zeros_like(accs[1])

    def run(seg, states):
        acc_ref = accs[seg]
        n = acc_ref.shape[0] // GLA_CHUNK
        assert n % 2 == 0
        need_out = seg == 1 or ctx_out

        def body(finish):
            def step(t, states):
                new_states, visited = [], []
                for d in (0, 1):
                    r0 = pl.multiple_of((t if d == 0 else n - 1 - t) * GLA_CHUNK, GLA_CHUNK)
                    o, st = chunk_fn(d, seg, r0, states[d], need_out)
                    if need_out:
                        acc_ref[pl.ds(r0, GLA_CHUNK), :] += o
                        visited.append(r0)
                    new_states.append(st)
                if finish:
                    for r0 in visited:
                        _gla_finish_chunk(acc_ref, zgs[seg], gain, outs[seg], r0)
                return tuple(new_states)
            return step

        states = lax.fori_loop(0, n // 2, body(False), states)
        return lax.fori_loop(n // 2, n, body(True), states)

    zero = tuple(jnp.zeros((LANES, LANES), F32) for _ in range(n_slabs))
    run(1, run(0, (zero, zero)))


def _hgrn2_kernel(zq_c, zff_c, zfb_c, zv_c, zg_c, zq_x, zff_x, zfb_x, zv_x, zg_x,
                  lbl_ref, gain_ref, mats_ref, masks_ref, masks_q_ref, oc_ref, ox_ref, acc_c, acc_x,
                  *, layer, ctx_out):
    c = GLA_CHUNK
    n_slabs = acc_c.shape[1] // LANES
    lbl = lbl_ref[...]
    e = jnp.exp(lbl - jnp.max(lbl, axis=0, keepdims=True))
    sm = e / jnp.sum(e, axis=0, keepdims=True)
    lower = jnp.zeros_like(sm[0])
    for l in range(1, layer + 1):
        lower = lower + sm[l]
    zq = (zq_c, zq_x)
    zf = ((zff_c, zff_x), (zfb_c, zfb_x))
    zv = (zv_c, zv_x)
    row = lax.broadcasted_iota(jnp.int32, (c, 1), 0)

    def chunk_fn(d, seg, r0, states, need_out):
        lb = lower[d:d + 1, :]
        f = lb + (1.0 - lb) * _sigmoid(zf[d][seg][pl.ds(r0, c), :])
        lf = jnp.log(f)
        k = 1.0 - f
        v = zv[seg][pl.ds(r0, c), :]
        lf_hi = lf.astype(BF16)
        lf_lo = (lf - lf_hi.astype(F32)).astype(BF16)
        lf_cat = jnp.concatenate([lf_hi, lf_lo], axis=0)
        gtot = jnp.exp(jnp.sum(lf, axis=0, keepdims=True))
        if not need_out:
            ke = k * jnp.exp(_dot(mats_ref[d, c:2 * c, :], lf_cat))
            return None, tuple(_gla_state_update(_slab(ke, s), _slab(v, s), _slab(gtot, s), states[s])
                               for s in range(n_slabs))
        q = _silu(zq[seg][pl.ds(r0, c), :])
        ex = jnp.exp(_dot(mats_ref[d], lf_cat))
        qe = q * ex[0:c]
        ke = k * ex[c:2 * c]
        s_cats = [jnp.zeros((c, 2 * c), F32) for _ in range(n_slabs)]
        for lvl in range(GLA_LEVELS + 1):
            if lvl < GLA_LEVELS:
                hs = c >> (lvl + 1)
                q_side = ((row // hs) % 2) == (1 if d == 0 else 0)
                rhs_src = jnp.where(q_side, q, k) * ex[(2 + lvl) * c:(3 + lvl) * c]
                lhs = rhs_src.astype(BF16)
            else:
                lhs, rhs_src = q.astype(BF16), k
            if lvl < GLA_COARSE_LEVELS:
                rows_q = _gla_query_rows(lvl, d)
                starts = rows_q[::hs]
                lhs_q = jnp.concatenate([rhs_src[int(r):int(r) + hs] for r in starts], axis=0).astype(BF16)
                gap = jnp.zeros((hs, 2 * c), F32)
                for s in range(n_slabs):
                    p = masks_q_ref[d, lvl] * _dot_nt(_slab(lhs_q, s), _stack_heads(_slab(rhs_src, s)))
                    pieces = []
                    for b in range(len(starts)):
                        blk = p[b * hs:(b + 1) * hs]
                        pieces += [gap, blk] if d == 0 else [blk, gap]
                    s_cats[s] = s_cats[s] + jnp.concatenate(pieces, axis=0)
                continue
            mask = masks_ref[d, lvl]
            for s in range(n_slabs):
                s_cats[s] = s_cats[s] + mask * _dot_nt(_slab(lhs, s), _stack_heads(_slab(rhs_src, s)))
        outs = [_gla_chunk_update(_slab(qe, s), _slab(ke, s), _slab(v, s), _slab(gtot, s), s_cats[s],
                                  states[s]) for s in range(n_slabs)]
        return jnp.concatenate([o for o, _ in outs], axis=1), tuple(st for _, st in outs)

    _gla_scan(chunk_fn, (acc_c, acc_x), (zg_c, zg_x), gain_ref[...], (oc_ref, ox_ref), ctx_out)


def _z_specs(groups, width, single_buffered=()):
    per_group = GROUP_WIDTH // width
    mode = lambda g: dict(pipeline_mode=pl.Buffered(1)) if g in single_buffered else {}
    c_specs = [pl.BlockSpec((CTX_LEN, width), lambda b, p, g=g: (b, g * per_group + p)) for g in groups]
    x_specs = [pl.BlockSpec((SEQ, width), lambda b, p, g=g: (b, g * per_group + p), **mode(g)) for g in groups]
    return c_specs, x_specs


def _mix_out(batch, width):
    specs = [pl.BlockSpec((CTX_LEN, width), lambda b, p: (b, p)),
             pl.BlockSpec((SEQ, width), lambda b, p: (b, p))]
    shapes = [jax.ShapeDtypeStruct((batch * CTX_LEN, GROUP_WIDTH), BF16),
              jax.ShapeDtypeStruct((batch * SEQ, GROUP_WIDTH), BF16)]
    return specs, shapes


def _gla_scratch(width):
    return [pltpu.VMEM((CTX_LEN, width), F32), pltpu.VMEM((SEQ, width), F32)]


def _hgrn2(z_c, z_x, lb_logits, gain, consts, *, layer, batch, ctx_out):
    mats, masks, masks_q = consts
    groups = (G_AQ, G_AFF, G_AFB, G_AV, G_AG)
    width = HGRN2_WIDTH
    c_specs, x_specs = _z_specs(groups, width, single_buffered=(G_AFB, G_AV, G_AG))
    out_specs, out_shapes = _mix_out(batch, width)
    n_c = len(groups)
    return pl.pallas_call(
        functools.partial(_hgrn2_kernel, layer=layer, ctx_out=ctx_out),
        grid=(batch, GROUP_WIDTH // width),
        in_specs=c_specs + x_specs + [
            pl.BlockSpec((DEPTH, 2, width), lambda b, p: (0, 0, p)),
            pl.BlockSpec((1, LANES), lambda b, p: (0, 0)),
            pl.BlockSpec(mats.shape, lambda b, p: (0, 0, 0)),
            pl.BlockSpec(masks.shape, lambda b, p: (0, 0, 0, 0)),
            pl.BlockSpec(masks_q.shape, lambda b, p: (0, 0, 0, 0)),
        ],
        out_specs=out_specs,
        out_shape=out_shapes,
        scratch_shapes=_gla_scratch(width),
        compiler_params=_cparams(("parallel", "parallel")),
        name="hgrn2",
    )(*([z_c] * n_c), *([z_x] * n_c), lb_logits, gain, mats, masks, masks_q)


def _retention_kernel(zq_c, zk_c, zv_c, zg_c, zq_x, zk_x, zv_x, zg_x, cos_ref, sin_ref,
                      dq_ref, dk_ref, gt_ref, dm_ref, gain_ref, oc_ref, ox_ref, acc_c, acc_x, *, ctx_out):
    c = GLA_CHUNK
    n_slabs = acc_c.shape[1] // LANES
    zq, zk, zv = (zq_c, zq_x), (zk_c, zk_x), (zv_c, zv_x)

    def chunk_fn(d, seg, r0, states, need_out):
        def rotated(t):
            return _rope(t, cos_ref[pl.ds(r0, c), :], sin_ref[pl.ds(r0, c), :]) if seg == 1 else t

        outs = []
        for s in range(n_slabs):
            cols = slice(s * LANES, (s + 1) * LANES)
            k = rotated(zk[seg][pl.ds(r0, c), cols] * (HEAD_DIM ** -0.5))
            v = zv[seg][pl.ds(r0, c), cols]
            if not need_out:
                outs.append((None, _gla_state_update(k * dk_ref[s, d], v, gt_ref[s], states[s])))
                continue
            q = rotated(zq[seg][pl.ds(r0, c), cols])
            s_cat = dm_ref[s, d] * _dot_nt(q.astype(BF16), _stack_heads(k))
            outs.append(_gla_chunk_update(q * dq_ref[s, d], k * dk_ref[s, d], v, gt_ref[s], s_cat, states[s]))
        o = jnp.concatenate([o for o, _ in outs], axis=1) if need_out else None
        return o, tuple(st for _, st in outs)

    _gla_scan(chunk_fn, (acc_c, acc_x), (zg_c, zg_x), gain_ref[...], (oc_ref, ox_ref), ctx_out)


def _retention(z_c, z_x, gain, rope, consts, *, batch, ctx_out):
    dq, dk, gt, dm = consts
    cos, sin = rope
    groups = (G_BQ, G_BK, G_BV, G_BG)
    width = RET_WIDTH
    c_specs, x_specs = _z_specs(groups, width)
    out_specs, out_shapes = _mix_out(batch, width)
    n_c = len(groups)
    c = GLA_CHUNK
    ns = width // LANES
    return pl.pallas_call(
        functools.partial(_retention_kernel, ctx_out=ctx_out),
        grid=(batch, GROUP_WIDTH // width),
        in_specs=c_specs + x_specs + [
            pl.BlockSpec((SEQ, LANES), lambda b, p: (0, 0)),
            pl.BlockSpec((SEQ, LANES), lambda b, p: (0, 0)),
            pl.BlockSpec((ns, 2, c, LANES), lambda b, p: (p, 0, 0, 0)),
            pl.BlockSpec((ns, 2, c, LANES), lambda b, p: (p, 0, 0, 0)),
            pl.BlockSpec((ns, 1, LANES), lambda b, p: (p, 0, 0)),
            pl.BlockSpec((ns, 2, c, 2 * c), lambda b, p: (p, 0, 0, 0)),
            pl.BlockSpec((1, LANES), lambda b, p: (0, 0)),
        ],
        out_specs=out_specs,
        out_shape=out_shapes,
        scratch_shapes=_gla_scratch(width),
        compiler_params=_cparams(("parallel", "parallel")),
        name="retention",
    )(*([z_c] * n_c), *([z_x] * n_c), cos, sin, dq, dk, gt, dm, gain)


ATTN_FILL_ROWS = 256
ATTN_WIDTH = 512
LOG2_E = 1.4426950408889634


def _diff_attn_kernel(*refs, lam_init, with_latent):
    if with_latent:
        q_ref, kc_ref, vc_ref, kx_ref, vx_ref, cos_ref, sin_ref, lam_ref, gain_ref, o_ref, kk, vv = refs
    else:
        q_ref, kc_ref, vc_ref, lam_ref, gain_ref, o_ref, kk, vv = refs
    tq, width = q_ref.shape
    n_heads = width // LANES

    @pl.when(pl.program_id(2) == 0)
    def _():
        kk[0:CTX_LEN, :] = kc_ref[...].astype(BF16)
        vv[0:CTX_LEN, :] = vc_ref[...].astype(BF16)
        if with_latent:
            for r0 in range(0, SEQ, ATTN_FILL_ROWS):
                rows = slice(r0, r0 + ATTN_FILL_ROWS)
                dst = slice(CTX_LEN + r0, CTX_LEN + r0 + ATTN_FILL_ROWS)
                vv[dst, :] = vx_ref[rows, :].astype(BF16)
                for h in range(n_heads):
                    cols = slice(h * LANES, (h + 1) * LANES)
                    kk[dst, cols] = _rope(kx_ref[rows, cols], cos_ref[rows, :], sin_ref[rows, :]).astype(BF16)

    lv = lam_ref[...]
    lam = (jnp.exp(jnp.sum(lv[0:1] * lv[1:2], axis=1, keepdims=True))
           - jnp.exp(jnp.sum(lv[2:3] * lv[3:4], axis=1, keepdims=True)) + lam_init)
    lane = lax.broadcasted_iota(jnp.int32, (1, LANES), 1)
    gain = gain_ref[...]
    if with_latent:
        q0 = pl.multiple_of(pl.program_id(2) * tq, tq)
        cos, sin = cos_ref[pl.ds(q0, tq), :], sin_ref[pl.ds(q0, tq), :]

    for h in range(n_heads):
        cols = slice(h * LANES, (h + 1) * LANES)
        q = q_ref[:, cols]
        if with_latent:
            q = _rope(q, cos, sin)
        q = (q * (HEAD_DIM ** -0.5 * LOG2_E)).astype(BF16)
        keys = kk[:, cols]

        def weights(qh):
            s = _dot_nt(qh, keys)
            p = jnp.exp2(s - jnp.max(s, axis=-1, keepdims=True))
            return p, jnp.sum(p, axis=-1, keepdims=True)

        zero = jnp.zeros_like(q)
        p1, l1 = weights(jnp.where(lane < HEAD_DIM, q, zero))
        p2, l2 = weights(jnp.where(lane < HEAD_DIM, zero, q))
        a = p1 - (lam * l1 / l2) * p2
        o = _dot(a.astype(BF16), vv[:, cols]) / l1
        o_ref[:, cols] = (_rms(o, gain) * (1.0 - lam_init)).astype(o_ref.dtype)


def _diff_attn(z_q, z_c, z_x, rope, lam_vecs, gain, *, lam_init, batch, tq):
    with_latent = z_x is not None
    q_len = z_q.shape[0] // batch
    nq = q_len // tq
    width = ATTN_WIDTH
    per_group = GROUP_WIDTH // width
    n_keys = CTX_LEN + (SEQ if with_latent else 0)
    block = lambda rows, g: pl.BlockSpec((rows, width), lambda b, h, i, g=g: (b, g * per_group + h))
    qspec = pl.BlockSpec((tq, width), lambda b, h, i: (b * nq + i, G_CQ * per_group + h))
    table = pl.BlockSpec((SEQ, LANES), lambda b, h, i: (0, 0))
    in_specs = [qspec, block(CTX_LEN, G_CK), block(CTX_LEN, G_CV)]
    args = [z_q, z_c, z_c]
    if with_latent:
        in_specs += [block(SEQ, G_CK), block(SEQ, G_CV), table, table]
        args += [z_x, z_x, *rope]
    in_specs += [pl.BlockSpec((4, HEAD_DIM), lambda b, h, i: (0, 0)),
                 pl.BlockSpec((1, LANES), lambda b, h, i: (0, 0))]
    args += [lam_vecs, gain]
    return pl.pallas_call(
        functools.partial(_diff_attn_kernel, lam_init=lam_init, with_latent=with_latent),
        grid=(batch, per_group, nq),
        in_specs=in_specs,
        out_specs=pl.BlockSpec((tq, width), lambda b, h, i: (b * nq + i, h)),
        out_shape=jax.ShapeDtypeStruct((z_q.shape[0], GROUP_WIDTH), BF16),
        scratch_shapes=[pltpu.VMEM((n_keys, width), BF16), pltpu.VMEM((n_keys, width), BF16)],
        compiler_params=_cparams(("parallel", "parallel", "arbitrary")),
        name="diff_attn",
    )(*args)


CONV_PAD = 8
F32_TINY = float(np.finfo(np.float32).tiny)


def _gelu_tanh(t):
    return 0.5 * t * (1.0 + jnp.tanh(math.sqrt(2.0 / math.pi) * (t + 0.044715 * (t * t * t))))


def _softplus(t):
    return jnp.maximum(t, 0.0) + jnp.log1p(jnp.exp(-jnp.abs(t)))


def _rglru_kernel(x_c, g_c, x_x, g_x, cw_ref, cb_ref, wr_ref, br_ref, wi_ref, bi_ref, lam_ref,
                  oc_ref, ox_ref, pad, xc, hs):
    n_tok = CTX_LEN + SEQ
    sc = SCAN_CHUNK
    cw = cw_ref[...]
    cb = cb_ref[...]
    zeros = jnp.zeros((CONV_PAD, LANES), F32)

    for x_ref, base, n in ((x_c, 0, CTX_LEN), (x_x, CTX_LEN, SEQ)):
        pad[0:CONV_PAD, :] = zeros
        pad[CONV_PAD:CONV_PAD + n, :] = x_ref[...]
        pad[CONV_PAD + n:2 * CONV_PAD + n, :] = zeros
        for r in range(n // sc):
            t0 = CONV_PAD + r * sc
            acc = cb + cw[0:1] * pad[t0 - 1:t0 - 1 + sc, :]
            acc += cw[1:2] * pad[t0:t0 + sc, :]
            acc += cw[2:3] * pad[t0 + 1:t0 + 1 + sc, :]
            acc += cw[3:4] * pad[t0 + 2:t0 + 2 + sc, :]
            xc[base + r * sc:base + (r + 1) * sc, :] = acc

    row = lax.broadcasted_iota(jnp.int32, (sc, 1), 0)
    n_chunks = n_tok // sc
    neg_c_softplus = -RG_C * _softplus(-lam_ref[...])
    hs[...] = jnp.zeros_like(hs)

    def scan_chunk(d, r0, carry):
        x = xc[pl.ds(r0, sc), :]
        xb = x.astype(BF16)
        r = _sigmoid(_dot(xb, wr_ref[d]) + br_ref[d:d + 1, :])
        i = _sigmoid(_dot(xb, wi_ref[d]) + bi_ref[d:d + 1, :])
        log_a = neg_c_softplus[d:d + 1, :] * r
        a = jnp.exp(log_a)
        var = -jnp.tanh(log_a) * (a * a + 1.0)
        u = var * lax.rsqrt(jnp.maximum(var, F32_TINY)) * (i * x)
        s = 1
        while s < sc:
            ok, shift = (row >= s, s) if d == 0 else (row < sc - s, sc - s)
            u_prev = jnp.where(ok, pltpu.roll(u, shift, axis=0), 0.0)
            a_prev = jnp.where(ok, pltpu.roll(a, shift, axis=0), 1.0)
            u = u + a * u_prev
            a = a * a_prev
            s *= 2
        h = u + a * carry
        hs[pl.ds(r0, sc), :] += h
        return h[sc - 1:sc, :] if d == 0 else h[0:1, :]

    def body(t, carries):
        fwd = scan_chunk(0, pl.multiple_of(t * sc, sc), carries[0])
        back = jnp.where(t == 0, 0, n_chunks - t)
        bwd = scan_chunk(1, pl.multiple_of(back * sc, sc), carries[1])
        return fwd, bwd

    zero = jnp.zeros((1, LANES), F32)
    lax.fori_loop(0, n_chunks, body, (zero, zero))

    oc_ref[...] = (_gelu_tanh(g_c[...]) * hs[0:CTX_LEN, :]).astype(oc_ref.dtype)
    for r in range(SEQ // sc):
        rows = slice(r * sc, (r + 1) * sc)
        ox_ref[rows, :] = (_gelu_tanh(g_x[rows, :]) * hs[CTX_LEN + r * sc:CTX_LEN + (r + 1) * sc, :]
                           ).astype(ox_ref.dtype)


def _rglru(z_c, z_x, conv_w, conv_b, w_r, b_r, w_i, b_i, lam, *, batch):
    groups = (G_DX, G_DG)
    c_specs, x_specs = _z_specs(groups, LANES)
    out_specs, out_shapes = _mix_out(batch, LANES)
    vec2 = pl.BlockSpec((2, LANES), lambda b, p: (0, p))
    wspec = pl.BlockSpec((2, LANES, LANES), lambda b, p: (0, p, p))
    return pl.pallas_call(
        _rglru_kernel,
        grid=(batch, SLABS),
        in_specs=c_specs + x_specs + [
            pl.BlockSpec((4, LANES), lambda b, p: (0, p)),
            pl.BlockSpec((1, LANES), lambda b, p: (0, p)),
            wspec, vec2, wspec, vec2, vec2,
        ],
        out_specs=out_specs,
        out_shape=out_shapes,
        scratch_shapes=[pltpu.VMEM((SEQ + 2 * CONV_PAD, LANES), F32),
                        pltpu.VMEM((CTX_LEN + SEQ, LANES), F32),
                        pltpu.VMEM((CTX_LEN + SEQ, LANES), F32)],
        compiler_params=_cparams(("parallel", "parallel")),
        name="rglru",
    )(z_c, z_c, z_x, z_x, conv_w, conv_b, w_r, b_r, w_i, b_i, lam)


def _block_diag_dense(w):
    two, nb, bs, _ = w.shape
    eye = jnp.eye(nb, dtype=w.dtype)
    return jnp.einsum("dgio,gh->dgiho", w, eye).reshape(two, nb * bs, nb * bs)


def kernel(x, c, ctx, c_ctx, w_ada, b_ada, norm_pre, norm_post, ffn_w_in, ffn_w_out, w_in, w_out,
           lb_logits, a_norm, b_norm, c_lambda, c_norm, d_conv_w, d_conv_b, d_w_r, d_b_r, d_w_i,
           d_b_i, d_lambda):
    batch = x.shape[0]
    assert x.shape == (batch, SEQ, D_MODEL) and ctx.shape == (batch, CTX_LEN, D_MODEL)
    assert batch < MOD_ROWS

    gla_consts = _gla_constants()
    ret_consts = _retention_constants()
    rope = _rope_constants()

    c_rows = jnp.zeros((MOD_ROWS, D_MODEL), F32).at[:batch].set(c).at[batch].set(c_ctx)
    mod = _ada_table(c_rows, w_ada, b_ada).reshape(DEPTH * MOD_ROWS, N_MOD, D_MODEL)

    n_c = batch * CTX_LEN
    tm_ffn, tm_proj, tm_out = 1024, 1024, 512
    tm_c = min(n_c, 1024)
    assert n_c % tm_c == 0
    row_x = lambda tm: (lambda i: i // (SEQ // tm))
    row_c = lambda i: batch

    hx = x.reshape(batch * SEQ, D_MODEL)
    hc = ctx.reshape(n_c, D_MODEL)
    tile2 = lambda g: jnp.tile(g.reshape(1, -1), (1, LANES // g.shape[-1]))
    w_out_b = w_out.astype(BF16)

    def ffn(h, l, idx, is_ctx, weights):
        return _ffn(h, mod, norm_pre[l, 2 * idx].reshape(1, D_MODEL), norm_post[l, 2 * idx].reshape(1, D_MODEL),
                    weights, layer=l, idx=idx, k0=6 * idx, emit=is_ctx,
                    row_of_tile=row_c if is_ctx else row_x(tm_ffn), tm=tm_c if is_ctx else tm_ffn)

    for l in range(DEPTH):
        ctx_out = l < DEPTH - 1
        gpre1 = norm_pre[l, 1].reshape(1, D_MODEL)
        gpost1 = norm_post[l, 1].reshape(1, D_MODEL)

        hc, ffn_w = ffn(hc, l, 0, True, (ffn_w_in, ffn_w_out))
        hx = ffn(hx, l, 0, False, ffn_w)

        z_c, w_in_l = _inproj(hc, mod, gpre1, w_in, layer=l, row_of_tile=row_c, tm=tm_c, emit=True)
        z_x = _inproj(hx, mod, gpre1, w_in_l, layer=l, row_of_tile=row_x(tm_proj), tm=tm_proj, emit=False)

        a_c, a_x = _hgrn2(z_c, z_x, lb_logits, tile2(a_norm[l]), gla_consts, layer=l, batch=batch, ctx_out=ctx_out)
        b_c, b_x = _retention(z_c, z_x, tile2(b_norm[l]), rope, ret_consts, batch=batch, ctx_out=ctx_out)

        lam_init = 0.8 - 0.6 * math.exp(-0.3 * l)
        cgain = c_norm[l].reshape(1, LANES)
        c_x = _diff_attn(z_x, z_c, z_x, rope, c_lambda[l], cgain, lam_init=lam_init, batch=batch, tq=256)

        d_c, d_x = _rglru(z_c, z_x, d_conv_w[l], d_conv_b[l].reshape(1, -1),
                          _block_diag_dense(d_w_r[l]).astype(BF16), d_b_r[l],
                          _block_diag_dense(d_w_i[l]).astype(BF16), d_b_i[l], d_lambda[l], batch=batch)

        hx = _outproj((a_x, b_x, c_x, d_x), w_out_b[l], hx, mod, gpost1, layer=l,
                      row_of_tile=row_x(tm_out), tm=tm_out)

        if ctx_out:
            c_c = _diff_attn(z_c, z_c, None, rope, c_lambda[l], cgain, lam_init=lam_init, batch=batch, tq=CTX_LEN)
            hc = _outproj((a_c, b_c, c_c, d_c), w_out_b[l], hc, mod, gpost1, layer=l,
                          row_of_tile=row_c, tm=min(tm_c, tm_out))
            hc, ffn_w = ffn(hc, l, 1, True, (ffn_w_in, ffn_w_out))
        else:
            ffn_w = _cast_ffn_weights(ffn_w_in, ffn_w_out, layer=l, idx=1)
        hx = ffn(hx, l, 1, False, ffn_w)

    return hx.reshape(batch, SEQ, D_MODEL)
```
